```python
import math
import jax
import jax.numpy as jnp
from jax import lax
import numpy as np

D_MODEL = 1024
BATCH = 16
SEQ = 4096
DEPTH = 4
DEC_BATCH = 8
DEC_SEQ = 32
PAST_LEN = 4096

CHUNK = 64
Q_BLOCK = 128
N_META = 16
RMS_EPS = 1e-6
NEG_INF = -1e30
HEAD_DIM = 64
GDN_HEADS = D_MODEL // 128
GDN_DK = 128
GDN_DV = 128
GDN_QKV = GDN_HEADS * (2 * GDN_DK + GDN_DV)
CONV_W = 4
SSM_WIDTH = D_MODEL // 2
SSM_GROUP = 16
SSM_GROUPS = SSM_WIDTH // SSM_GROUP
SSM_STATE = 64
FOX_HEADS = D_MODEL // 128
DIFF_HEADS = D_MODEL // 256
DIFF_V = 2 * HEAD_DIM
ROT_DIM = HEAD_DIM // 4
ROPE_THETA = 500000.0
FFN_HIDDEN = ((8 * D_MODEL + 3 * 256 - 1) // (3 * 256)) * 256
EVEN_PROJ_SIZES = (GDN_QKV, GDN_HEADS * GDN_DV, GDN_HEADS, GDN_HEADS, SSM_WIDTH)
EVEN_PROJ = sum(EVEN_PROJ_SIZES)
EVEN_MIX = GDN_HEADS * GDN_DV + SSM_WIDTH
ODD_PROJ_SIZES = (FOX_HEADS * HEAD_DIM,) * 3 + (FOX_HEADS,) + (2 * DIFF_HEADS * HEAD_DIM,) * 2 + (DIFF_HEADS * DIFF_V,)
ODD_PROJ = sum(ODD_PROJ_SIZES)
ODD_MIX = FOX_HEADS * HEAD_DIM + DIFF_HEADS * DIFF_V

kernel_name = 'hybrid_streaming_encoder_step'


def split_cols(a, sizes):
    cuts = [int(c) for c in np.cumsum(sizes)[:-1]]
    return jnp.split(a, cuts, axis=-1)


def rms_norm(x, gain):
    xf = x.astype(jnp.float32)
    y = xf * lax.rsqrt(jnp.mean(xf * xf, axis=-1, keepdims=True) + RMS_EPS)
    return (y * gain.astype(jnp.float32)).astype(x.dtype)


def l2_normalize(x):
    return x * lax.rsqrt(jnp.sum(x * x, axis=-1, keepdims=True) + 1e-6)


def swiglu(h, w1, w3, w2):
    return (jax.nn.silu(h @ w1) * (h @ w3)) @ w2


def partial_rope(x, pos):
    half = ROT_DIM // 2
    inv_freq = ROPE_THETA ** (-jnp.arange(0, ROT_DIM, 2, dtype=jnp.float32) / ROT_DIM)
    ang = pos.astype(jnp.float32)[:, None] * inv_freq[None, :]
    cos = jnp.cos(ang)[None, :, None, :]
    sin = jnp.sin(ang)[None, :, None, :]
    xr = x[..., :ROT_DIM].astype(jnp.float32)
    x1, x2 = xr[..., :half], xr[..., half:]
    rot = jnp.concatenate([x1 * cos - x2 * sin, x2 * cos + x1 * sin], axis=-1).astype(x.dtype)
    return jnp.concatenate([rot, x[..., ROT_DIM:]], axis=-1)


def causal_short_conv(x, buf, w):
    T = x.shape[1]
    xp = jnp.concatenate([buf.astype(x.dtype), x], axis=1)
    y = sum(xp[:, j:j + T] * w[j] for j in range(CONV_W))
    return jax.nn.silu(y), xp[:, T:]


def gated_delta_rule(q, k, v, beta, logg, s0):
    bsz, T, H, _ = q.shape
    dv = v.shape[-1]
    n = -(-T // CHUNK)
    pad = n * CHUNK - T

    def blocks(a):
        a = a.reshape(a.shape[:3] + (-1,))
        a = jnp.pad(a, ((0, 0), (0, pad), (0, 0), (0, 0)))
        return jnp.transpose(a.reshape(bsz, n, CHUNK, H, -1), (1, 0, 3, 2, 4))

    qc, kc, vc = blocks(q), blocks(k), blocks(v)
    bc = blocks(beta)[..., 0]
    gc = jnp.cumsum(blocks(logg)[..., 0], axis=-1)
    causal = jnp.tril(jnp.ones((CHUNK, CHUNK), dtype=bool))
    strict = jnp.tril(jnp.ones((CHUNK, CHUNK), dtype=bool), -1)
    decay = jnp.exp(jnp.where(causal, gc[..., :, None] - gc[..., None, :], -jnp.inf))
    kk = jnp.einsum('nbhid,nbhjd->nbhij', kc, kc)
    a_low = jnp.where(strict, bc[..., :, None] * kk * decay, 0.0)
    rhs = jnp.concatenate([vc * bc[..., None], kc * (bc * jnp.exp(gc))[..., None]], axis=-1)
    sol = lax.linalg.triangular_solve(a_low + jnp.eye(CHUNK, dtype=a_low.dtype), rhs,
                                      left_side=True, lower=True)
    u, w = sol[..., :dv], sol[..., dv:]
    qk = jnp.where(causal, jnp.einsum('nbhid,nbhjd->nbhij', qc, kc) * decay, 0.0)

    def step(S, blk):
        q_, k_, u_, w_, qk_, g_ = blk
        v_new = u_ - jnp.einsum('bhcd,bhde->bhce', w_, S)
        o = (jnp.einsum('bhcd,bhde->bhce', q_ * jnp.exp(g_)[..., None], S)
             + jnp.einsum('bhij,bhje->bhie', qk_, v_new))
        g_last = g_[..., -1:]
        S = (S * jnp.exp(g_last)[..., None]
             + jnp.einsum('bhcd,bhce->bhde', k_ * jnp.exp(g_last - g_)[..., None], v_new))
        return S, o

    s_final, o = lax.scan(step, s0, (qc, kc, u, w, qk, gc))
    o = jnp.transpose(o, (1, 0, 3, 2, 4)).reshape(bsz, n * CHUNK, H, dv)[:, :T]
    return o, s_final


def s5_mixer(u, re0, im0, lam_re, lam_im, log_dt, b_re, b_im, c_re, c_im, d_skip, glu_w, glu_b):
    bsz, T, _ = u.shape
    uf = u.astype(jnp.float32).reshape(bsz, T, SSM_GROUPS, SSM_GROUP)
    lr = jnp.minimum(lam_re.astype(jnp.float32), -1e-4)
    li = lam_im.astype(jnp.float32)
    dt = jnp.exp(log_dt.astype(jnp.float32))[:, None]
    mag = jnp.exp(lr * dt)
    ar, ai = mag * jnp.cos(li * dt), mag * jnp.sin(li * dt)
    den = lr * lr + li * li
    nr, ni = ar - 1.0, ai
    cr, ci = (nr * lr + ni * li) / den, (ni * lr - nr * li) / den
    br, bi = b_re.astype(jnp.float32), b_im.astype(jnp.float32)
    bbr = cr[..., None] * br - ci[..., None] * bi
    bbi = cr[..., None] * bi + ci[..., None] * br
    bur = jnp.einsum('btgm,gpm->btgp', uf, bbr)
    bui = jnp.einsum('btgm,gpm->btgp', uf, bbi)
    r0, i0 = re0.astype(jnp.float32), im0.astype(jnp.float32)
    bur = bur.at[:, 0].add(ar * r0 - ai * i0)
    bui = bui.at[:, 0].add(ar * i0 + ai * r0)
    a_r = jnp.broadcast_to(ar, (1, T) + ar.shape)
    a_i = jnp.broadcast_to(ai, (1, T) + ai.shape)

    def combine(e1, e2):
        a1r, a1i, b1r, b1i = e1
        a2r, a2i, b2r, b2i = e2
        return (a1r * a2r - a1i * a2i, a1r * a2i + a1i * a2r,
                a2r * b1r - a2i * b1i + b2r, a2r * b1i + a2i * b1r + b2i)

    _, _, xr, xi = lax.associative_scan(combine, (a_r, a_i, bur, bui), axis=1)
    y = (jnp.einsum('btgp,gmp->btgm', xr, c_re.astype(jnp.float32))
         - jnp.einsum('btgp,gmp->btgm', xi, c_im.astype(jnp.float32))
         + uf * d_skip.astype(jnp.float32).reshape(SSM_GROUPS, SSM_GROUP))
    hg = jax.nn.gelu(y.reshape(bsz, T, SSM_WIDTH))
    out = hg * jax.nn.sigmoid(hg @ glu_w.astype(jnp.float32) + glu_b.astype(jnp.float32))
    return out.astype(u.dtype), xr[:, -1], xi[:, -1]


def even_mixer(h, conv_buf, s_delta, s_re, s_im, w_in, w_out, conv_w, a_log, dt_bias, out_gain,
               lam_re, lam_im, log_dt, b_re, b_im, c_re, c_im, d_skip, glu_w, glu_b):
    bsz, T, _ = h.shape
    qkv, z, beta_in, a_in, u = split_cols(h @ w_in, EVEN_PROJ_SIZES)
    qkv, new_buf = causal_short_conv(qkv, conv_buf, conv_w)
    q, k, v = split_cols(qkv.astype(jnp.float32), (GDN_HEADS * GDN_DK, GDN_HEADS * GDN_DK, GDN_HEADS * GDN_DV))
    q = l2_normalize(q.reshape(bsz, T, GDN_HEADS, GDN_DK)) * (GDN_DK ** -0.5)
    k = l2_normalize(k.reshape(bsz, T, GDN_HEADS, GDN_DK))
    v = v.reshape(bsz, T, GDN_HEADS, GDN_DV)
    beta = jax.nn.sigmoid(beta_in.astype(jnp.float32))
    logg = -jnp.exp(a_log.astype(jnp.float32)) * jax.nn.softplus(a_in.astype(jnp.float32) + dt_bias.astype(jnp.float32))
    o, s_new = gated_delta_rule(q, k, v, beta, logg, s_delta.astype(jnp.float32))
    o = rms_norm(o, out_gain) * jax.nn.silu(z.astype(jnp.float32).reshape(bsz, T, GDN_HEADS, GDN_DV))
    out_a = o.reshape(bsz, T, GDN_HEADS * GDN_DV).astype(h.dtype)
    out_b, re_new, im_new = s5_mixer(u, s_re, s_im, lam_re, lam_im, log_dt, b_re, b_im, c_re, c_im, d_skip, glu_w, glu_b)
    y = jnp.concatenate([out_a, out_b], axis=-1) @ w_out
    return (y, new_buf, s_new.astype(h.dtype), re_new.astype(h.dtype), im_new.astype(h.dtype))


def sweep_query_blocks(fn, *q_arrays):
    T = q_arrays[0].shape[1]
    nb = -(-T // Q_BLOCK)
    pad = nb * Q_BLOCK - T

    def split(a):
        a = jnp.pad(a, [(0, 0), (0, pad)] + [(0, 0)] * (a.ndim - 2))
        return jnp.swapaxes(a.reshape((a.shape[0], nb, Q_BLOCK) + a.shape[2:]), 0, 1)

    out = lax.map(lambda blk: fn(*blk), tuple(split(a) for a in q_arrays))
    out = jnp.swapaxes(out, 0, 1)
    return out.reshape((out.shape[0], nb * Q_BLOCK) + out.shape[3:])[:, :T]


def odd_mixer(h, kf_past, vf_past, lf_past, kd_past, vd_past, pos, cid, past_pos, past_cid,
              w_in, w_out, f_bias, diff_lambda, diff_gain, lam_init):
    bsz, T, _ = h.shape
    qf, kf, vf, f_in, qd, kd, vd = split_cols(h @ w_in, ODD_PROJ_SIZES)

    def heads(a, n):
        return a.reshape(bsz, T, n, -1)

    qf, kf, vf = heads(qf, FOX_HEADS), heads(kf, FOX_HEADS), heads(vf, FOX_HEADS)
    logf = jax.nn.log_sigmoid(f_in.astype(jnp.float32) + f_bias.astype(jnp.float32))
    qd = partial_rope(heads(qd, 2 * DIFF_HEADS), pos)
    kd = partial_rope(heads(kd, 2 * DIFF_HEADS), pos)
    vd = heads(vd, DIFF_HEADS)
    kf_all = jnp.concatenate([kf_past.astype(h.dtype), kf], axis=1)
    vf_all = jnp.concatenate([vf_past.astype(h.dtype), vf], axis=1)
    kd_all = jnp.concatenate([kd_past.astype(h.dtype), kd], axis=1)
    vd_all = jnp.concatenate([vd_past.astype(h.dtype), vd], axis=1)
    cum_f = jnp.cumsum(jnp.concatenate([lf_past.astype(jnp.float32), logf], axis=1), axis=1)
    fk = jnp.transpose(cum_f, (0, 2, 1))
    fq = cum_f[:, cum_f.shape[1] - T:]
    kpos = jnp.concatenate([past_pos, pos])
    kcid = jnp.concatenate([past_cid, cid])
    lq1, lk1, lq2, lk2 = diff_lambda.astype(jnp.float32)
    lam = jnp.exp(jnp.sum(lq1 * lk1)) - jnp.exp(jnp.sum(lq2 * lk2)) + lam_init
    scale = HEAD_DIM ** -0.5

    def attend(qf_b, fq_b, qd_b, pos_b, cid_b):
        nq = qf_b.shape[1]
        s = jnp.einsum('bqhd,bkhd->bhqk', qf_b, kf_all, preferred_element_type=jnp.float32) * scale
        s = s + jnp.transpose(fq_b, (0, 2, 1))[..., None] - fk[:, :, None, :]
        p = jax.nn.softmax(jnp.where(kpos[None, :] <= pos_b[0][:, None], s, NEG_INF), axis=-1)
        o_fox = jnp.einsum('bhqk,bkhd->bqhd', p.astype(vf_all.dtype), vf_all).reshape(bsz, nq, -1)
        s2 = jnp.einsum('bqhd,bkhd->bhqk', qd_b, kd_all, preferred_element_type=jnp.float32) * scale
        p2 = jax.nn.softmax(jnp.where(kcid[None, :] <= cid_b[0][:, None], s2, NEG_INF), axis=-1)
        p2 = p2.reshape(bsz, DIFF_HEADS, 2, nq, -1)
        pd = p2[:, :, 0] - lam * p2[:, :, 1]
        o_diff = jnp.einsum('bhqk,bkhe->bqhe', pd.astype(vd_all.dtype), vd_all)
        o_diff = rms_norm(o_diff, diff_gain) * (1.0 - lam_init)
        return jnp.concatenate([o_fox, o_diff.reshape(bsz, nq, -1)], axis=-1)

    mixed = sweep_query_blocks(attend, qf, fq, qd, pos[None], cid[None])
    return (mixed @ w_out, kf, vf, logf.astype(h.dtype), kd, vd)


def setup_inputs(seed: int = 0) -> dict:
    key = jax.random.key(seed)
    keys = iter(jax.random.split(key, 64))
    f32 = jnp.float32

    def normal(shape, scale=1.0):
        return jax.random.normal(next(keys), shape, f32) * scale

    def uniform(shape, lo, hi):
        return jax.random.uniform(next(keys), shape, f32, lo, hi)

    ne, no = (DEPTH + 1) // 2, DEPTH // 2
    gdn_dt = jnp.exp(uniform((ne, GDN_HEADS), math.log(1e-3), math.log(1e-1)))
    return {
        'x_prompt': normal((BATCH, SEQ, D_MODEL)),
        'x_sample': normal((DEC_BATCH, DEC_SEQ, D_MODEL)),
        'state_conv': normal((ne, DEC_BATCH, CONV_W - 1, GDN_QKV)),
        'state_delta': normal((ne, DEC_BATCH, GDN_HEADS, GDN_DK, GDN_DV), 0.1),
        'state_ssm_re': normal((ne, DEC_BATCH, SSM_GROUPS, SSM_STATE), 0.1),
        'state_ssm_im': normal((ne, DEC_BATCH, SSM_GROUPS, SSM_STATE), 0.1),
        'cache_fox_k': normal((no, DEC_BATCH, PAST_LEN, FOX_HEADS, HEAD_DIM)),
        'cache_fox_v': normal((no, DEC_BATCH, PAST_LEN, FOX_HEADS, HEAD_DIM)),
        'cache_fox_logf': jax.nn.log_sigmoid(3.0 + normal((no, DEC_BATCH, PAST_LEN, FOX_HEADS))),
        'cache_diff_k': normal((no, DEC_BATCH, PAST_LEN, 2 * DIFF_HEADS, HEAD_DIM)),
        'cache_diff_v': normal((no, DEC_BATCH, PAST_LEN, DIFF_HEADS, DIFF_V)),
        'meta_tokens': normal((N_META, D_MODEL)),
        'norm_mix': 1.0 + normal((DEPTH, D_MODEL), 0.02),
        'norm_ffn': 1.0 + normal((DEPTH, D_MODEL), 0.02),
        'norm_final': 1.0 + normal((D_MODEL,), 0.02),
        'w_in_even': normal((ne, D_MODEL, EVEN_PROJ), D_MODEL ** -0.5),
        'w_out_even': normal((ne, EVEN_MIX, D_MODEL), EVEN_MIX ** -0.5),
        'conv_w': normal((ne, CONV_W, GDN_QKV), CONV_W ** -0.5),
        'gdn_a_log': jnp.log(uniform((ne, GDN_HEADS), 1.0, 16.0)),
        'gdn_dt_bias': gdn_dt + jnp.log(-jnp.expm1(-gdn_dt)),
        'gdn_out_norm': 1.0 + normal((ne, GDN_DV), 0.02),
        'ssm_lambda_re': -0.5 + normal((ne, SSM_GROUPS, SSM_STATE), 0.01),
        'ssm_lambda_im': jnp.pi * jnp.arange(SSM_STATE, dtype=f32) + normal((ne, SSM_GROUPS, SSM_STATE), 0.01),
        'ssm_log_dt': uniform((ne, SSM_GROUPS), math.log(1e-3), math.log(1e-1)),
        'ssm_b_re': normal((ne, SSM_GROUPS, SSM_STATE, SSM_GROUP), (2 * SSM_GROUP) ** -0.5),
        'ssm_b_im': normal((ne, SSM_GROUPS, SSM_STATE, SSM_GROUP), (2 * SSM_GROUP) ** -0.5),
        'ssm_c_re': normal((ne, SSM_GROUPS, SSM_GROUP, SSM_STATE), SSM_STATE ** -0.5),
        'ssm_c_im': normal((ne, SSM_GROUPS, SSM_GROUP, SSM_STATE), SSM_STATE ** -0.5),
        'ssm_d': normal((ne, SSM_WIDTH)),
        'ssm_glu_w': normal((ne, SSM_WIDTH, SSM_WIDTH), SSM_WIDTH ** -0.5),
        'ssm_glu_b': normal((ne, SSM_WIDTH), 0.01),
        'w_in_odd': normal((no, D_MODEL, ODD_PROJ), D_MODEL ** -0.5),
        'w_out_odd': normal((no, ODD_MIX, D_MODEL), ODD_MIX ** -0.5),
        'fox_f_bias': 3.0 + normal((no, FOX_HEADS), 0.5),
        'diff_lambda': normal((no, 4, HEAD_DIM), 0.1),
        'diff_out_norm': 1.0 + normal((no, DIFF_V), 0.02),
        'ffn_w1': normal((DEPTH, D_MODEL, FFN_HIDDEN), D_MODEL ** -0.5),
        'ffn_w3': normal((DEPTH, D_MODEL, FFN_HIDDEN), D_MODEL ** -0.5),
        'ffn_w2': normal((DEPTH, FFN_HIDDEN, D_MODEL), FFN_HIDDEN ** -0.5),
    }


def reference(x_prompt, x_sample, state_conv, state_delta, state_ssm_re, state_ssm_im,
              cache_fox_k, cache_fox_v, cache_fox_logf, cache_diff_k, cache_diff_v,
              meta_tokens, norm_mix, norm_ffn, norm_final,
              w_in_even, w_out_even, conv_w, gdn_a_log, gdn_dt_bias, gdn_out_norm,
              ssm_lambda_re, ssm_lambda_im, ssm_log_dt, ssm_b_re, ssm_b_im, ssm_c_re, ssm_c_im,
              ssm_d, ssm_glu_w, ssm_glu_b,
              w_in_odd, w_out_odd, fox_f_bias, diff_lambda, diff_out_norm,
              ffn_w1, ffn_w3, ffn_w2):
    dtp = x_prompt.dtype
    bp = x_prompt.shape[0]
    meta = jnp.broadcast_to(meta_tokens.astype(dtp)[None], (bp, N_META, D_MODEL))
    hp = jnp.concatenate([meta, x_prompt], axis=1)
    hs = x_sample
    lp, ls = hp.shape[1], hs.shape[1]
    pos_p = jnp.arange(lp, dtype=jnp.int32)
    cid_p = jnp.where(pos_p < N_META, -1, (pos_p - N_META) // CHUNK)
    pos_s = PAST_LEN + jnp.arange(ls, dtype=jnp.int32)
    cid_s = pos_s // CHUNK
    past_pos = jnp.arange(PAST_LEN, dtype=jnp.int32)
    past_cid = past_pos // CHUNK
    no_pos = jnp.zeros((0,), jnp.int32)

    conv_p, conv_s, delta_p, delta_s = [], [], [], []
    re_p, re_s, im_p, im_s = [], [], [], []
    fk_p, fk_s, fv_p, fv_s, fl_p, fl_s = [], [], [], [], [], []
    dk_p, dk_s, dv_p, dv_s = [], [], [], []
    for l in range(DEPTH):
        i = l // 2
        if l % 2 == 0:
            pe = (w_in_even[i], w_out_even[i], conv_w[i], gdn_a_log[i], gdn_dt_bias[i], gdn_out_norm[i],
                  ssm_lambda_re[i], ssm_lambda_im[i], ssm_log_dt[i], ssm_b_re[i], ssm_b_im[i],
                  ssm_c_re[i], ssm_c_im[i], ssm_d[i], ssm_glu_w[i], ssm_glu_b[i])
            yp, cb, sd, sr, si = even_mixer(
                rms_norm(hp, norm_mix[l]),
                jnp.zeros((bp, CONV_W - 1, GDN_QKV), dtp),
                jnp.zeros((bp, GDN_HEADS, GDN_DK, GDN_DV), jnp.float32),
                jnp.zeros((bp, SSM_GROUPS, SSM_STATE), jnp.float32),
                jnp.zeros((bp, SSM_GROUPS, SSM_STATE), jnp.float32), *pe)
            ys, cb2, sd2, sr2, si2 = even_mixer(
                rms_norm(hs, norm_mix[l]), state_conv[i], state_delta[i],
                state_ssm_re[i], state_ssm_im[i], *pe)
            conv_p.append(cb); conv_s.append(cb2)
            delta_p.append(sd); delta_s.append(sd2)
            re_p.append(sr); re_s.append(sr2)
            im_p.append(si); im_s.append(si2)
        else:
            lam_init = 0.8 - 0.6 * math.exp(-0.3 * l)
            po = (w_in_odd[i], w_out_odd[i], fox_f_bias[i], diff_lambda[i], diff_out_norm[i], lam_init)
            yp, a1, a2, a3, a4, a5 = odd_mixer(
                rms_norm(hp, norm_mix[l]),
                jnp.zeros((bp, 0, FOX_HEADS, HEAD_DIM), dtp), jnp.zeros((bp, 0, FOX_HEADS, HEAD_DIM), dtp),
                jnp.zeros((bp, 0, FOX_HEADS), jnp.float32),
                jnp.zeros((bp, 0, 2 * DIFF_HEADS, HEAD_DIM), dtp), jnp.zeros((bp, 0, DIFF_HEADS, DIFF_V), dtp),
                pos_p, cid_p, no_pos, no_pos, *po)
            ys, b1, b2, b3, b4, b5 = odd_mixer(
                rms_norm(hs, norm_mix[l]), cache_fox_k[i], cache_fox_v[i], cache_fox_logf[i],
                cache_diff_k[i], cache_diff_v[i], pos_s, cid_s, past_pos, past_cid, *po)
            fk_p.append(a1); fk_s.append(b1)
            fv_p.append(a2); fv_s.append(b2)
            fl_p.append(a3); fl_s.append(b3)
            dk_p.append(a4); dk_s.append(b4)
            dv_p.append(a5); dv_s.append(b5)
        hp = hp + yp
        hs = hs + ys
        hp = hp + swiglu(rms_norm(hp, norm_ffn[l]), ffn_w1[l], ffn_w3[l], ffn_w2[l])
        hs = hs + swiglu(rms_norm(hs, norm_ffn[l]), ffn_w1[l], ffn_w3[l], ffn_w2[l])

    y_prompt = rms_norm(hp, norm_final)[:, N_META:]
    y_sample = rms_norm(hs, norm_final)
    return (y_prompt, y_sample,
            jnp.stack(conv_p), jnp.stack(conv_s), jnp.stack(delta_p), jnp.stack(delta_s),
            jnp.stack(re_p), jnp.stack(re_s), jnp.stack(im_p), jnp.stack(im_s),
            jnp.stack(fk_p), jnp.stack(fk_s), jnp.stack(fv_p), jnp.stack(fv_s),
            jnp.stack(fl_p), jnp.stack(fl_s), jnp.stack(dk_p), jnp.stack(dk_s),
            jnp.stack(dv_p), jnp.stack(dv_s))
```

```python
import functools
import math

import jax
import jax.numpy as jnp
import numpy as np
from jax import lax
from jax.experimental import pallas as pl
from jax.experimental.pallas import tpu as pltpu

F32 = jnp.float32
BF16 = jnp.bfloat16
HIGHEST = lax.Precision.HIGHEST

D_MODEL = 1024
N_META = 16
RMS_EPS = 1e-6
HEAD_DIM = 64
GDN_HEADS = 8
GDN_DK = 128
CONV_W = 4
SSM_WIDTH = 512
SSM_GROUP = 16
SSM_GROUPS = 32
SSM_STATE = 64
FOX_HEADS = 8
DIFF_HEADS = 4
DIFF_V = 128
ROT_DIM = 16
ROPE_THETA = 500000.0
FFN_HIDDEN = 2816
FFN_CHUNK = 256
BLOCK = 64
VMEM_LIMIT = 56 * 1024 * 1024


def _cparams(sem):
    return pltpu.CompilerParams(dimension_semantics=sem, vmem_limit_bytes=VMEM_LIMIT)


def _const_spec(shape):
    nd = len(shape)
    return pl.BlockSpec(shape, lambda *_: (0,) * nd)


def _sigmoid(x):
    return 1.0 / (1.0 + jnp.exp(-x))


def _silu(x):
    return x * _sigmoid(x)


def _rms_scale(x):
    return x * lax.rsqrt(jnp.mean(x * x, axis=-1, keepdims=True) + RMS_EPS)


def _row_spec(kind, tm, width, nt):
    if kind == "flat":
        return pl.BlockSpec((tm, width), lambda b, i: (b * nt + i, 0))
    return pl.BlockSpec((tm, width), lambda b, i: (i, b))


def _row_shape(kind, bsz, tb, width, dtype):
    if kind == "flat":
        return jax.ShapeDtypeStruct((bsz * tb, width), dtype)
    return jax.ShapeDtypeStruct((tb, bsz * width), dtype)


def _inproj_kernel(x_ref, g_ref, w_ref, *o_refs, widths):
    xn = (_rms_scale(x_ref[...]) * g_ref[...]).astype(BF16)
    off = 0
    for o_ref, wd in zip(o_refs, widths):
        for c0 in range(0, wd, 512):
            cw = min(512, wd - c0)
            o_ref[:, c0:c0 + cw] = jnp.dot(xn, w_ref[:, off + c0:off + c0 + cw],
                                           preferred_element_type=F32)
        off += wd


def inproj(x, gain, w, outs, bsz, tb, tm):
    nt = tb // tm
    widths = tuple(wd for wd, _ in outs)
    return pl.pallas_call(
        functools.partial(_inproj_kernel, widths=widths),
        grid=(bsz, nt),
        in_specs=[_row_spec("flat", tm, D_MODEL, nt), _const_spec((1, D_MODEL)), _const_spec(w.shape)],
        out_specs=[_row_spec(kind, tm, wd, nt) for wd, kind in outs],
        out_shape=[_row_shape(kind, bsz, tb, wd, F32) for wd, kind in outs],
        compiler_params=_cparams(("parallel", "parallel")),
        name="inproj",
    )(x, gain.reshape(1, D_MODEL), w)


def _outffn_kernel(res_ref, *refs, a_widths, nc):
    na = len(a_widths)
    a_refs = refs[:na]
    wout_ref, g_ref, w1_ref, w3_ref, w2_ref, o_ref, acc_ref, xn_ref = refs[na:]
    x1 = res_ref[...]
    off = 0
    for a_ref, wd in zip(a_refs, a_widths):
        x1 = x1 + jnp.dot(a_ref[...].astype(BF16), wout_ref[off:off + wd, :], preferred_element_type=F32)
        off += wd
    xn_ref[...] = (_rms_scale(x1) * g_ref[...]).astype(BF16)
    acc_ref[...] = x1

    def body(c, carry):
        xn = xn_ref[...]
        h1 = jnp.dot(xn, w1_ref[c], preferred_element_type=F32)
        h3 = jnp.dot(xn, w3_ref[c], preferred_element_type=F32)
        act = (_silu(h1) * h3).astype(BF16)
        acc_ref[...] += jnp.dot(act, w2_ref[c], preferred_element_type=F32)
        return carry

    lax.fori_loop(0, nc, body, 0)
    o_ref[...] = acc_ref[...]


def outproj_ffn(res, mixed, w_out, gain, w1, w3, w2, bsz, tb, tm):
    nt = tb // tm
    nc = w1.shape[0]
    a_widths = tuple(wd for _, wd, _ in mixed)
    row = _row_spec("flat", tm, D_MODEL, nt)
    return pl.pallas_call(
        functools.partial(_outffn_kernel, a_widths=a_widths, nc=nc),
        grid=(bsz, nt),
        in_specs=[row] + [_row_spec(kind, tm, wd, nt) for _, wd, kind in mixed]
        + [_const_spec(w_out.shape), _const_spec((1, D_MODEL)), _const_spec(w1.shape), _const_spec(w3.shape),
           _const_spec(w2.shape)],
        out_specs=row,
        out_shape=jax.ShapeDtypeStruct(res.shape, F32),
        scratch_shapes=[pltpu.VMEM((tm, D_MODEL), F32), pltpu.VMEM((tm, D_MODEL), BF16)],
        compiler_params=_cparams(("parallel", "parallel")),
        name="outproj_ffn",
    )(res, *[a for a, _, _ in mixed], w_out, gain.reshape(1, D_MODEL), w1, w3, w2)


def _final_norm_kernel(x_ref, g_ref, o_ref):
    o_ref[...] = _rms_scale(x_ref[...]) * g_ref[...]


def final_norm(x, gain, bsz, tb, t_out, tm):
    spec = pl.BlockSpec((None, tm, D_MODEL), lambda b, i: (b, i, 0))
    return pl.pallas_call(
        _final_norm_kernel,
        grid=(bsz, t_out // tm),
        in_specs=[spec, _const_spec((1, D_MODEL))],
        out_specs=spec,
        out_shape=jax.ShapeDtypeStruct((bsz, t_out, D_MODEL), F32),
        compiler_params=_cparams(("parallel", "parallel")),
        name="final_norm",
    )(x.reshape(bsz, tb, D_MODEL), gain.reshape(1, D_MODEL))


def _dot_nt(a, b):
    return lax.dot_general(a, b, (((1,), (1,)), ((), ())), preferred_element_type=F32)


def _dot_hi(a, b):
    return jnp.dot(a, b, precision=HIGHEST, preferred_element_type=F32)


def _unit_lower_solve(a_low, rhs, csz):
    sub = 16
    r = lax.broadcasted_iota(jnp.int32, (csz, csz), 0)
    c = lax.broadcasted_iota(jnp.int32, (csz, csz), 1)
    same = (r // sub) == (c // sub)
    eye = (r == c).astype(F32)
    dg = jnp.where(same, a_low, 0.0)
    off = a_low - dg
    d2 = _dot_hi(dg, dg)
    d4 = _dot_hi(d2, d2)
    d8 = _dot_hi(d4, d4)
    t0 = eye - dg
    t0 = t0 + _dot_hi(t0, d2)
    t0 = t0 + _dot_hi(t0, d4)
    t0 = t0 + _dot_hi(t0, d8)
    g1 = _dot_hi(t0, off)
    x = _dot_hi(t0, rhs)
    nblk = csz // sub
    powers = []
    gp = g1
    k = 2
    while k < nblk:
        gp = _dot_hi(gp, gp)
        powers.append(gp)
        k *= 2
    for gp in powers:
        x = x + _dot_hi(gp, x)
    return x - _dot_hi(g1, x)


def _gdn_kernel(qkv_ref, z_ref, ba_ref, cw_ref, alog_ref, dtb_ref, gain_ref, conv0_ref, s0_ref,
                o_ref, sout_ref, xbuf, s_ref, *, csz):
    c = pl.program_id(1)

    @pl.when(c == 0)
    def _():
        xbuf[0:8, :] = conv0_ref[...]
        s_ref[...] = s0_ref[...]

    xbuf[8:8 + csz, :] = qkv_ref[...]

    ba = ba_ref[...]
    beta = _sigmoid(ba)
    xs = ba + dtb_ref[...]
    softplus = jnp.maximum(xs, 0.0) + jnp.log(1.0 + jnp.exp(-jnp.abs(xs)))
    logg = -jnp.exp(alog_ref[...]) * softplus
    r = lax.broadcasted_iota(jnp.int32, (csz, csz), 0)
    cc = lax.broadcasted_iota(jnp.int32, (csz, csz), 1)
    causal = r >= cc
    strict = r > cc
    gc = _dot_hi(causal.astype(F32), logg)
    gct = gc.T
    gain = gain_ref[...]

    def conv_silu(col):
        y = cw_ref[3:4, col:col + GDN_DK] * xbuf[8:8 + csz, col:col + GDN_DK]
        for j in range(3):
            y = y + cw_ref[j:j + 1, col:col + GDN_DK] * xbuf[5 + j:5 + j + csz, col:col + GDN_DK]
        return _silu(y)

    for h in range(GDN_HEADS):
        q = conv_silu(h * GDN_DK)
        k = conv_silu(GDN_HEADS * GDN_DK + h * GDN_DK)
        v = conv_silu(2 * GDN_HEADS * GDN_DK + h * GDN_DK)
        q = q * lax.rsqrt(jnp.sum(q * q, axis=-1, keepdims=True) + 1e-6) * (GDN_DK ** -0.5)
        k = k * lax.rsqrt(jnp.sum(k * k, axis=-1, keepdims=True) + 1e-6)
        bh = beta[:, h:h + 1]
        gch = gc[:, 8 + h:9 + h]
        grow = gct[8 + h:9 + h, :]
        decay = jnp.where(causal, jnp.exp(gch - grow), 0.0)
        kb = k.astype(BF16)
        kk = _dot_nt(kb, kb)
        qk = _dot_nt(q.astype(BF16), kb)
        a_low = jnp.where(strict, bh * kk * decay, 0.0)
        eg = jnp.exp(gch)
        rhs = jnp.concatenate([v * bh, k * (bh * eg)], axis=-1)
        sol = _unit_lower_solve(a_low, rhs, csz)
        u = sol[:, :GDN_DK]
        w = sol[:, GDN_DK:]
        s_old = s_ref[h]
        sb = s_old.astype(BF16)
        v_new = u - jnp.dot(w.astype(BF16), sb, preferred_element_type=F32)
        vb = v_new.astype(BF16)
        o = (jnp.dot((q * eg).astype(BF16), sb, preferred_element_type=F32)
             + jnp.dot((qk * decay).astype(BF16), vb, preferred_element_type=F32))
        g_last = gch[csz - 1:csz, :]
        k2 = (k * jnp.exp(g_last - gch)).astype(BF16)
        s_ref[h] = s_old * jnp.exp(g_last) + jnp.dot(k2.T, vb, preferred_element_type=F32)
        zz = z_ref[:, h * GDN_DK:(h + 1) * GDN_DK]
        o_ref[:, h * GDN_DK:(h + 1) * GDN_DK] = (_rms_scale(o) * gain * _silu(zz)).astype(BF16)

    xbuf[0:8, :] = xbuf[csz:csz + 8, :]

    @pl.when(c == pl.num_programs(1) - 1)
    def _():
        sout_ref[...] = s_ref[...]


def gdn(qkv, z, ba, conv_w, a_log, dt_bias, out_gain, conv0, s0, bsz, tb, csz, first_block):
    nblk = tb // csz
    qk_w = 3 * GDN_HEADS * GDN_DK

    def rows(b, c):
        return (b * nblk + (c + first_block) % nblk, 0)

    def lane_pad(vec):
        return jnp.zeros((1, 128), F32).at[0, 8:16].set(vec.astype(F32))

    conv0p = jnp.concatenate([jnp.zeros((bsz, 8 - (CONV_W - 1), qk_w), F32), conv0.astype(F32)], axis=1)
    return pl.pallas_call(
        functools.partial(_gdn_kernel, csz=csz),
        grid=(bsz, nblk),
        in_specs=[pl.BlockSpec((csz, qk_w), rows), pl.BlockSpec((csz, GDN_HEADS * GDN_DK), rows),
                  pl.BlockSpec((csz, 128), rows), _const_spec((CONV_W, qk_w)), _const_spec((1, 128)),
                  _const_spec((1, 128)), _const_spec((1, GDN_DK)),
                  pl.BlockSpec((None, 8, qk_w), lambda b, c: (b, 0, 0)),
                  pl.BlockSpec((None, GDN_HEADS, GDN_DK, GDN_DK), lambda b, c: (b, 0, 0, 0))],
        out_specs=[pl.BlockSpec((csz, GDN_HEADS * GDN_DK), rows),
                   pl.BlockSpec((None, GDN_HEADS, GDN_DK, GDN_DK), lambda b, c: (b, 0, 0, 0))],
        out_shape=[jax.ShapeDtypeStruct((bsz * tb, GDN_HEADS * GDN_DK), BF16),
                   jax.ShapeDtypeStruct((bsz, GDN_HEADS, GDN_DK, GDN_DK), F32)],
        scratch_shapes=[pltpu.VMEM((8 + csz, qk_w), F32), pltpu.VMEM((GDN_HEADS, GDN_DK, GDN_DK), F32)],
        compiler_params=_cparams(("parallel", "arbitrary")),
        name="gdn",
    )(qkv, z, ba, conv_w.astype(F32), lane_pad(a_log), lane_pad(dt_bias), out_gain.reshape(1, GDN_DK).astype(F32),
      conv0p, s0.astype(F32))


S5_CHUNKS = 4
S5_CH_U = SSM_WIDTH // S5_CHUNKS
S5_CH_P = S5_CH_U // SSM_GROUP * SSM_STATE


def _s5_kernel(u_ref, wb_ref, wc_ref, a_ref, d_ref, gw_ref, gb_ref, st0_ref, o_ref, stout_ref, xs, st_ref,
               *, tt, bsz):
    c = pl.program_id(0)

    @pl.when(c == 0)
    def _():
        st_ref[...] = st0_ref[...]

    rows = tt * bsz
    u = u_ref[...].reshape(rows, SSM_WIDTH)
    ub = u.astype(BF16)
    for j in range(S5_CHUNKS):
        xs[j] = jnp.dot(ub[:, j * S5_CH_U:(j + 1) * S5_CH_U], wb_ref[j], preferred_element_type=F32)

    for j in range(S5_CHUNKS):
        ar = jnp.broadcast_to(a_ref[j, 0:1, :], (bsz, S5_CH_P))
        ai = jnp.broadcast_to(a_ref[j, 1:2, :], (bsz, S5_CH_P))

        def step(t, carry, j=j, ar=ar, ai=ai):
            re, im = carry
            row = pl.multiple_of(t * bsz, bsz)
            nre = ar * re - ai * im + xs[j, pl.ds(row, bsz), 0:S5_CH_P]
            nim = ar * im + ai * re + xs[j, pl.ds(row, bsz), S5_CH_P:2 * S5_CH_P]
            xs[j, pl.ds(row, bsz), 0:S5_CH_P] = nre
            xs[j, pl.ds(row, bsz), S5_CH_P:2 * S5_CH_P] = nim
            return nre, nim

        re, im = lax.fori_loop(0, tt, step, (st_ref[j, :, 0:S5_CH_P], st_ref[j, :, S5_CH_P:2 * S5_CH_P]))
        st_ref[j, :, 0:S5_CH_P] = re
        st_ref[j, :, S5_CH_P:2 * S5_CH_P] = im

    ys = [jnp.dot(xs[j].astype(BF16), wc_ref[j], preferred_element_type=F32) for j in range(S5_CHUNKS)]
    y = jnp.concatenate(ys, axis=-1) + u * d_ref[...]
    hg = 0.5 * y * (1.0 + jnp.tanh(math.sqrt(2.0 / math.pi) * (y + 0.044715 * (y * y * y))))
    gate = _sigmoid(jnp.dot(hg.astype(BF16), gw_ref[...], preferred_element_type=F32) + gb_ref[...])
    o_ref[...] = (hg * gate).reshape(tt, bsz, SSM_WIDTH)

    @pl.when(c == pl.num_programs(0) - 1)
    def _():
        stout_ref[...] = st_ref[...]


def s5(u_tb, re0, im0, lam_re, lam_im, log_dt, b_re, b_im, c_re, c_im, d_skip, glu_w, glu_b,
       bsz, tb, tt, first_tile):
    ntile = tb // tt
    lr = jnp.minimum(lam_re.astype(F32), -1e-4)
    li = lam_im.astype(F32)
    dt = jnp.exp(log_dt.astype(F32))[:, None]
    mag = jnp.exp(lr * dt)
    ar, ai = mag * jnp.cos(li * dt), mag * jnp.sin(li * dt)
    den = lr * lr + li * li
    nr, ni = ar - 1.0, ai
    cr, ci = (nr * lr + ni * li) / den, (ni * lr - nr * li) / den
    br, bi = b_re.astype(F32), b_im.astype(F32)
    bbr = cr[..., None] * br - ci[..., None] * bi
    bbi = cr[..., None] * bi + ci[..., None] * br
    gpc = S5_CH_U // SSM_GROUP

    def block_diag_in(bb):
        bb = bb.reshape(S5_CHUNKS, gpc, SSM_STATE, SSM_GROUP)
        eye = jnp.eye(gpc, dtype=F32)
        return jnp.einsum("jgpm,gh->jgmhp", bb, eye).reshape(S5_CHUNKS, S5_CH_U, S5_CH_P)

    def block_diag_out(cm):
        cm = cm.reshape(S5_CHUNKS, gpc, SSM_GROUP, SSM_STATE)
        eye = jnp.eye(gpc, dtype=F32)
        return jnp.einsum("jgmp,gh->jgphm", cm, eye).reshape(S5_CHUNKS, S5_CH_P, S5_CH_U)

    wb = jnp.concatenate([block_diag_in(bbr), block_diag_in(bbi)], axis=-1).astype(BF16)
    wc = jnp.concatenate([block_diag_out(c_re.astype(F32)), -block_diag_out(c_im.astype(F32))],
                         axis=1).astype(BF16)
    a_pack = jnp.stack([ar.reshape(S5_CHUNKS, S5_CH_P), ai.reshape(S5_CHUNKS, S5_CH_P)], axis=1)
    st0 = jnp.concatenate([re0.astype(F32).reshape(bsz, S5_CHUNKS, S5_CH_P),
                           im0.astype(F32).reshape(bsz, S5_CHUNKS, S5_CH_P)], axis=-1).transpose(1, 0, 2)

    def tile(c):
        return ((c + first_tile) % ntile, 0, 0)

    out, st = pl.pallas_call(
        functools.partial(_s5_kernel, tt=tt, bsz=bsz),
        grid=(ntile,),
        in_specs=[pl.BlockSpec((tt, bsz, SSM_WIDTH), tile), _const_spec(wb.shape), _const_spec(wc.shape),
                  _const_spec(a_pack.shape), _const_spec((1, SSM_WIDTH)), _const_spec((SSM_WIDTH, SSM_WIDTH)),
                  _const_spec((1, SSM_WIDTH)), _const_spec(st0.shape)],
        out_specs=[pl.BlockSpec((tt, bsz, SSM_WIDTH), tile), _const_spec(st0.shape)],
        out_shape=[jax.ShapeDtypeStruct((tb, bsz, SSM_WIDTH), F32), jax.ShapeDtypeStruct(st0.shape, F32)],
        scratch_shapes=[pltpu.VMEM((S5_CHUNKS, tt * bsz, 2 * S5_CH_P), F32),
                        pltpu.VMEM((S5_CHUNKS, bsz, 2 * S5_CH_P), F32)],
        compiler_params=_cparams(("arbitrary",)),
        name="s5",
    )(u_tb, wb, wc, a_pack, d_skip.reshape(1, SSM_WIDTH).astype(F32), glu_w.astype(BF16),
      glu_b.reshape(1, SSM_WIDTH).astype(F32), st0)
    st = st.transpose(1, 0, 2)
    re_new = st[..., :S5_CH_P].reshape(bsz, SSM_GROUPS, SSM_STATE)
    im_new = st[..., S5_CH_P:].reshape(bsz, SSM_GROUPS, SSM_STATE)
    return out, re_new, im_new


EVEN_OUTS = ((3 * GDN_HEADS * GDN_DK, "flat"), (GDN_HEADS * GDN_DK, "flat"), (128, "flat"), (SSM_WIDTH, "tmajor"))


def pack_even_w_in(w_in):
    qkv_w = 3 * GDN_HEADS * GDN_DK
    z_w = GDN_HEADS * GDN_DK
    c0 = qkv_w + z_w
    ba = jnp.zeros((D_MODEL, 128), w_in.dtype).at[:, :2 * GDN_HEADS].set(w_in[:, c0:c0 + 2 * GDN_HEADS])
    return jnp.concatenate([w_in[:, :c0], ba, w_in[:, c0 + 2 * GDN_HEADS:]], axis=1).astype(BF16)


def even_mixer(x, gain, p, conv0, s0, re0, im0, bsz, tb, tm, csz, first_block, tt):
    qkv, z, ba, u_t = inproj(x, gain, p["w_in"], EVEN_OUTS, bsz, tb, tm)
    o_gdn, s_new = gdn(qkv, z, ba, p["conv_w"], p["a_log"], p["dt_bias"], p["out_gain"], conv0, s0,
                       bsz, tb, csz, first_block)
    o_s5, re_new, im_new = s5(u_t.reshape(tb, bsz, SSM_WIDTH), re0, im0, p["lam_re"], p["lam_im"], p["log_dt"],
                              p["b_re"], p["b_im"], p["c_re"], p["c_im"], p["d"], p["glu_w"], p["glu_b"],
                              bsz, tb, tt, first_block * csz // tt)
    mixed = [(o_gdn, GDN_HEADS * GDN_DK, "flat"), (o_s5.reshape(tb, bsz * SSM_WIDTH), SSM_WIDTH, "tmajor")]
    return mixed, qkv, s_new, re_new, im_new


def _gate_kernel(f_ref, bias_ref, c0_ref, logf_ref, cum_ref, tot_ref, carry, *, pad_rows, log_sigmoid):
    c = pl.program_id(1)

    @pl.when(c == 0)
    def _():
        carry[...] = c0_ref[...]

    x = f_ref[...]
    if log_sigmoid:
        x = x + bias_ref[...]
        x = jnp.minimum(x, 0.0) - jnp.log(1.0 + jnp.exp(-jnp.abs(x)))
    rows = x.shape[0]
    if pad_rows:
        keep = jnp.logical_or(c > 0, lax.broadcasted_iota(jnp.int32, x.shape, 0) >= pad_rows)
        x = jnp.where(keep, x, 0.0)
    r = lax.broadcasted_iota(jnp.int32, (rows, rows), 0)
    cc = lax.broadcasted_iota(jnp.int32, (rows, rows), 1)
    cum = _dot_hi((r >= cc).astype(F32), x) + carry[...]
    logf_ref[...] = x
    cum_ref[...] = cum
    carry[...] = cum[rows - 1:rows, :]

    @pl.when(c == pl.num_programs(1) - 1)
    def _():
        tot_ref[...] = carry[...]


def gate_cumsum(f, bias, carry0, bsz, tb, blk, first_block, pad_rows, log_sigmoid):
    nblk = tb // blk

    def rows(b, c):
        return (b, (c + first_block) % nblk, 0)

    one = pl.BlockSpec((None, 1, 128), lambda b, c: (b, 0, 0))
    return pl.pallas_call(
        functools.partial(_gate_kernel, pad_rows=pad_rows, log_sigmoid=log_sigmoid),
        grid=(bsz, nblk),
        in_specs=[pl.BlockSpec((None, blk, 128), rows), _const_spec((1, 128)), one],
        out_specs=[pl.BlockSpec((None, blk, 128), rows), pl.BlockSpec((None, blk, 128), rows), one],
        out_shape=[jax.ShapeDtypeStruct((bsz, tb, 128), F32), jax.ShapeDtypeStruct((bsz, tb, 128), F32),
                   jax.ShapeDtypeStruct((bsz, 1, 128), F32)],
        scratch_shapes=[pltpu.VMEM((1, 128), F32)],
        compiler_params=_cparams(("parallel", "arbitrary")),
        name="gate_cumsum",
    )(f, bias, carry0)


def _rope_kernel(x_ref, c_ref, s1_ref, s2_ref, o_ref):
    x = x_ref[...]
    width = x.shape[-1]
    o_ref[...] = (x * c_ref[...] + pltpu.roll(x, width - ROT_DIM // 2, 1) * s1_ref[...]
                  + pltpu.roll(x, ROT_DIM // 2, 1) * s2_ref[...])


def rope_tables(pos, nheads):
    half = ROT_DIM // 2
    inv_freq = ROPE_THETA ** (-jnp.arange(0, ROT_DIM, 2, dtype=F32) / ROT_DIM)
    ang = pos.astype(F32)[:, None] * inv_freq[None, :]
    cos, sin = jnp.cos(ang), jnp.sin(ang)
    t = pos.shape[0]
    rest = jnp.zeros((t, HEAD_DIM - ROT_DIM), F32)
    z = jnp.zeros((t, half), F32)
    ctab = jnp.concatenate([cos, cos, rest + 1.0], axis=1)
    s1 = jnp.concatenate([-sin, z, rest], axis=1)
    s2 = jnp.concatenate([z, sin, rest], axis=1)
    return tuple(jnp.tile(a, (1, nheads)) for a in (ctab, s1, s2))


def rope(x, tables, bsz, tb, tm):
    width = x.shape[-1]
    nt = tb // tm
    xs = pl.BlockSpec((None, tm, width), lambda i, b: (b, i, 0))
    ts = pl.BlockSpec((tm, width), lambda i, b: (i, 0))
    return pl.pallas_call(
        _rope_kernel,
        grid=(nt, bsz),
        in_specs=[xs, ts, ts, ts],
        out_specs=xs,
        out_shape=jax.ShapeDtypeStruct(x.shape, F32),
        compiler_params=_cparams(("parallel", "parallel")),
        name="rope",
    )(x, *tables)


NEG_INF = -1e30
N_MAPS = FOX_HEADS + 2 * DIFF_HEADS


def _attn_kernel(*refs, tq, tk, te, has_main, main_causal, fox_extra, diff_extra, zero_pad_q, pad_rows,
                 out_scale):
    if has_main:
        (qf_ref, qd_ref, cq_ref, kf_ref, vf_ref, kd_ref, vd_ref, ck_ref,
         kfe_ref, vfe_ref, kde_ref, vde_ref, cke_ref, lam_ref, gain_ref,
         o_ref, m_ref, l_ref, accf_ref, accd_ref) = refs
    else:
        (qf_ref, qd_ref, cq_ref, kfe_ref, vfe_ref, kde_ref, vde_ref, cke_ref, lam_ref, gain_ref,
         o_ref, m_ref, l_ref, accf_ref, accd_ref) = refs
    i = pl.program_id(1)
    j = pl.program_id(2)
    scale = HEAD_DIM ** -0.5

    def update(idx, s, v, acc_ref, lo, hi):
        m_prev = m_ref[idx]
        m_new = jnp.maximum(m_prev, jnp.max(s, axis=-1, keepdims=True))
        alpha = jnp.exp(m_prev - m_new)
        p = jnp.exp(s - m_new)
        l_ref[idx] = alpha * l_ref[idx] + jnp.sum(p, axis=-1, keepdims=True)
        acc_ref[:, lo:hi] = alpha * acc_ref[:, lo:hi] + jnp.dot(p.astype(BF16), v, preferred_element_type=F32)
        m_ref[idx] = m_new

    def process(kf_r, vf_r, kd_r, vd_r, ckt, fox_mask, diff_mask):
        qf = (qf_ref[...] * scale).astype(BF16)
        qd = (qd_ref[...] * scale).astype(BF16)
        kf = kf_r[...].astype(BF16)
        vf = vf_r[...].astype(BF16)
        kd = kd_r[...].astype(BF16)
        vd = vd_r[...].astype(BF16)
        cq = cq_ref[...]
        for h in range(FOX_HEADS):
            lo, hi = h * HEAD_DIM, (h + 1) * HEAD_DIM
            s = _dot_nt(qf[:, lo:hi], kf[:, lo:hi]) + (cq[:, h:h + 1] - ckt[h:h + 1, :])
            if fox_mask is not None:
                s = jnp.where(fox_mask, s, NEG_INF)
            update(h, s, vf[:, lo:hi], accf_ref, lo, hi)
        for hm in range(2 * DIFF_HEADS):
            lo, hi = hm * HEAD_DIM, (hm + 1) * HEAD_DIM
            s = _dot_nt(qd[:, lo:hi], kd[:, lo:hi])
            if diff_mask is not None:
                s = jnp.where(diff_mask, s, NEG_INF)
            hd, which = hm // 2, hm % 2
            update(FOX_HEADS + hm, s, vd[:, hd * DIFF_V:(hd + 1) * DIFF_V],
                   accd_ref.at[which], hd * DIFF_V, (hd + 1) * DIFF_V)

    def extra_masks():
        r = lax.broadcasted_iota(jnp.int32, (tq, te), 0)
        c = lax.broadcasted_iota(jnp.int32, (tq, te), 1)
        table = {"none": None, "valid": c >= pad_rows, "causal": c <= r,
                 "causal_valid": jnp.logical_and(c <= r, c >= pad_rows)}
        return table[fox_extra], table[diff_extra]

    @pl.when(j == 0)
    def _():
        m_ref[...] = jnp.full(m_ref.shape, NEG_INF, F32)
        l_ref[...] = jnp.zeros(l_ref.shape, F32)
        accf_ref[...] = jnp.zeros(accf_ref.shape, F32)
        accd_ref[...] = jnp.zeros(accd_ref.shape, F32)
        fm, dm = extra_masks()
        process(kfe_ref, vfe_ref, kde_ref, vde_ref, cke_ref[...], fm, dm)

    if has_main:
        visible = (j <= i) if main_causal else (j >= 0)

        @pl.when(visible)
        def _():
            if main_causal:
                r = lax.broadcasted_iota(jnp.int32, (tq, tk), 0) + i * tq
                c = lax.broadcasted_iota(jnp.int32, (tq, tk), 1) + j * tk
                fm = c <= r
                dm = (c // BLOCK) <= (r // BLOCK)
            else:
                fm = dm = None
            process(kf_ref, vf_ref, kd_ref, vd_ref, ck_ref[...], fm, dm)

        last = (j == i) if main_causal else (j == pl.num_programs(2) - 1)
    else:
        last = j == 0

    @pl.when(last)
    def _():
        lam = lam_ref[...]
        inv_l = 1.0 / l_ref[...]
        outs = []
        for h in range(FOX_HEADS):
            outs.append(accf_ref[:, h * HEAD_DIM:(h + 1) * HEAD_DIM] * inv_l[h])
        for hd in range(DIFF_HEADS):
            lo, hi = hd * DIFF_V, (hd + 1) * DIFF_V
            o = (accd_ref[0, :, lo:hi] * inv_l[FOX_HEADS + 2 * hd]
                 - lam * (accd_ref[1, :, lo:hi] * inv_l[FOX_HEADS + 2 * hd + 1]))
            outs.append(_rms_scale(o) * gain_ref[...] * out_scale)
        out = jnp.concatenate(outs, axis=-1)
        if zero_pad_q:
            out = jnp.where(lax.broadcasted_iota(jnp.int32, out.shape, 0) >= pad_rows, out, 0.0)
        o_ref[...] = out.astype(BF16)


def attention(q_arrays, q_rows, main, extra, lam, gain, out_scale, bsz, tq, nq, tk, te,
              main_causal, fox_extra, diff_extra, zero_pad_q, q_block0=0, pad_rows=BLOCK - N_META):
    qf, qd, cumq = q_arrays
    fw, dw = FOX_HEADS * HEAD_DIM, 2 * DIFF_HEADS * HEAD_DIM
    has_main = main is not None
    nk = main[0].shape[1] // tk if has_main else 1

    def qspec(width):
        return pl.BlockSpec((None, tq, width), lambda b, i, j: (b, i + q_block0, 0))

    in_specs = [qspec(fw), qspec(dw), qspec(128)]
    args = [qf, qd, cumq]
    if has_main:
        kf, vf, kd, vd, cumkt = main
        if main_causal:
            kmap = lambda b, i, j: (b, jnp.minimum(j, i), 0)
            cmap = lambda b, i, j: (b, jnp.minimum(j, i), 0, 0)
        else:
            kmap = lambda b, i, j: (b, j, 0)
            cmap = lambda b, i, j: (b, j, 0, 0)
        in_specs += [pl.BlockSpec((None, tk, fw), kmap), pl.BlockSpec((None, tk, fw), kmap),
                     pl.BlockSpec((None, tk, dw), kmap), pl.BlockSpec((None, tk, DIFF_HEADS * DIFF_V), kmap),
                     pl.BlockSpec((None, None, FOX_HEADS, tk), cmap)]
        args += [kf, vf, kd, vd, cumkt]
    kfe, vfe, kde, vde, eblk, cumkte = extra
    emap = lambda b, i, j: (b, eblk, 0)
    in_specs += [pl.BlockSpec((None, te, fw), emap), pl.BlockSpec((None, te, fw), emap),
                 pl.BlockSpec((None, te, dw), emap), pl.BlockSpec((None, te, DIFF_HEADS * DIFF_V), emap),
                 pl.BlockSpec((None, FOX_HEADS, te), lambda b, i, j: (b, 0, 0)),
                 _const_spec((1, 1)), _const_spec((1, DIFF_V))]
    args += [kfe, vfe, kde, vde, cumkte, lam.reshape(1, 1).astype(F32), gain.reshape(1, DIFF_V).astype(F32)]
    width = fw + DIFF_HEADS * DIFF_V
    return pl.pallas_call(
        functools.partial(_attn_kernel, tq=tq, tk=tk, te=te, has_main=has_main, main_causal=main_causal,
                          fox_extra=fox_extra, diff_extra=diff_extra, zero_pad_q=zero_pad_q, pad_rows=pad_rows,
                          out_scale=out_scale),
        grid=(bsz, nq, nk),
        in_specs=in_specs,
        out_specs=pl.BlockSpec((None, tq, width), lambda b, i, j: (b, i, 0)),
        out_shape=jax.ShapeDtypeStruct((bsz, nq * tq, width), BF16),
        scratch_shapes=[pltpu.VMEM((N_MAPS, tq, 1), F32), pltpu.VMEM((N_MAPS, tq, 1), F32),
                        pltpu.VMEM((tq, fw), F32), pltpu.VMEM((2, tq, DIFF_HEADS * DIFF_V), F32)],
        compiler_params=_cparams(("parallel", "parallel", "arbitrary")),
        name="attention",
    )(*args)


ODD_OUTS = ((512, "flat"),) * 6 + ((128, "flat"),)


def pack_odd_w_in(w_in):
    c0 = 3 * FOX_HEADS * HEAD_DIM
    f = jnp.zeros((D_MODEL, 128), w_in.dtype).at[:, :FOX_HEADS].set(w_in[:, c0:c0 + FOX_HEADS])
    return jnp.concatenate([w_in[:, :c0], w_in[:, c0 + FOX_HEADS:], f], axis=1).astype(BF16)


def diff_lambda_value(diff_lambda, lam_init):
    lq1, lk1, lq2, lk2 = diff_lambda.astype(F32)
    return jnp.exp(jnp.sum(lq1 * lk1)) - jnp.exp(jnp.sum(lq2 * lk2)) + lam_init


def _heads_on_sublanes(cum, nk, tk):
    bsz = cum.shape[0]
    return cum[:, :nk * tk, :FOX_HEADS].reshape(bsz, nk, tk, FOX_HEADS).transpose(0, 1, 3, 2)


def odd_mixer_prompt(x, gain, p, lam_init, bsz, tb, nframes, tm, tq):
    qf, kf, vf, qd, kd, vd, f = [a.reshape(bsz, tb, -1) for a in inproj(x, gain, p["w_in"], ODD_OUTS, bsz, tb, tm)]
    meta_blk = tb // BLOCK - 1
    pad = BLOCK - N_META
    logf, cum, _ = gate_cumsum(f, p["f_bias"], jnp.zeros((bsz, 1, 128), F32), bsz, tb, BLOCK, meta_blk, pad, True)
    pos = jnp.concatenate([N_META + jnp.arange(nframes, dtype=jnp.int32), jnp.zeros((pad,), jnp.int32),
                           jnp.arange(N_META, dtype=jnp.int32)])
    tables = rope_tables(pos, 2 * DIFF_HEADS)
    qd = rope(qd, tables, bsz, tb, tm)
    kd = rope(kd, tables, bsz, tb, tm)
    nk = nframes // tq
    cumkt = _heads_on_sublanes(cum, nk, tq)
    cumkte = cum[:, nframes:, :FOX_HEADS].transpose(0, 2, 1)
    lam = diff_lambda_value(p["diff_lambda"], lam_init)
    extra = (kf, vf, kd, vd, meta_blk, cumkte)
    common = dict(lam=lam, gain=p["diff_gain"], out_scale=1.0 - lam_init, bsz=bsz, te=BLOCK, diff_extra="valid")
    frames = attention((qf, qd, cum), tb, (kf, vf, kd, vd, cumkt), extra, tq=tq, nq=nk, tk=tq, main_causal=True,
                       fox_extra="valid", zero_pad_q=False, **common)
    meta = attention((qf, qd, cum), tb, None, extra, tq=BLOCK, nq=1, tk=BLOCK, main_causal=False,
                     fox_extra="causal_valid", zero_pad_q=True, q_block0=meta_blk, **common)
    mixed = jnp.concatenate([frames, meta], axis=1).reshape(bsz * tb, -1)
    return mixed, kf, vf, logf, kd, vd


def odd_mixer_sample(x, gain, p, lam_init, kf_past, vf_past, lf_past, kd_past, vd_past, bsz, tb, tk):
    qf, kf, vf, qd, kd, vd, f = [a.reshape(bsz, tb, -1) for a in inproj(x, gain, p["w_in"], ODD_OUTS, bsz, tb, tb)]
    past = kf_past.shape[1]
    nk = past // tk
    lf128 = jnp.zeros((bsz, past, 128), F32).at[:, :, :FOX_HEADS].set(lf_past.astype(F32))
    zero_c = jnp.zeros((bsz, 1, 128), F32)
    _, cum_past, total = gate_cumsum(lf128, p["f_bias"], zero_c, bsz, past, tk, 0, 0, False)
    logf, cum, _ = gate_cumsum(f, p["f_bias"], total, bsz, tb, tb, 0, 0, True)
    tables = rope_tables(past + jnp.arange(tb, dtype=jnp.int32), 2 * DIFF_HEADS)
    qd = rope(qd, tables, bsz, tb, tb)
    kd = rope(kd, tables, bsz, tb, tb)
    flat = lambda a: a.astype(F32).reshape(bsz, past, -1)
    main = (flat(kf_past), flat(vf_past), flat(kd_past), flat(vd_past), _heads_on_sublanes(cum_past, nk, tk))
    extra = (kf, vf, kd, vd, 0, cum[:, :, :FOX_HEADS].transpose(0, 2, 1))
    mixed = attention((qf, qd, cum), tb, main, extra, lam=diff_lambda_value(p["diff_lambda"], lam_init),
                      gain=p["diff_gain"], out_scale=1.0 - lam_init, bsz=bsz, tq=tb, nq=1, tk=tk, te=tb,
                      main_causal=False, fox_extra="causal", diff_extra="none", zero_pad_q=False)
    return mixed.reshape(bsz * tb, -1), kf, vf, logf, kd, vd


def kernel(x_prompt, x_sample, state_conv, state_delta, state_ssm_re, state_ssm_im, cache_fox_k, cache_fox_v,
           cache_fox_logf, cache_diff_k, cache_diff_v, meta_tokens, norm_mix, norm_ffn, norm_final, w_in_even,
           w_out_even, conv_w, gdn_a_log, gdn_dt_bias, gdn_out_norm, ssm_lambda_re, ssm_lambda_im, ssm_log_dt,
           ssm_b_re, ssm_b_im, ssm_c_re, ssm_c_im, ssm_d, ssm_glu_w, ssm_glu_b, w_in_odd, w_out_odd, fox_f_bias,
           diff_lambda, diff_out_norm, ffn_w1, ffn_w3, ffn_w2):
    bp, seq, _ = x_prompt.shape
    bs, ls, _ = x_sample.shape
    depth = norm_mix.shape[0]
    tb = seq + BLOCK
    pad = BLOCK - N_META
    tm_p = 320
    tq = 512
    meta = jnp.broadcast_to(meta_tokens.astype(F32)[None], (bp, N_META, D_MODEL))
    xp = jnp.concatenate([x_prompt.astype(F32), jnp.zeros((bp, pad, D_MODEL), F32), meta], axis=1)
    xp = xp.reshape(bp * tb, D_MODEL)
    xs = x_sample.astype(F32).reshape(bs * ls, D_MODEL)

    def logical(a):
        return jnp.concatenate([a[:, tb - N_META:], a[:, :seq]], axis=1)

    outs = {k: [] for k in ("conv_p", "conv_s", "delta_p", "delta_s", "re_p", "re_s", "im_p", "im_s", "fk_p", "fk_s",
                            "fv_p", "fv_s", "fl_p", "fl_s", "dk_p", "dk_s", "dv_p", "dv_s")}
    nchunk = FFN_HIDDEN // FFN_CHUNK
    for l in range(depth):
        i = l // 2
        if l % 2 == 0:
            pe = dict(w_in=pack_even_w_in(w_in_even[i]), conv_w=conv_w[i], a_log=gdn_a_log[i], dt_bias=gdn_dt_bias[i],
                      out_gain=gdn_out_norm[i], lam_re=ssm_lambda_re[i], lam_im=ssm_lambda_im[i],
                      log_dt=ssm_log_dt[i], b_re=ssm_b_re[i], b_im=ssm_b_im[i], c_re=ssm_c_re[i], c_im=ssm_c_im[i],
                      d=ssm_d[i], glu_w=ssm_glu_w[i], glu_b=ssm_glu_b[i])
            qkv_w = 3 * GDN_HEADS * GDN_DK
            mixed_p, qkv_p, sd_p, re_p, im_p = even_mixer(
                xp, norm_mix[l], pe, jnp.zeros((bp, CONV_W - 1, qkv_w), F32),
                jnp.zeros((bp, GDN_HEADS, GDN_DK, GDN_DK), F32), jnp.zeros((bp, SSM_GROUPS, SSM_STATE), F32),
                jnp.zeros((bp, SSM_GROUPS, SSM_STATE), F32), bp, tb, tm_p, BLOCK, tb // BLOCK - 1, 32)
            mixed_s, qkv_s, sd_s, re_s, im_s = even_mixer(
                xs, norm_mix[l], pe, state_conv[i], state_delta[i], state_ssm_re[i], state_ssm_im[i],
                bs, ls, ls, ls, 0, ls)
            outs["conv_p"].append(qkv_p.reshape(bp, tb, qkv_w)[:, seq - (CONV_W - 1):seq])
            outs["conv_s"].append(jnp.concatenate([state_conv[i].astype(F32), qkv_s.reshape(bs, ls, qkv_w)],
                                                  axis=1)[:, ls:])
            outs["delta_p"].append(sd_p); outs["delta_s"].append(sd_s)
            outs["re_p"].append(re_p); outs["re_s"].append(re_s)
            outs["im_p"].append(im_p); outs["im_s"].append(im_s)
            w_out = w_out_even[i]
        else:
            lam_init = 0.8 - 0.6 * math.exp(-0.3 * l)
            po = dict(w_in=pack_odd_w_in(w_in_odd[i]), diff_lambda=diff_lambda[i], diff_gain=diff_out_norm[i],
                      f_bias=jnp.zeros((1, 128), F32).at[0, :FOX_HEADS].set(fox_f_bias[i].astype(F32)))
            mixed_p, kf, vf, lf, kd, vd = odd_mixer_prompt(xp, norm_mix[l], po, lam_init, bp, tb, seq, tm_p, tq)
            mixed_p = [(mixed_p, mixed_p.shape[-1], "flat")]
            outs["fk_p"].append(logical(kf).reshape(bp, -1, FOX_HEADS, HEAD_DIM))
            outs["fv_p"].append(logical(vf).reshape(bp, -1, FOX_HEADS, HEAD_DIM))
            outs["fl_p"].append(logical(lf)[:, :, :FOX_HEADS])
            outs["dk_p"].append(logical(kd).reshape(bp, -1, 2 * DIFF_HEADS, HEAD_DIM))
            outs["dv_p"].append(logical(vd).reshape(bp, -1, DIFF_HEADS, DIFF_V))
            mixed_s, kf, vf, lf, kd, vd = odd_mixer_sample(
                xs, norm_mix[l], po, lam_init, cache_fox_k[i], cache_fox_v[i], cache_fox_logf[i], cache_diff_k[i],
                cache_diff_v[i], bs, ls, 512)
            mixed_s = [(mixed_s, mixed_s.shape[-1], "flat")]
            outs["fk_s"].append(kf.reshape(bs, ls, FOX_HEADS, HEAD_DIM))
            outs["fv_s"].append(vf.reshape(bs, ls, FOX_HEADS, HEAD_DIM))
            outs["fl_s"].append(lf[:, :, :FOX_HEADS])
            outs["dk_s"].append(kd.reshape(bs, ls, 2 * DIFF_HEADS, HEAD_DIM))
            outs["dv_s"].append(vd.reshape(bs, ls, DIFF_HEADS, DIFF_V))
            w_out = w_out_odd[i]
        w1 = ffn_w1[l].astype(BF16).reshape(D_MODEL, nchunk, FFN_CHUNK).transpose(1, 0, 2)
        w3 = ffn_w3[l].astype(BF16).reshape(D_MODEL, nchunk, FFN_CHUNK).transpose(1, 0, 2)
        w2 = ffn_w2[l].astype(BF16).reshape(nchunk, FFN_CHUNK, D_MODEL)
        xp = outproj_ffn(xp, mixed_p, w_out.astype(BF16), norm_ffn[l], w1, w3, w2, bp, tb, tm_p)
        xs = outproj_ffn(xs, mixed_s, w_out.astype(BF16), norm_ffn[l], w1, w3, w2, bs, ls, ls)

    y_prompt = final_norm(xp, norm_final, bp, tb, seq, 512).reshape(bp, seq, D_MODEL)
    y_sample = final_norm(xs, norm_final, bs, ls, ls, ls).reshape(bs, ls, D_MODEL)
    order = ("conv_p", "conv_s", "delta_p", "delta_s", "re_p", "re_s", "im_p", "im_s", "fk_p", "fk_s", "fv_p", "fv_s",
             "fl_p", "fl_s", "dk_p", "dk_s", "dv_p", "dv_s")
    return (y_prompt, y_sample) + tuple(jnp.stack(outs[k]) for k in order)
```

```python
import functools
import math

import jax
import jax.numpy as jnp
import numpy as np
from jax import lax
from jax.experimental import pallas as pl
from jax.experimental.pallas import tpu as pltpu

F32 = jnp.float32
BF16 = jnp.bfloat16
HIGHEST = lax.Precision.HIGHEST
LOG2E = math.log2(math.e)

D_MODEL = 1024
N_META = 16
RMS_EPS = 1e-6
HEAD_DIM = 64
GDN_HEADS = 8
GDN_DK = 128
CONV_W = 4
SSM_WIDTH = 512
SSM_GROUP = 16
SSM_GROUPS = 32
SSM_STATE = 64
FOX_HEADS = 8
DIFF_HEADS = 4
DIFF_V = 128
ROT_DIM = 16
ROPE_THETA = 500000.0
FFN_HIDDEN = 2816
FFN_CHUNK = 256
BLOCK = 64
VMEM_LIMIT = 56 * 1024 * 1024


def _cparams(sem):
    return pltpu.CompilerParams(dimension_semantics=sem, vmem_limit_bytes=VMEM_LIMIT)


def _const_spec(shape):
    nd = len(shape)
    return pl.BlockSpec(shape, lambda *_: (0,) * nd)


def _sigmoid(x):
    return 1.0 / (1.0 + jnp.exp(-x))


def _silu(x):
    return x * _sigmoid(x)


def _rms_scale(x):
    return x * lax.rsqrt(jnp.mean(x * x, axis=-1, keepdims=True) + RMS_EPS)


def _row_spec(kind, tm, width, nt):
    if kind == "flat":
        return pl.BlockSpec((tm, width), lambda b, i: (b * nt + i, 0))
    return pl.BlockSpec((tm, width), lambda b, i: (i, b))


def _row_shape(kind, bsz, tb, width, dtype):
    if kind == "flat":
        return jax.ShapeDtypeStruct((bsz * tb, width), dtype)
    return jax.ShapeDtypeStruct((tb, bsz * width), dtype)


def _inproj_kernel(x_ref, g_ref, w_ref, *o_refs, widths):
    xn = (_rms_scale(x_ref[...]) * g_ref[...]).astype(BF16)
    off = 0
    for o_ref, wd in zip(o_refs, widths):
        for c0 in range(0, wd, 512):
            cw = min(512, wd - c0)
            o_ref[:, c0:c0 + cw] = jnp.dot(xn, w_ref[:, off + c0:off + c0 + cw],
                                           preferred_element_type=F32)
        off += wd


def inproj(x, gain, w, outs, bsz, tb, tm):
    nt = tb // tm
    widths = tuple(wd for wd, _ in outs)
    return pl.pallas_call(
        functools.partial(_inproj_kernel, widths=widths),
        grid=(bsz, nt),
        in_specs=[_row_spec("flat", tm, D_MODEL, nt), _const_spec((1, D_MODEL)), _const_spec(w.shape)],
        out_specs=[_row_spec(kind, tm, wd, nt) for wd, kind in outs],
        out_shape=[_row_shape(kind, bsz, tb, wd, F32) for wd, kind in outs],
        compiler_params=_cparams(("parallel", "parallel")),
        name="inproj",
    )(x, gain.reshape(1, D_MODEL), w)


def _outffn_kernel(res_ref, *refs, a_widths, nc):
    na = len(a_widths)
    a_refs = refs[:na]
    wout_ref, g_ref, w1_ref, w3_ref, w2_ref, o_ref, acc_ref, xn_ref = refs[na:]
    x1 = res_ref[...]
    off = 0
    for a_ref, wd in zip(a_refs, a_widths):
        x1 = x1 + jnp.dot(a_ref[...].astype(BF16), wout_ref[off:off + wd, :], preferred_element_type=F32)
        off += wd
    xn_ref[...] = (_rms_scale(x1) * g_ref[...]).astype(BF16)
    acc_ref[...] = x1

    def body(c, carry):
        xn = xn_ref[...]
        h1 = jnp.dot(xn, w1_ref[c], preferred_element_type=F32)
        h3 = jnp.dot(xn, w3_ref[c], preferred_element_type=F32)
        act = (_silu(h1) * h3).astype(BF16)
        acc_ref[...] += jnp.dot(act, w2_ref[c], preferred_element_type=F32)
        return carry

    lax.fori_loop(0, nc, body, 0)
    o_ref[...] = acc_ref[...]


def outproj_ffn(res, mixed, w_out, gain, w1, w3, w2, bsz, tb, tm):
    nt = tb // tm
    nc = w1.shape[0]
    a_widths = tuple(wd for _, wd, _ in mixed)
    row = _row_spec("flat", tm, D_MODEL, nt)
    return pl.pallas_call(
        functools.partial(_outffn_kernel, a_widths=a_widths, nc=nc),
        grid=(bsz, nt),
        in_specs=[row] + [_row_spec(kind, tm, wd, nt) for _, wd, kind in mixed]
        + [_const_spec(w_out.shape), _const_spec((1, D_MODEL)), _const_spec(w1.shape), _const_spec(w3.shape),
           _const_spec(w2.shape)],
        out_specs=row,
        out_shape=jax.ShapeDtypeStruct(res.shape, F32),
        scratch_shapes=[pltpu.VMEM((tm, D_MODEL), F32), pltpu.VMEM((tm, D_MODEL), BF16)],
        compiler_params=_cparams(("parallel", "parallel")),
        name="outproj_ffn",
    )(res, *[a for a, _, _ in mixed], w_out, gain.reshape(1, D_MODEL), w1, w3, w2)


def _final_norm_kernel(x_ref, g_ref, o_ref):
    o_ref[...] = _rms_scale(x_ref[...]) * g_ref[...]


def final_norm(x, gain, bsz, tb, t_out, tm):
    spec = pl.BlockSpec((None, tm, D_MODEL), lambda b, i: (b, i, 0))
    return pl.pallas_call(
        _final_norm_kernel,
        grid=(bsz, t_out // tm),
        in_specs=[spec, _const_spec((1, D_MODEL))],
        out_specs=spec,
        out_shape=jax.ShapeDtypeStruct((bsz, t_out, D_MODEL), F32),
        compiler_params=_cparams(("parallel", "parallel")),
        name="final_norm",
    )(x.reshape(bsz, tb, D_MODEL), gain.reshape(1, D_MODEL))


def _dot_nt(a, b):
    return lax.dot_general(a, b, (((1,), (1,)), ((), ())), preferred_element_type=F32)


def _dot_hi(a, b):
    return jnp.dot(a, b, precision=HIGHEST, preferred_element_type=F32)


def _bdot(a, b):
    return lax.dot_general(a, b, (((2,), (1,)), ((0,), (0,))), preferred_element_type=F32)


def _bdot_nt(a, b):
    return lax.dot_general(a, b, (((2,), (2,)), ((0,), (0,))), preferred_element_type=F32)


def _split(a):
    hi = a.astype(BF16)
    return hi, (a - hi.astype(F32)).astype(BF16)


def _bdot_split(a, b):
    return _bdot(a[0], b[0]) + _bdot(a[1], b[0]) + _bdot(a[0], b[1])


def _unit_lower_solve(a_low, rhs, csz):
    sub = 16
    r = lax.broadcasted_iota(jnp.int32, (csz, csz), 0)
    c = lax.broadcasted_iota(jnp.int32, (csz, csz), 1)
    same = ((r // sub) == (c // sub))[None]
    eye = (r == c).astype(F32)[None]
    dg = jnp.where(same, a_low, 0.0)
    off = a_low - dg
    d1 = _split(dg)
    d2 = _split(_bdot_split(d1, d1))
    d4 = _split(_bdot_split(d2, d2))
    d8 = _split(_bdot_split(d4, d4))
    t0 = eye - dg
    for dp in (d2, d4, d8):
        t0 = t0 + _bdot_split(_split(t0), dp)
    t0 = _split(t0)
    g1 = _split(_bdot_split(t0, _split(off)))
    x = _bdot_split(t0, _split(rhs))
    nblk = csz // sub
    powers = []
    gp = g1
    k = 2
    while k < nblk:
        gp = _split(_bdot_split(gp, gp))
        powers.append(gp)
        k *= 2
    for gp in powers:
        x = x + _bdot_split(gp, _split(x))
    return x - _bdot_split(g1, _split(x))


def _gdn_kernel(qkv_ref, z_ref, ba_ref, cw_ref, alog_ref, dtb_ref, gain_ref, conv0_ref, s0_ref,
                o_ref, sout_ref, xbuf, s_ref, *, csz):
    c = pl.program_id(1)

    @pl.when(c == 0)
    def _():
        xbuf[0:8, :] = conv0_ref[...]
        s_ref[...] = s0_ref[...]

    xbuf[8:8 + csz, :] = qkv_ref[...]

    ba = ba_ref[...]
    beta = _sigmoid(ba)
    xs = ba + dtb_ref[...]
    softplus = jnp.maximum(xs, 0.0) + jnp.log(1.0 + jnp.exp(-jnp.abs(xs)))
    logg = -jnp.exp(alog_ref[...]) * softplus
    r = lax.broadcasted_iota(jnp.int32, (csz, csz), 0)
    cc = lax.broadcasted_iota(jnp.int32, (csz, csz), 1)
    causal = r >= cc
    strict = r > cc
    gc = _dot_hi(causal.astype(F32), logg)
    gct = gc.T
    gain = gain_ref[...]

    def conv_silu(col):
        y = cw_ref[3:4, col:col + GDN_DK] * xbuf[8:8 + csz, col:col + GDN_DK]
        for j in range(3):
            y = y + cw_ref[j:j + 1, col:col + GDN_DK] * xbuf[5 + j:5 + j + csz, col:col + GDN_DK]
        return _silu(y)

    heads = range(GDN_HEADS)
    q = jnp.stack([conv_silu(h * GDN_DK) for h in heads])
    k = jnp.stack([conv_silu((GDN_HEADS + h) * GDN_DK) for h in heads])
    v = jnp.stack([conv_silu((2 * GDN_HEADS + h) * GDN_DK) for h in heads])
    q = q * lax.rsqrt(jnp.sum(q * q, axis=-1, keepdims=True) + 1e-6) * (GDN_DK ** -0.5)
    k = k * lax.rsqrt(jnp.sum(k * k, axis=-1, keepdims=True) + 1e-6)
    bh = jnp.stack([beta[:, h:h + 1] for h in heads])
    gch = jnp.stack([gc[:, 8 + h:9 + h] for h in heads])
    grow = jnp.stack([gct[8 + h:9 + h, :] for h in heads])
    decay = jnp.where(causal[None], jnp.exp(gch - grow), 0.0)
    kb = k.astype(BF16)
    kk = _bdot_nt(kb, kb)
    qk = _bdot_nt(q.astype(BF16), kb)
    a_low = jnp.where(strict[None], bh * kk * decay, 0.0)
    eg = jnp.exp(gch)
    rhs = jnp.concatenate([v * bh, k * (bh * eg)], axis=-1)
    sol = _unit_lower_solve(a_low, rhs, csz)
    u = sol[:, :, :GDN_DK]
    w = sol[:, :, GDN_DK:]
    s_old = s_ref[...]
    sb = s_old.astype(BF16)
    v_new = u - _bdot(w.astype(BF16), sb)
    vb = v_new.astype(BF16)
    o = _bdot((q * eg).astype(BF16), sb) + _bdot((qk * decay).astype(BF16), vb)
    g_last = gch[:, csz - 1:csz, :]
    k2 = (k * jnp.exp(g_last - gch)).astype(BF16)
    on = _rms_scale(o) * gain
    for h in heads:
        s_ref[h] = s_old[h] * jnp.exp(g_last[h]) + jnp.dot(k2[h].T, vb[h], preferred_element_type=F32)
        zz = z_ref[:, h * GDN_DK:(h + 1) * GDN_DK]
        o_ref[:, h * GDN_DK:(h + 1) * GDN_DK] = (on[h] * _silu(zz)).astype(BF16)

    xbuf[0:8, :] = xbuf[csz:csz + 8, :]

    @pl.when(c == pl.num_programs(1) - 1)
    def _():
        sout_ref[...] = s_ref[...]


def gdn(qkv, z, ba, conv_w, a_log, dt_bias, out_gain, conv0, s0, bsz, tb, csz, first_block):
    nblk = tb // csz
    qk_w = 3 * GDN_HEADS * GDN_DK

    def rows(b, c):
        return (b * nblk + (c + first_block) % nblk, 0)

    def lane_pad(vec):
        return jnp.zeros((1, 128), F32).at[0, 8:16].set(vec.astype(F32))

    conv0p = jnp.concatenate([jnp.zeros((bsz, 8 - (CONV_W - 1), qk_w), F32), conv0.astype(F32)], axis=1)
    return pl.pallas_call(
        functools.partial(_gdn_kernel, csz=csz),
        grid=(bsz, nblk),
        in_specs=[pl.BlockSpec((csz, qk_w), rows), pl.BlockSpec((csz, GDN_HEADS * GDN_DK), rows),
                  pl.BlockSpec((csz, 128), rows), _const_spec((CONV_W, qk_w)), _const_spec((1, 128)),
                  _const_spec((1, 128)), _const_spec((1, GDN_DK)),
                  pl.BlockSpec((None, 8, qk_w), lambda b, c: (b, 0, 0)),
                  pl.BlockSpec((None, GDN_HEADS, GDN_DK, GDN_DK), lambda b, c: (b, 0, 0, 0))],
        out_specs=[pl.BlockSpec((csz, GDN_HEADS * GDN_DK), rows),
                   pl.BlockSpec((None, GDN_HEADS, GDN_DK, GDN_DK), lambda b, c: (b, 0, 0, 0))],
        out_shape=[jax.ShapeDtypeStruct((bsz * tb, GDN_HEADS * GDN_DK), BF16),
                   jax.ShapeDtypeStruct((bsz, GDN_HEADS, GDN_DK, GDN_DK), F32)],
        scratch_shapes=[pltpu.VMEM((8 + csz, qk_w), F32), pltpu.VMEM((GDN_HEADS, GDN_DK, GDN_DK), F32)],
        compiler_params=_cparams(("parallel", "arbitrary")),
        name="gdn",
    )(qkv, z, ba, conv_w.astype(F32), lane_pad(a_log), lane_pad(dt_bias), out_gain.reshape(1, GDN_DK).astype(F32),
      conv0p, s0.astype(F32))


S5_CHUNKS = 4
S5_CH_U = SSM_WIDTH // S5_CHUNKS
S5_CH_P = S5_CH_U // SSM_GROUP * SSM_STATE


def _s5_kernel(u_ref, wb_ref, wc_ref, a_ref, d_ref, gw_ref, gb_ref, st0_ref, o_ref, stout_ref, xs, st_ref,
               *, tt, bsz):
    c = pl.program_id(0)

    @pl.when(c == 0)
    def _():
        st_ref[...] = st0_ref[...]

    rows = tt * bsz
    u = u_ref[...].reshape(rows, SSM_WIDTH)
    ub = u.astype(BF16)
    for j in range(S5_CHUNKS):
        xs[j] = jnp.dot(ub[:, j * S5_CH_U:(j + 1) * S5_CH_U], wb_ref[j], preferred_element_type=F32)

    for j in range(S5_CHUNKS):
        ar = jnp.broadcast_to(a_ref[j, 0:1, :], (bsz, S5_CH_P))
        ai = jnp.broadcast_to(a_ref[j, 1:2, :], (bsz, S5_CH_P))

        def step(t, carry, j=j, ar=ar, ai=ai):
            re, im = carry
            row = pl.multiple_of(t * bsz, bsz)
            nre = ar * re - ai * im + xs[j, pl.ds(row, bsz), 0:S5_CH_P]
            nim = ar * im + ai * re + xs[j, pl.ds(row, bsz), S5_CH_P:2 * S5_CH_P]
            xs[j, pl.ds(row, bsz), 0:S5_CH_P] = nre
            xs[j, pl.ds(row, bsz), S5_CH_P:2 * S5_CH_P] = nim
            return nre, nim

        re, im = lax.fori_loop(0, tt, step, (st_ref[j, :, 0:S5_CH_P], st_ref[j, :, S5_CH_P:2 * S5_CH_P]))
        st_ref[j, :, 0:S5_CH_P] = re
        st_ref[j, :, S5_CH_P:2 * S5_CH_P] = im

    ys = [jnp.dot(xs[j].astype(BF16), wc_ref[j], preferred_element_type=F32) for j in range(S5_CHUNKS)]
    y = jnp.concatenate(ys, axis=-1) + u * d_ref[...]
    hg = 0.5 * y * (1.0 + jnp.tanh(math.sqrt(2.0 / math.pi) * (y + 0.044715 * (y * y * y))))
    gate = _sigmoid(jnp.dot(hg.astype(BF16), gw_ref[...], preferred_element_type=F32) + gb_ref[...])
    o_ref[...] = (hg * gate).reshape(tt, bsz, SSM_WIDTH)

    @pl.when(c == pl.num_programs(0) - 1)
    def _():
        stout_ref[...] = st_ref[...]


def s5(u_tb, re0, im0, lam_re, lam_im, log_dt, b_re, b_im, c_re, c_im, d_skip, glu_w, glu_b,
       bsz, tb, tt, first_tile):
    ntile = tb // tt
    lr = jnp.minimum(lam_re.astype(F32), -1e-4)
    li = lam_im.astype(F32)
    dt = jnp.exp(log_dt.astype(F32))[:, None]
    mag = jnp.exp(lr * dt)
    ar, ai = mag * jnp.cos(li * dt), mag * jnp.sin(li * dt)
    den = lr * lr + li * li
    nr, ni = ar - 1.0, ai
    cr, ci = (nr * lr + ni * li) / den, (ni * lr - nr * li) / den
    br, bi = b_re.astype(F32), b_im.astype(F32)
    bbr = cr[..., None] * br - ci[..., None] * bi
    bbi = cr[..., None] * bi + ci[..., None] * br
    gpc = S5_CH_U // SSM_GROUP

    def block_diag_in(bb):
        bb = bb.reshape(S5_CHUNKS, gpc, SSM_STATE, SSM_GROUP)
        eye = jnp.eye(gpc, dtype=F32)
        return jnp.einsum("jgpm,gh->jgmhp", bb, eye).reshape(S5_CHUNKS, S5_CH_U, S5_CH_P)

    def block_diag_out(cm):
        cm = cm.reshape(S5_CHUNKS, gpc, SSM_GROUP, SSM_STATE)
        eye = jnp.eye(gpc, dtype=F32)
        return jnp.einsum("jgmp,gh->jgphm", cm, eye).reshape(S5_CHUNKS, S5_CH_P, S5_CH_U)

    wb = jnp.concatenate([block_diag_in(bbr), block_diag_in(bbi)], axis=-1).astype(BF16)
    wc = jnp.concatenate([block_diag_out(c_re.astype(F32)), -block_diag_out(c_im.astype(F32))],
                         axis=1).astype(BF16)
    a_pack = jnp.stack([ar.reshape(S5_CHUNKS, S5_CH_P), ai.reshape(S5_CHUNKS, S5_CH_P)], axis=1)
    st0 = jnp.concatenate([re0.astype(F32).reshape(bsz, S5_CHUNKS, S5_CH_P),
                           im0.astype(F32).reshape(bsz, S5_CHUNKS, S5_CH_P)], axis=-1).transpose(1, 0, 2)

    def tile(c):
        return ((c + first_tile) % ntile, 0, 0)

    out, st = pl.pallas_call(
        functools.partial(_s5_kernel, tt=tt, bsz=bsz),
        grid=(ntile,),
        in_specs=[pl.BlockSpec((tt, bsz, SSM_WIDTH), tile), _const_spec(wb.shape), _const_spec(wc.shape),
                  _const_spec(a_pack.shape), _const_spec((1, SSM_WIDTH)), _const_spec((SSM_WIDTH, SSM_WIDTH)),
                  _const_spec((1, SSM_WIDTH)), _const_spec(st0.shape)],
        out_specs=[pl.BlockSpec((tt, bsz, SSM_WIDTH), tile), _const_spec(st0.shape)],
        out_shape=[jax.ShapeDtypeStruct((tb, bsz, SSM_WIDTH), F32), jax.ShapeDtypeStruct(st0.shape, F32)],
        scratch_shapes=[pltpu.VMEM((S5_CHUNKS, tt * bsz, 2 * S5_CH_P), F32),
                        pltpu.VMEM((S5_CHUNKS, bsz, 2 * S5_CH_P), F32)],
        compiler_params=_cparams(("arbitrary",)),
        name="s5",
    )(u_tb, wb, wc, a_pack, d_skip.reshape(1, SSM_WIDTH).astype(F32), glu_w.astype(BF16),
      glu_b.reshape(1, SSM_WIDTH).astype(F32), st0)
    st = st.transpose(1, 0, 2)
    re_new = st[..., :S5_CH_P].reshape(bsz, SSM_GROUPS, SSM_STATE)
    im_new = st[..., S5_CH_P:].reshape(bsz, SSM_GROUPS, SSM_STATE)
    return out, re_new, im_new


EVEN_OUTS = ((3 * GDN_HEADS * GDN_DK, "flat"), (GDN_HEADS * GDN_DK, "flat"), (128, "flat"), (SSM_WIDTH, "tmajor"))


def pack_even_w_in(w_in):
    qkv_w = 3 * GDN_HEADS * GDN_DK
    z_w = GDN_HEADS * GDN_DK
    c0 = qkv_w + z_w
    ba = jnp.zeros((D_MODEL, 128), w_in.dtype).at[:, :2 * GDN_HEADS].set(w_in[:, c0:c0 + 2 * GDN_HEADS])
    return jnp.concatenate([w_in[:, :c0], ba, w_in[:, c0 + 2 * GDN_HEADS:]], axis=1).astype(BF16)


def even_mixer(x, gain, p, conv0, s0, re0, im0, bsz, tb, tm, csz, first_block, tt):
    qkv, z, ba, u_t = inproj(x, gain, p["w_in"], EVEN_OUTS, bsz, tb, tm)
    o_gdn, s_new = gdn(qkv, z, ba, p["conv_w"], p["a_log"], p["dt_bias"], p["out_gain"], conv0, s0,
                       bsz, tb, csz, first_block)
    o_s5, re_new, im_new = s5(u_t.reshape(tb, bsz, SSM_WIDTH), re0, im0, p["lam_re"], p["lam_im"], p["log_dt"],
                              p["b_re"], p["b_im"], p["c_re"], p["c_im"], p["d"], p["glu_w"], p["glu_b"],
                              bsz, tb, tt, first_block * csz // tt)
    mixed = [(o_gdn, GDN_HEADS * GDN_DK, "flat"), (o_s5.reshape(tb, bsz * SSM_WIDTH), SSM_WIDTH, "tmajor")]
    return mixed, qkv, s_new, re_new, im_new


def _gate_kernel(f_ref, bias_ref, c0_ref, logf_ref, cum_ref, tot_ref, carry, *, pad_rows, log_sigmoid):
    c = pl.program_id(1)

    @pl.when(c == 0)
    def _():
        carry[...] = c0_ref[...]

    x = f_ref[...]
    if log_sigmoid:
        x = x + bias_ref[...]
        x = jnp.minimum(x, 0.0) - jnp.log(1.0 + jnp.exp(-jnp.abs(x)))
    rows = x.shape[0]
    if pad_rows:
        keep = jnp.logical_or(c > 0, lax.broadcasted_iota(jnp.int32, x.shape, 0) >= pad_rows)
        x = jnp.where(keep, x, 0.0)
    r = lax.broadcasted_iota(jnp.int32, (rows, rows), 0)
    cc = lax.broadcasted_iota(jnp.int32, (rows, rows), 1)
    cum = _dot_hi((r >= cc).astype(F32), x) + carry[...]
    logf_ref[...] = x
    cum_ref[...] = cum * LOG2E
    carry[...] = cum[rows - 1:rows, :]

    @pl.when(c == pl.num_programs(1) - 1)
    def _():
        tot_ref[...] = carry[...]


def gate_cumsum(f, bias, carry0, bsz, tb, blk, first_block, pad_rows, log_sigmoid):
    nblk = tb // blk

    def rows(b, c):
        return (b, (c + first_block) % nblk, 0)

    one = pl.BlockSpec((None, 1, 128), lambda b, c: (b, 0, 0))
    return pl.pallas_call(
        functools.partial(_gate_kernel, pad_rows=pad_rows, log_sigmoid=log_sigmoid),
        grid=(bsz, nblk),
        in_specs=[pl.BlockSpec((None, blk, 128), rows), _const_spec((1, 128)), one],
        out_specs=[pl.BlockSpec((None, blk, 128), rows), pl.BlockSpec((None, blk, 128), rows), one],
        out_shape=[jax.ShapeDtypeStruct((bsz, tb, 128), F32), jax.ShapeDtypeStruct((bsz, tb, 128), F32),
                   jax.ShapeDtypeStruct((bsz, 1, 128), F32)],
        scratch_shapes=[pltpu.VMEM((1, 128), F32)],
        compiler_params=_cparams(("parallel", "arbitrary")),
        name="gate_cumsum",
    )(f, bias, carry0)


N_BIAS_TERMS = 3


def _head_padded(x, scale, other):
    lane = lax.broadcasted_iota(jnp.int32, (x.shape[0], 128), 1)
    parts = []
    for h in range(x.shape[1] // HEAD_DIM):
        g = x[:, (h // 2) * 128:(h // 2 + 1) * 128] * scale
        own = (lane >= HEAD_DIM) == bool(h % 2)
        parts.append(jnp.where(own, g, other(h, lane)).astype(BF16))
    return jnp.concatenate(parts, axis=-1)


def _rotary(x, c_ref, s1_ref, s2_ref):
    width = x.shape[-1]
    return (x * c_ref[...] + pltpu.roll(x, width - ROT_DIM // 2, 1) * s1_ref[...]
            + pltpu.roll(x, ROT_DIM // 2, 1) * s2_ref[...])


def _prep_q_kernel(qf_ref, qd_ref, c_ref, s1_ref, s2_ref, qfa_ref, qda_ref):
    qscale = HEAD_DIM ** -0.5 * LOG2E
    ones = lambda h, lane: jnp.where(lane % HEAD_DIM < N_BIAS_TERMS, 1.0, 0.0)
    qfa_ref[...] = _head_padded(qf_ref[...], qscale, ones)
    qda_ref[...] = _head_padded(_rotary(qd_ref[...], c_ref, s1_ref, s2_ref), qscale, lambda h, lane: 0.0)


def _prep_kv_kernel(*refs, rotate):
    if rotate:
        kf_ref, vf_ref, kd_ref, vd_ref, cum_ref, c_ref, s1_ref, s2_ref, kfa_ref, vfa_ref, kdb_ref, vda_ref, kdr_ref = refs
        kd = _rotary(kd_ref[...], c_ref, s1_ref, s2_ref)
        kdr_ref[...] = kd
    else:
        kf_ref, vf_ref, kd_ref, vd_ref, cum_ref, kfa_ref, vfa_ref, kdb_ref, vda_ref = refs
        kd = kd_ref[...]
    cum = cum_ref[...]

    def offset_pieces(h, lane):
        c = jnp.broadcast_to(cum[:, h:h + 1], lane.shape)
        hi = c.astype(BF16).astype(F32)
        mid = (c - hi).astype(BF16).astype(F32)
        lo = c - hi - mid
        sel = lane % HEAD_DIM
        return jnp.where(sel == 0, -hi, jnp.where(sel == 1, -mid, jnp.where(sel == 2, -lo, 0.0)))

    kfa_ref[...] = _head_padded(kf_ref[...], 1.0, offset_pieces)
    vfa_ref[...] = _head_padded(vf_ref[...], 1.0, lambda h, lane: 1.0)
    kdb_ref[...] = kd.astype(BF16)
    vd = vd_ref[...].astype(BF16)
    ones = jnp.ones((vd.shape[0], DIFF_V), BF16)
    vda_ref[...] = jnp.concatenate(
        [a for hd in range(DIFF_HEADS) for a in (vd[:, hd * DIFF_V:(hd + 1) * DIFF_V], ones)], axis=-1)


def attn_prep_q(qf, qd, tables, bsz, tb, tm):
    width = qf.shape[-1]
    xs = pl.BlockSpec((None, tm, width), lambda i, b: (b, i, 0))
    qs = pl.BlockSpec((None, tm, 2 * width), lambda i, b: (b, i, 0))
    ts = pl.BlockSpec((tm, width), lambda i, b: (i, 0))
    shp = jax.ShapeDtypeStruct((bsz, tb, 2 * width), BF16)
    return pl.pallas_call(
        _prep_q_kernel,
        grid=(tb // tm, bsz),
        in_specs=[xs, xs, ts, ts, ts],
        out_specs=[qs, qs],
        out_shape=[shp, shp],
        compiler_params=_cparams(("parallel", "parallel")),
        name="attn_prep_q",
    )(qf, qd, *tables)


def attn_prep_kv(kf, vf, kd, vd, cum2, tables, bsz, tb, tm):
    width = kf.shape[-1]
    rotate = tables is not None
    xs = pl.BlockSpec((None, tm, width), lambda i, b: (b, i, 0))
    ws = pl.BlockSpec((None, tm, 2 * width), lambda i, b: (b, i, 0))
    cs = pl.BlockSpec((None, tm, 128), lambda i, b: (b, i, 0))
    ts = pl.BlockSpec((tm, width), lambda i, b: (i, 0))
    shp = lambda w, dt: jax.ShapeDtypeStruct((bsz, tb, w), dt)
    outs = pl.pallas_call(
        functools.partial(_prep_kv_kernel, rotate=rotate),
        grid=(tb // tm, bsz),
        in_specs=[xs] * 4 + [cs] + ([ts] * 3 if rotate else []),
        out_specs=[ws, ws, xs, ws] + ([xs] if rotate else []),
        out_shape=[shp(2 * width, BF16), shp(2 * width, BF16), shp(width, BF16), shp(2 * width, BF16)]
        + ([shp(width, F32)] if rotate else []),
        compiler_params=_cparams(("parallel", "parallel")),
        name="attn_prep_kv",
    )(kf, vf, kd, vd, cum2, *(tables if rotate else ()))
    return outs if rotate else list(outs) + [None]


def rope_tables(pos, nheads):
    half = ROT_DIM // 2
    inv_freq = ROPE_THETA ** (-jnp.arange(0, ROT_DIM, 2, dtype=F32) / ROT_DIM)
    ang = pos.astype(F32)[:, None] * inv_freq[None, :]
    cos, sin = jnp.cos(ang), jnp.sin(ang)
    t = pos.shape[0]
    rest = jnp.zeros((t, HEAD_DIM - ROT_DIM), F32)
    z = jnp.zeros((t, half), F32)
    ctab = jnp.concatenate([cos, cos, rest + 1.0], axis=1)
    s1 = jnp.concatenate([-sin, z, rest], axis=1)
    s2 = jnp.concatenate([z, sin, rest], axis=1)
    return tuple(jnp.tile(a, (1, nheads)) for a in (ctab, s1, s2))


NEG_INF = -1e30
N_MAPS = FOX_HEADS + 2 * DIFF_HEADS


def _attn_kernel(*refs, tq, tk, te, has_main, main_causal, fox_extra, diff_extra, zero_pad_q, pad_rows,
                 out_scale):
    if has_main:
        (qf_ref, qd_ref, cq_ref, kf_ref, vf_ref, kd_ref, vd_ref,
         kfe_ref, vfe_ref, kde_ref, vde_ref, lam_ref, gain_ref,
         o_ref, m_ref, cqb_ref, accf_ref, accd_ref) = refs
    else:
        (qf_ref, qd_ref, cq_ref, kfe_ref, vfe_ref, kde_ref, vde_ref, lam_ref, gain_ref,
         o_ref, m_ref, cqb_ref, accf_ref, accd_ref) = refs
    i = pl.program_id(1)
    j = pl.program_id(2)
    neg = NEG_INF * LOG2E

    def lanes(x, ncols):
        reps, rem = divmod(ncols, 128)
        return jnp.concatenate([x] * reps + ([x[:, :rem]] if rem else []), axis=-1)

    def one_map(idx, q, k, v, acc_ref, slot, cq, mask):
        t = _dot_nt(q, k)
        ncols = t.shape[-1]
        if mask is not None:
            t = jnp.where(mask, t, neg)
        mt = jnp.broadcast_to(jnp.max(t, axis=-1, keepdims=True), (tq, 128))
        m_prev = m_ref[idx]
        if cq is None:
            m_new = jnp.maximum(m_prev, mt)
            shift = m_new
        else:
            m_new = jnp.maximum(m_prev, mt + cq)
            shift = m_new - cq
        p = jnp.exp2(t - lanes(shift, ncols))
        alpha = jnp.exp2(m_prev - m_new)
        m_ref[idx] = m_new
        acc_ref[slot] = (lanes(alpha, acc_ref.shape[-1]) * acc_ref[slot]
                         + jnp.dot(p.astype(BF16), v, preferred_element_type=F32))

    def process(k_tiles, fox_mask, diff_mask):
        def rows(pick, lo, hi):
            parts = [pick(tile)[:, lo:hi] for tile in k_tiles]
            return parts[0] if len(parts) == 1 else jnp.concatenate(parts, axis=0)

        for h in range(FOX_HEADS):
            lo, hi = h * 128, (h + 1) * 128
            one_map(h, qf_ref[:, lo:hi], rows(lambda t: t[0], lo, hi), rows(lambda t: t[1], lo, hi),
                    accf_ref, h, cqb_ref[h], fox_mask)
        for hm in range(2 * DIFF_HEADS):
            g = hm // 2
            one_map(FOX_HEADS + hm, qd_ref[:, hm * 128:(hm + 1) * 128],
                    rows(lambda t: t[2], g * 128, (g + 1) * 128),
                    rows(lambda t: t[3], g * 2 * DIFF_V, (g + 1) * 2 * DIFF_V), accd_ref, hm, None, diff_mask)

    def visible(extra_kind, main_kind, width):
        if extra_kind == "none" and main_kind == "none":
            return None
        r = lax.broadcasted_iota(jnp.int32, (tq, width), 0)
        c = lax.broadcasted_iota(jnp.int32, (tq, width), 1)
        cm = c - te
        rules = {"valid": c >= pad_rows, "causal": c <= r,
                 "causal_valid": jnp.logical_and(c <= r, c >= pad_rows),
                 "main_causal": cm <= r,
                 "main_block_causal": cm < (lax.shift_right_logical(r, 6) + 1) * BLOCK}
        in_extra = c < te
        if main_kind == "none":
            return jnp.logical_or(c >= te, rules[extra_kind])
        if extra_kind == "none":
            return jnp.logical_or(in_extra, rules[main_kind])
        return jnp.logical_or(jnp.logical_and(in_extra, rules[extra_kind]),
                              jnp.logical_and(c >= te, rules[main_kind]))

    @pl.when(j == 0)
    def _():
        m_ref[...] = jnp.full(m_ref.shape, NEG_INF, F32)
        accf_ref[...] = jnp.zeros(accf_ref.shape, F32)
        accd_ref[...] = jnp.zeros(accd_ref.shape, F32)
        for h in range(FOX_HEADS):
            cqb_ref[h] = jnp.broadcast_to(cq_ref[:, h:h + 1], (tq, 128))

    extra_tile = (kfe_ref, vfe_ref, kde_ref, vde_ref)
    if has_main:
        main_tile = (kf_ref, vf_ref, kd_ref, vd_ref)
        if main_causal:
            assert tq == tk
            last, before_last = j == i, j < i
            fox_main, diff_main = "main_causal", "main_block_causal"
        else:
            last, before_last = j == pl.num_programs(2) - 1, j < pl.num_programs(2) - 1
            fox_main = diff_main = "none"

        @pl.when(before_last)
        def _():
            process([main_tile], None, None)

        @pl.when(last)
        def _():
            process([extra_tile, main_tile], visible(fox_extra, fox_main, te + tk),
                    visible(diff_extra, diff_main, te + tk))
    else:
        last = j == 0
        process([extra_tile], visible(fox_extra, "none", te), visible(diff_extra, "none", te))

    @pl.when(last)
    def _():
        lam = lam_ref[...]
        upper = lax.broadcasted_iota(jnp.int32, (tq, 128), 1) >= HEAD_DIM
        outs = []
        for g in range(FOX_HEADS // 2):
            a0, a1 = accf_ref[2 * g], accf_ref[2 * g + 1]
            outs.append(jnp.where(upper, a1 / pltpu.roll(a1, HEAD_DIM, 1), a0 / pltpu.roll(a0, HEAD_DIM, 1)))
        for hd in range(DIFF_HEADS):
            a1, a2 = accd_ref[2 * hd], accd_ref[2 * hd + 1]
            o = a1[:, :DIFF_V] / a1[:, DIFF_V:] - lam * (a2[:, :DIFF_V] / a2[:, DIFF_V:])
            outs.append(_rms_scale(o) * gain_ref[...] * out_scale)
        out = jnp.concatenate(outs, axis=-1)
        if zero_pad_q:
            out = jnp.where(lax.broadcasted_iota(jnp.int32, out.shape, 0) >= pad_rows, out, 0.0)
        o_ref[...] = out.astype(BF16)


def attention(q_arrays, q_rows, main, extra, lam, gain, out_scale, bsz, tq, nq, tk, te,
              main_causal, fox_extra, diff_extra, zero_pad_q, q_block0=0, pad_rows=BLOCK - N_META):
    qf, qd, cumq = q_arrays
    has_main = main is not None
    nk = main[0].shape[1] // tk if has_main else 1
    kv_widths = (FOX_HEADS * 128, FOX_HEADS * 128, 2 * DIFF_HEADS * HEAD_DIM, DIFF_HEADS * 2 * DIFF_V)

    def qspec(width):
        return pl.BlockSpec((None, tq, width), lambda b, i, j: (b, i + q_block0, 0))

    in_specs = [qspec(FOX_HEADS * 128), qspec(2 * DIFF_HEADS * 128), qspec(128)]
    args = [qf, qd, cumq]
    if has_main:
        if main_causal:
            kmap = lambda b, i, j: (b, jnp.minimum(j, i), 0)
        else:
            kmap = lambda b, i, j: (b, j, 0)
        in_specs += [pl.BlockSpec((None, tk, w), kmap) for w in kv_widths]
        args += list(main)
    eblk = extra[4]
    emap = lambda b, i, j: (b, eblk, 0)
    in_specs += [pl.BlockSpec((None, te, w), emap) for w in kv_widths] + [_const_spec((1, 1)), _const_spec((1, DIFF_V))]
    args += list(extra[:4]) + [lam.reshape(1, 1).astype(F32), gain.reshape(1, DIFF_V).astype(F32)]
    width = FOX_HEADS * HEAD_DIM + DIFF_HEADS * DIFF_V
    return pl.pallas_call(
        functools.partial(_attn_kernel, tq=tq, tk=tk, te=te, has_main=has_main, main_causal=main_causal,
                          fox_extra=fox_extra, diff_extra=diff_extra, zero_pad_q=zero_pad_q, pad_rows=pad_rows,
                          out_scale=out_scale),
        grid=(bsz, nq, nk),
        in_specs=in_specs,
        out_specs=pl.BlockSpec((None, tq, width), lambda b, i, j: (b, i, 0)),
        out_shape=jax.ShapeDtypeStruct((bsz, nq * tq, width), BF16),
        scratch_shapes=[pltpu.VMEM((N_MAPS, tq, 128), F32), pltpu.VMEM((FOX_HEADS, tq, 128), F32),
                        pltpu.VMEM((FOX_HEADS, tq, 128), F32), pltpu.VMEM((2 * DIFF_HEADS, tq, 2 * DIFF_V), F32)],
        compiler_params=_cparams(("parallel", "parallel", "arbitrary")),
        name="attention",
    )(*args)


ODD_OUTS = ((512, "flat"),) * 6 + ((128, "flat"),)


def pack_odd_w_in(w_in):
    c0 = 3 * FOX_HEADS * HEAD_DIM
    f = jnp.zeros((D_MODEL, 128), w_in.dtype).at[:, :FOX_HEADS].set(w_in[:, c0:c0 + FOX_HEADS])
    return jnp.concatenate([w_in[:, :c0], w_in[:, c0 + FOX_HEADS:], f], axis=1).astype(BF16)


def diff_lambda_value(diff_lambda, lam_init):
    lq1, lk1, lq2, lk2 = diff_lambda.astype(F32)
    return jnp.exp(jnp.sum(lq1 * lk1)) - jnp.exp(jnp.sum(lq2 * lk2)) + lam_init


def odd_mixer_prompt(x, gain, p, lam_init, bsz, tb, nframes, tm, tq):
    qf, kf, vf, qd, kd, vd, f = [a.reshape(bsz, tb, -1) for a in inproj(x, gain, p["w_in"], ODD_OUTS, bsz, tb, tm)]
    meta_blk = tb // BLOCK - 1
    pad = BLOCK - N_META
    logf, cum, _ = gate_cumsum(f, p["f_bias"], jnp.zeros((bsz, 1, 128), F32), bsz, tb, BLOCK, meta_blk, pad, True)
    pos = jnp.concatenate([N_META + jnp.arange(nframes, dtype=jnp.int32), jnp.zeros((pad,), jnp.int32),
                           jnp.arange(N_META, dtype=jnp.int32)])
    tables = rope_tables(pos, 2 * DIFF_HEADS)
    qfa, qda = attn_prep_q(qf, qd, tables, bsz, tb, tm)
    kfa, vfa, kdb, vda, kd = attn_prep_kv(kf, vf, kd, vd, cum, tables, bsz, tb, tm)
    nk = nframes // tq
    lam = diff_lambda_value(p["diff_lambda"], lam_init)
    extra = (kfa, vfa, kdb, vda, meta_blk)
    common = dict(lam=lam, gain=p["diff_gain"], out_scale=1.0 - lam_init, bsz=bsz, te=BLOCK, diff_extra="valid")
    frames = attention((qfa, qda, cum), tb, (kfa, vfa, kdb, vda), extra, tq=tq, nq=nk, tk=tq,
                       main_causal=True, fox_extra="valid", zero_pad_q=False, **common)
    meta = attention((qfa, qda, cum), tb, None, extra, tq=BLOCK, nq=1, tk=BLOCK, main_causal=False,
                     fox_extra="causal_valid", zero_pad_q=True, q_block0=meta_blk, **common)
    mixed = jnp.concatenate([frames, meta], axis=1).reshape(bsz * tb, -1)
    return mixed, kf, vf, logf, kd, vd


def odd_mixer_sample(x, gain, p, lam_init, kf_past, vf_past, lf_past, kd_past, vd_past, bsz, tb, tk):
    qf, kf, vf, qd, kd, vd, f = [a.reshape(bsz, tb, -1) for a in inproj(x, gain, p["w_in"], ODD_OUTS, bsz, tb, tb)]
    past = kf_past.shape[1]
    nk = past // tk
    lf128 = jnp.zeros((bsz, past, 128), F32).at[:, :, :FOX_HEADS].set(lf_past.astype(F32))
    zero_c = jnp.zeros((bsz, 1, 128), F32)
    _, cum_past, total = gate_cumsum(lf128, p["f_bias"], zero_c, bsz, past, tk, 0, 0, False)
    logf, cum, _ = gate_cumsum(f, p["f_bias"], total, bsz, tb, tb, 0, 0, True)
    tables = rope_tables(past + jnp.arange(tb, dtype=jnp.int32), 2 * DIFF_HEADS)
    qfa, qda = attn_prep_q(qf, qd, tables, bsz, tb, tb)
    kfa, vfa, kdb, vda, kd = attn_prep_kv(kf, vf, kd, vd, cum, tables, bsz, tb, tb)
    flat = lambda a: a.astype(F32).reshape(bsz, past, -1)
    main = tuple(attn_prep_kv(flat(kf_past), flat(vf_past), flat(kd_past), flat(vd_past), cum_past, None,
                              bsz, past, tk)[:4])
    extra = (kfa, vfa, kdb, vda, 0)
    mixed = attention((qfa, qda, cum), tb, main, extra, lam=diff_lambda_value(p["diff_lambda"], lam_init),
                      gain=p["diff_gain"], out_scale=1.0 - lam_init, bsz=bsz, tq=tb, nq=1, tk=tk, te=tb,
                      main_causal=False, fox_extra="causal", diff_extra="none", zero_pad_q=False)
    return mixed.reshape(bsz * tb, -1), kf, vf, logf, kd, vd


def kernel(x_prompt, x_sample, state_conv, state_delta, state_ssm_re, state_ssm_im, cache_fox_k, cache_fox_v,
           cache_fox_logf, cache_diff_k, cache_diff_v, meta_tokens, norm_mix, norm_ffn, norm_final, w_in_even,
           w_out_even, conv_w, gdn_a_log, gdn_dt_bias, gdn_out_norm, ssm_lambda_re, ssm_lambda_im, ssm_log_dt,
           ssm_b_re, ssm_b_im, ssm_c_re, ssm_c_im, ssm_d, ssm_glu_w, ssm_glu_b, w_in_odd, w_out_odd, fox_f_bias,
           diff_lambda, diff_out_norm, ffn_w1, ffn_w3, ffn_w2):
    bp, seq, _ = x_prompt.shape
    bs, ls, _ = x_sample.shape
    depth = norm_mix.shape[0]
    tb = seq + BLOCK
    pad = BLOCK - N_META
    tm_p = 320
    tq = 512
    meta = jnp.broadcast_to(meta_tokens.astype(F32)[None], (bp, N_META, D_MODEL))
    xp = jnp.concatenate([x_prompt.astype(F32), jnp.zeros((bp, pad, D_MODEL), F32), meta], axis=1)
    xp = xp.reshape(bp * tb, D_MODEL)
    xs = x_sample.astype(F32).reshape(bs * ls, D_MODEL)

    def logical(a):
        return jnp.concatenate([a[:, tb - N_META:], a[:, :seq]], axis=1)

    outs = {k: [] for k in ("conv_p", "conv_s", "delta_p", "delta_s", "re_p", "re_s", "im_p", "im_s", "fk_p", "fk_s",
                            "fv_p", "fv_s", "fl_p", "fl_s", "dk_p", "dk_s", "dv_p", "dv_s")}
    nchunk = FFN_HIDDEN // FFN_CHUNK
    for l in range(depth):
        i = l // 2
        if l % 2 == 0:
            pe = dict(w_in=pack_even_w_in(w_in_even[i]), conv_w=conv_w[i], a_log=gdn_a_log[i], dt_bias=gdn_dt_bias[i],
                      out_gain=gdn_out_norm[i], lam_re=ssm_lambda_re[i], lam_im=ssm_lambda_im[i],
                      log_dt=ssm_log_dt[i], b_re=ssm_b_re[i], b_im=ssm_b_im[i], c_re=ssm_c_re[i], c_im=ssm_c_im[i],
                      d=ssm_d[i], glu_w=ssm_glu_w[i], glu_b=ssm_glu_b[i])
            qkv_w = 3 * GDN_HEADS * GDN_DK
            mixed_p, qkv_p, sd_p, re_p, im_p = even_mixer(
                xp, norm_mix[l], pe, jnp.zeros((bp, CONV_W - 1, qkv_w), F32),
                jnp.zeros((bp, GDN_HEADS, GDN_DK, GDN_DK), F32), jnp.zeros((bp, SSM_GROUPS, SSM_STATE), F32),
                jnp.zeros((bp, SSM_GROUPS, SSM_STATE), F32), bp, tb, tm_p, BLOCK, tb // BLOCK - 1, 32)
            mixed_s, qkv_s, sd_s, re_s, im_s = even_mixer(
                xs, norm_mix[l], pe, state_conv[i], state_delta[i], state_ssm_re[i], state_ssm_im[i],
                bs, ls, ls, ls, 0, ls)
            outs["conv_p"].append(qkv_p.reshape(bp, tb, qkv_w)[:, seq - (CONV_W - 1):seq])
            outs["conv_s"].append(jnp.concatenate([state_conv[i].astype(F32), qkv_s.reshape(bs, ls, qkv_w)],
                                                  axis=1)[:, ls:])
            outs["delta_p"].append(sd_p); outs["delta_s"].append(sd_s)
            outs["re_p"].append(re_p); outs["re_s"].append(re_s)
            outs["im_p"].append(im_p); outs["im_s"].append(im_s)
            w_out = w_out_even[i]
        else:
            lam_init = 0.8 - 0.6 * math.exp(-0.3 * l)
            po = dict(w_in=pack_odd_w_in(w_in_odd[i]), diff_lambda=diff_lambda[i], diff_gain=diff_out_norm[i],
                      f_bias=jnp.zeros((1, 128), F32).at[0, :FOX_HEADS].set(fox_f_bias[i].astype(F32)))
            mixed_p, kf, vf, lf, kd, vd = odd_mixer_prompt(xp, norm_mix[l], po, lam_init, bp, tb, seq, tm_p, tq)
            mixed_p = [(mixed_p, mixed_p.shape[-1], "flat")]
            outs["fk_p"].append(logical(kf).reshape(bp, -1, FOX_HEADS, HEAD_DIM))
            outs["fv_p"].append(logical(vf).reshape(bp, -1, FOX_HEADS, HEAD_DIM))
            outs["fl_p"].append(logical(lf)[:, :, :FOX_HEADS])
            outs["dk_p"].append(logical(kd).reshape(bp, -1, 2 * DIFF_HEADS, HEAD_DIM))
            outs["dv_p"].append(logical(vd).reshape(bp, -1, DIFF_HEADS, DIFF_V))
            mixed_s, kf, vf, lf, kd, vd = odd_mixer_sample(
                xs, norm_mix[l], po, lam_init, cache_fox_k[i], cache_fox_v[i], cache_fox_logf[i], cache_diff_k[i],
                cache_diff_v[i], bs, ls, 512)
            mixed_s = [(mixed_s, mixed_s.shape[-1], "flat")]
            outs["fk_s"].append(kf.reshape(bs, ls, FOX_HEADS, HEAD_DIM))
            outs["fv_s"].append(vf.reshape(bs, ls, FOX_HEADS, HEAD_DIM))
            outs["fl_s"].append(lf[:, :, :FOX_HEADS])
            outs["dk_s"].append(kd.reshape(bs, ls, 2 * DIFF_HEADS, HEAD_DIM))
            outs["dv_s"].append(vd.reshape(bs, ls, DIFF_HEADS, DIFF_V))
            w_out = w_out_odd[i]
        w1 = ffn_w1[l].astype(BF16).reshape(D_MODEL, nchunk, FFN_CHUNK).transpose(1, 0, 2)
        w3 = ffn_w3[l].astype(BF16).reshape(D_MODEL, nchunk, FFN_CHUNK).transpose(1, 0, 2)
        w2 = ffn_w2[l].astype(BF16).reshape(nchunk, FFN_CHUNK, D_MODEL)
        xp = outproj_ffn(xp, mixed_p, w_out.astype(BF16), norm_ffn[l], w1, w3, w2, bp, tb, tm_p)
        xs = outproj_ffn(xs, mixed_s, w_out.astype(BF16), norm_ffn[l], w1, w3, w2, bs, ls, ls)

    y_prompt = final_norm(xp, norm_final, bp, tb, seq, 512).reshape(bp, seq, D_MODEL)
    y_sample = final_norm(xs, norm_final, bs, ls, ls, ls).reshape(bs, ls, D_MODEL)
    order = ("conv_p", "conv_s", "delta_p", "delta_s", "re_p", "re_s", "im_p", "im_s", "fk_p", "fk_s", "fv_p", "fv_s",
             "fl_p", "fl_s", "dk_p", "dk_s", "dv_p", "dv_s")
    return (y_prompt, y_sample) + tuple(jnp.stack(outs[k]) for k in order)
```

```python
import functools
import math

import jax
import jax.numpy as jnp
import numpy as np
from jax import lax
from jax.experimental import pallas as pl
from jax.experimental.pallas import tpu as pltpu

F32 = jnp.float32
BF16 = jnp.bfloat16
HIGHEST = lax.Precision.HIGHEST
LOG2E = math.log2(math.e)

D_MODEL = 1024
N_META = 16
RMS_EPS = 1e-6
HEAD_DIM = 64
GDN_HEADS = 8
GDN_DK = 128
CONV_W = 4
SSM_WIDTH = 512
SSM_GROUP = 16
SSM_GROUPS = 32
SSM_STATE = 64
FOX_HEADS = 8
DIFF_HEADS = 4
DIFF_V = 128
ROT_DIM = 16
ROPE_THETA = 500000.0
FFN_HIDDEN = 2816
FFN_CHUNK = 256
BLOCK = 64
VMEM_LIMIT = 56 * 1024 * 1024


def _cparams(sem):
    return pltpu.CompilerParams(dimension_semantics=sem, vmem_limit_bytes=VMEM_LIMIT)


def _const_spec(shape, single_buffer=False):
    nd = len(shape)
    if single_buffer:
        return pl.BlockSpec(shape, lambda *_: (0,) * nd, pipeline_mode=pl.Buffered(1))
    return pl.BlockSpec(shape, lambda *_: (0,) * nd)


def _sigmoid(x):
    return 1.0 / (1.0 + jnp.exp(-x))


def _silu(x):
    return x * _sigmoid(x)


def _rms_scale(x):
    return x * lax.rsqrt(jnp.mean(x * x, axis=-1, keepdims=True) + RMS_EPS)


def _row_spec(kind, tm, width, nt):
    if kind == "flat":
        return pl.BlockSpec((tm, width), lambda b, i: (b * nt + i, 0))
    return pl.BlockSpec((tm, width), lambda b, i: (i, b))


def _row_shape(kind, bsz, tb, width, dtype):
    if kind == "flat":
        return jax.ShapeDtypeStruct((bsz * tb, width), dtype)
    return jax.ShapeDtypeStruct((tb, bsz * width), dtype)


def _inproj_kernel(x_ref, g_ref, w_ref, *o_refs, widths):
    xn = (_rms_scale(x_ref[...]) * g_ref[...]).astype(BF16)
    off = 0
    for o_ref, wd in zip(o_refs, widths):
        for c0 in range(0, wd, 512):
            cw = min(512, wd - c0)
            o_ref[:, c0:c0 + cw] = jnp.dot(xn, w_ref[:, off + c0:off + c0 + cw],
                                           preferred_element_type=F32)
        off += wd


def inproj(x, gain, w, outs, bsz, tb, tm):
    nt = tb // tm
    widths = tuple(wd for wd, _ in outs)
    return pl.pallas_call(
        functools.partial(_inproj_kernel, widths=widths),
        grid=(bsz, nt),
        in_specs=[_row_spec("flat", tm, D_MODEL, nt), _const_spec((1, D_MODEL)), _const_spec(w.shape, True)],
        out_specs=[_row_spec(kind, tm, wd, nt) for wd, kind in outs],
        out_shape=[_row_shape(kind, bsz, tb, wd, F32) for wd, kind in outs],
        compiler_params=_cparams(("parallel", "parallel")),
        name="inproj",
    )(x, gain.reshape(1, D_MODEL), w)


def _outffn_kernel(res_ref, *refs, a_widths, nc):
    na = len(a_widths)
    a_refs = refs[:na]
    wout_ref, g_ref, w1_ref, w3_ref, w2_ref, o_ref, acc_ref, xn_ref = refs[na:]
    x1 = res_ref[...]
    off = 0
    for a_ref, wd in zip(a_refs, a_widths):
        x1 = x1 + jnp.dot(a_ref[...].astype(BF16), wout_ref[off:off + wd, :], preferred_element_type=F32)
        off += wd
    xn_ref[...] = (_rms_scale(x1) * g_ref[...]).astype(BF16)
    acc_ref[...] = x1

    def body(c, carry):
        xn = xn_ref[...]
        h1 = jnp.dot(xn, w1_ref[c], preferred_element_type=F32)
        h3 = jnp.dot(xn, w3_ref[c], preferred_element_type=F32)
        act = (_silu(h1) * h3).astype(BF16)
        acc_ref[...] += jnp.dot(act, w2_ref[c], preferred_element_type=F32)
        return carry

    lax.fori_loop(0, nc, body, 0)
    o_ref[...] = acc_ref[...]


def outproj_ffn(res, mixed, w_out, gain, w1, w3, w2, bsz, tb, tm):
    nt = tb // tm
    nc = w1.shape[0]
    a_widths = tuple(wd for _, wd, _ in mixed)
    row = _row_spec("flat", tm, D_MODEL, nt)
    return pl.pallas_call(
        functools.partial(_outffn_kernel, a_widths=a_widths, nc=nc),
        grid=(bsz, nt),
        in_specs=[row] + [_row_spec(kind, tm, wd, nt) for _, wd, kind in mixed]
        + [_const_spec(w_out.shape, True), _const_spec((1, D_MODEL)), _const_spec(w1.shape, True),
           _const_spec(w3.shape, True), _const_spec(w2.shape, True)],
        out_specs=row,
        out_shape=jax.ShapeDtypeStruct(res.shape, F32),
        scratch_shapes=[pltpu.VMEM((tm, D_MODEL), F32), pltpu.VMEM((tm, D_MODEL), BF16)],
        compiler_params=_cparams(("parallel", "parallel")),
        name="outproj_ffn",
    )(res, *[a for a, _, _ in mixed], w_out, gain.reshape(1, D_MODEL), w1, w3, w2)


def _final_norm_kernel(x_ref, g_ref, o_ref):
    o_ref[...] = _rms_scale(x_ref[...]) * g_ref[...]


def final_norm(x, gain, bsz, tb, t_out, tm):
    spec = pl.BlockSpec((None, tm, D_MODEL), lambda b, i: (b, i, 0))
    return pl.pallas_call(
        _final_norm_kernel,
        grid=(bsz, t_out // tm),
        in_specs=[spec, _const_spec((1, D_MODEL))],
        out_specs=spec,
        out_shape=jax.ShapeDtypeStruct((bsz, t_out, D_MODEL), F32),
        compiler_params=_cparams(("parallel", "parallel")),
        name="final_norm",
    )(x.reshape(bsz, tb, D_MODEL), gain.reshape(1, D_MODEL))


def _dot_nt(a, b):
    return lax.dot_general(a, b, (((1,), (1,)), ((), ())), preferred_element_type=F32)


def _dot_hi(a, b):
    return jnp.dot(a, b, precision=HIGHEST, preferred_element_type=F32)


def _bdot(a, b):
    return lax.dot_general(a, b, (((2,), (1,)), ((0,), (0,))), preferred_element_type=F32)


def _bdot_nt(a, b):
    return lax.dot_general(a, b, (((2,), (2,)), ((0,), (0,))), preferred_element_type=F32)


def _split(a):
    hi = a.astype(BF16)
    return hi, (a - hi.astype(F32)).astype(BF16)


def _bdot_split(a, b):
    return _bdot(a[0], b[0]) + _bdot(a[1], b[0]) + _bdot(a[0], b[1])


def _unit_lower_solve(a_low, rhs, csz):
    sub = 16
    r = lax.broadcasted_iota(jnp.int32, (csz, csz), 0)
    c = lax.broadcasted_iota(jnp.int32, (csz, csz), 1)
    same = ((r // sub) == (c // sub))[None]
    eye = (r == c).astype(F32)[None]
    dg = jnp.where(same, a_low, 0.0)
    off = a_low - dg
    d1 = _split(dg)
    d2 = _split(_bdot_split(d1, d1))
    d4 = _split(_bdot_split(d2, d2))
    d8 = _split(_bdot_split(d4, d4))
    t0 = eye - dg
    for dp in (d2, d4, d8):
        t0 = t0 + _bdot_split(_split(t0), dp)
    t0 = _split(t0)
    g1 = _split(_bdot_split(t0, _split(off)))
    x = _bdot_split(t0, _split(rhs))
    nblk = csz // sub
    powers = []
    gp = g1
    k = 2
    while k < nblk:
        gp = _split(_bdot_split(gp, gp))
        powers.append(gp)
        k *= 2
    for gp in powers:
        x = x + _bdot_split(gp, _split(x))
    return x - _bdot_split(g1, _split(x))


def _gdn_kernel(qkv_ref, z_ref, ba_ref, cw_ref, alog_ref, dtb_ref, gain_ref, conv0_ref, s0_ref,
                o_ref, sout_ref, xbuf, s_ref, *, csz, nb):
    c = pl.program_id(1)
    nh = nb * GDN_HEADS

    @pl.when(c == 0)
    def _():
        xbuf[:, 0:8, :] = conv0_ref[...]
        s_ref[...] = s0_ref[...].reshape(nh, GDN_DK, GDN_DK)

    xbuf[:, 8:8 + csz, :] = qkv_ref[...]

    r = lax.broadcasted_iota(jnp.int32, (csz, csz), 0)
    cc = lax.broadcasted_iota(jnp.int32, (csz, csz), 1)
    causal = r >= cc
    strict = r > cc
    gain = gain_ref[...]
    beta, gc, gct = [], [], []
    for bb in range(nb):
        ba = ba_ref[bb]
        beta.append(_sigmoid(ba))
        xs = ba + dtb_ref[...]
        softplus = jnp.maximum(xs, 0.0) + jnp.log(1.0 + jnp.exp(-jnp.abs(xs)))
        logg = -jnp.exp(alog_ref[...]) * softplus
        gc.append(_dot_hi(causal.astype(F32), logg))
        gct.append(gc[bb].T)

    def conv_silu(bb, col):
        y = cw_ref[3:4, col:col + GDN_DK] * xbuf[bb, 8:8 + csz, col:col + GDN_DK]
        for j in range(3):
            y = y + cw_ref[j:j + 1, col:col + GDN_DK] * xbuf[bb, 5 + j:5 + j + csz, col:col + GDN_DK]
        return _silu(y)

    heads = [(bb, h) for bb in range(nb) for h in range(GDN_HEADS)]
    q = jnp.stack([conv_silu(bb, h * GDN_DK) for bb, h in heads])
    k = jnp.stack([conv_silu(bb, (GDN_HEADS + h) * GDN_DK) for bb, h in heads])
    v = jnp.stack([conv_silu(bb, (2 * GDN_HEADS + h) * GDN_DK) for bb, h in heads])
    q = q * lax.rsqrt(jnp.sum(q * q, axis=-1, keepdims=True) + 1e-6) * (GDN_DK ** -0.5)
    k = k * lax.rsqrt(jnp.sum(k * k, axis=-1, keepdims=True) + 1e-6)
    bh = jnp.stack([beta[bb][:, h:h + 1] for bb, h in heads])
    gch = jnp.stack([gc[bb][:, 8 + h:9 + h] for bb, h in heads])
    grow = jnp.stack([gct[bb][8 + h:9 + h, :] for bb, h in heads])
    decay = jnp.where(causal[None], jnp.exp(gch - grow), 0.0)
    kb = k.astype(BF16)
    kk = _bdot_nt(kb, kb)
    qk = _bdot_nt(q.astype(BF16), kb)
    a_low = jnp.where(strict[None], bh * kk * decay, 0.0)
    eg = jnp.exp(gch)
    rhs = jnp.concatenate([v * bh, k * (bh * eg)], axis=-1)
    sol = _unit_lower_solve(a_low, rhs, csz)
    u = sol[:, :, :GDN_DK]
    w = sol[:, :, GDN_DK:]
    s_old = s_ref[...]
    sb = s_old.astype(BF16)
    v_new = u - _bdot(w.astype(BF16), sb)
    vb = v_new.astype(BF16)
    o = _bdot((q * eg).astype(BF16), sb) + _bdot((qk * decay).astype(BF16), vb)
    g_last = gch[:, csz - 1:csz, :]
    k2 = (k * jnp.exp(g_last - gch)).astype(BF16)
    on = _rms_scale(o) * gain
    for n, (bb, h) in enumerate(heads):
        s_ref[n] = s_old[n] * jnp.exp(g_last[n]) + jnp.dot(k2[n].T, vb[n], preferred_element_type=F32)
        zz = z_ref[bb, :, h * GDN_DK:(h + 1) * GDN_DK]
        o_ref[bb, :, h * GDN_DK:(h + 1) * GDN_DK] = (on[n] * _silu(zz)).astype(BF16)

    xbuf[:, 0:8, :] = xbuf[:, csz:csz + 8, :]

    @pl.when(c == pl.num_programs(1) - 1)
    def _():
        sout_ref[...] = s_ref[...].reshape(nb, GDN_HEADS, GDN_DK, GDN_DK)


GDN_BATCHES_PER_STEP = 4


def gdn(qkv, z, ba, conv_w, a_log, dt_bias, out_gain, conv0, s0, bsz, tb, csz, first_block):
    nblk = tb // csz
    nb = GDN_BATCHES_PER_STEP
    qk_w = 3 * GDN_HEADS * GDN_DK
    o_w = GDN_HEADS * GDN_DK

    def rows(b, c):
        return (b, (c + first_block) % nblk, 0)

    def lane_pad(vec):
        return jnp.zeros((1, 128), F32).at[0, 8:16].set(vec.astype(F32))

    conv0p = jnp.concatenate([jnp.zeros((bsz, 8 - (CONV_W - 1), qk_w), F32), conv0.astype(F32)], axis=1)
    state = pl.BlockSpec((nb, GDN_HEADS, GDN_DK, GDN_DK), lambda b, c: (b, 0, 0, 0))
    o, s_new = pl.pallas_call(
        functools.partial(_gdn_kernel, csz=csz, nb=nb),
        grid=(bsz // nb, nblk),
        in_specs=[pl.BlockSpec((nb, csz, qk_w), rows), pl.BlockSpec((nb, csz, o_w), rows),
                  pl.BlockSpec((nb, csz, 128), rows), _const_spec((CONV_W, qk_w)), _const_spec((1, 128)),
                  _const_spec((1, 128)), _const_spec((1, GDN_DK)),
                  pl.BlockSpec((nb, 8, qk_w), lambda b, c: (b, 0, 0)), state],
        out_specs=[pl.BlockSpec((nb, csz, o_w), rows), state],
        out_shape=[jax.ShapeDtypeStruct((bsz, tb, o_w), BF16),
                   jax.ShapeDtypeStruct((bsz, GDN_HEADS, GDN_DK, GDN_DK), F32)],
        scratch_shapes=[pltpu.VMEM((nb, 8 + csz, qk_w), F32), pltpu.VMEM((nb * GDN_HEADS, GDN_DK, GDN_DK), F32)],
        compiler_params=_cparams(("parallel", "arbitrary")),
        name="gdn",
    )(qkv.reshape(bsz, tb, qk_w), z.reshape(bsz, tb, o_w), ba.reshape(bsz, tb, 128), conv_w.astype(F32),
      lane_pad(a_log), lane_pad(dt_bias), out_gain.reshape(1, GDN_DK).astype(F32), conv0p, s0.astype(F32))
    return o.reshape(bsz * tb, o_w), s_new


S5_CHUNKS = 4
S5_CH_U = SSM_WIDTH // S5_CHUNKS
S5_CH_P = S5_CH_U // SSM_GROUP * SSM_STATE


def _s5_kernel(u_ref, wb_ref, wc_ref, a_ref, d_ref, gw_ref, gb_ref, st0_ref, o_ref, stout_ref, xs, st_ref,
               *, tt, bsz):
    c = pl.program_id(0)

    @pl.when(c == 0)
    def _():
        st_ref[...] = st0_ref[...]

    rows = tt * bsz
    u = u_ref[...].reshape(rows, SSM_WIDTH)
    ub = u.astype(BF16)
    for j in range(S5_CHUNKS):
        xs[j] = jnp.dot(ub[:, j * S5_CH_U:(j + 1) * S5_CH_U], wb_ref[j], preferred_element_type=F32)

    for j in range(S5_CHUNKS):
        ar = jnp.broadcast_to(a_ref[j, 0:1, :], (bsz, S5_CH_P))
        ai = jnp.broadcast_to(a_ref[j, 1:2, :], (bsz, S5_CH_P))

        def step(t, carry, j=j, ar=ar, ai=ai):
            re, im = carry
            row = pl.multiple_of(t * bsz, bsz)
            nre = ar * re - ai * im + xs[j, pl.ds(row, bsz), 0:S5_CH_P]
            nim = ar * im + ai * re + xs[j, pl.ds(row, bsz), S5_CH_P:2 * S5_CH_P]
            xs[j, pl.ds(row, bsz), 0:S5_CH_P] = nre
            xs[j, pl.ds(row, bsz), S5_CH_P:2 * S5_CH_P] = nim
            return nre, nim

        re, im = lax.fori_loop(0, tt, step, (st_ref[j, :, 0:S5_CH_P], st_ref[j, :, S5_CH_P:2 * S5_CH_P]))
        st_ref[j, :, 0:S5_CH_P] = re
        st_ref[j, :, S5_CH_P:2 * S5_CH_P] = im

    ys = [jnp.dot(xs[j].astype(BF16), wc_ref[j], preferred_element_type=F32) for j in range(S5_CHUNKS)]
    y = jnp.concatenate(ys, axis=-1) + u * d_ref[...]
    hg = 0.5 * y * (1.0 + jnp.tanh(math.sqrt(2.0 / math.pi) * (y + 0.044715 * (y * y * y))))
    gate = _sigmoid(jnp.dot(hg.astype(BF16), gw_ref[...], preferred_element_type=F32) + gb_ref[...])
    o_ref[...] = (hg * gate).reshape(tt, bsz, SSM_WIDTH)

    @pl.when(c == pl.num_programs(0) - 1)
    def _():
        stout_ref[...] = st_ref[...]


def s5(u_tb, re0, im0, lam_re, lam_im, log_dt, b_re, b_im, c_re, c_im, d_skip, glu_w, glu_b,
       bsz, tb, tt, first_tile):
    ntile = tb // tt
    lr = jnp.minimum(lam_re.astype(F32), -1e-4)
    li = lam_im.astype(F32)
    dt = jnp.exp(log_dt.astype(F32))[:, None]
    mag = jnp.exp(lr * dt)
    ar, ai = mag * jnp.cos(li * dt), mag * jnp.sin(li * dt)
    den = lr * lr + li * li
    nr, ni = ar - 1.0, ai
    cr, ci = (nr * lr + ni * li) / den, (ni * lr - nr * li) / den
    br, bi = b_re.astype(F32), b_im.astype(F32)
    bbr = cr[..., None] * br - ci[..., None] * bi
    bbi = cr[..., None] * bi + ci[..., None] * br
    gpc = S5_CH_U // SSM_GROUP

    def block_diag_in(bb):
        bb = bb.reshape(S5_CHUNKS, gpc, SSM_STATE, SSM_GROUP)
        eye = jnp.eye(gpc, dtype=F32)
        return jnp.einsum("jgpm,gh->jgmhp", bb, eye).reshape(S5_CHUNKS, S5_CH_U, S5_CH_P)

    def block_diag_out(cm):
        cm = cm.reshape(S5_CHUNKS, gpc, SSM_GROUP, SSM_STATE)
        eye = jnp.eye(gpc, dtype=F32)
        return jnp.einsum("jgmp,gh->jgphm", cm, eye).reshape(S5_CHUNKS, S5_CH_P, S5_CH_U)

    wb = jnp.concatenate([block_diag_in(bbr), block_diag_in(bbi)], axis=-1).astype(BF16)
    wc = jnp.concatenate([block_diag_out(c_re.astype(F32)), -block_diag_out(c_im.astype(F32))],
                         axis=1).astype(BF16)
    a_pack = jnp.stack([ar.reshape(S5_CHUNKS, S5_CH_P), ai.reshape(S5_CHUNKS, S5_CH_P)], axis=1)
    st0 = jnp.concatenate([re0.astype(F32).reshape(bsz, S5_CHUNKS, S5_CH_P),
                           im0.astype(F32).reshape(bsz, S5_CHUNKS, S5_CH_P)], axis=-1).transpose(1, 0, 2)

    def tile(c):
        return ((c + first_tile) % ntile, 0, 0)

    out, st = pl.pallas_call(
        functools.partial(_s5_kernel, tt=tt, bsz=bsz),
        grid=(ntile,),
        in_specs=[pl.BlockSpec((tt, bsz, SSM_WIDTH), tile), _const_spec(wb.shape), _const_spec(wc.shape),
                  _const_spec(a_pack.shape), _const_spec((1, SSM_WIDTH)), _const_spec((SSM_WIDTH, SSM_WIDTH)),
                  _const_spec((1, SSM_WIDTH)), _const_spec(st0.shape)],
        out_specs=[pl.BlockSpec((tt, bsz, SSM_WIDTH), tile), _const_spec(st0.shape)],
        out_shape=[jax.ShapeDtypeStruct((tb, bsz, SSM_WIDTH), F32), jax.ShapeDtypeStruct(st0.shape, F32)],
        scratch_shapes=[pltpu.VMEM((S5_CHUNKS, tt * bsz, 2 * S5_CH_P), F32),
                        pltpu.VMEM((S5_CHUNKS, bsz, 2 * S5_CH_P), F32)],
        compiler_params=_cparams(("arbitrary",)),
        name="s5",
    )(u_tb, wb, wc, a_pack, d_skip.reshape(1, SSM_WIDTH).astype(F32), glu_w.astype(BF16),
      glu_b.reshape(1, SSM_WIDTH).astype(F32), st0)
    st = st.transpose(1, 0, 2)
    re_new = st[..., :S5_CH_P].reshape(bsz, SSM_GROUPS, SSM_STATE)
    im_new = st[..., S5_CH_P:].reshape(bsz, SSM_GROUPS, SSM_STATE)
    return out, re_new, im_new


EVEN_OUTS = ((3 * GDN_HEADS * GDN_DK, "flat"), (GDN_HEADS * GDN_DK, "flat"), (128, "flat"), (SSM_WIDTH, "tmajor"))


def pack_even_w_in(w_in):
    qkv_w = 3 * GDN_HEADS * GDN_DK
    z_w = GDN_HEADS * GDN_DK
    c0 = qkv_w + z_w
    ba = jnp.zeros((D_MODEL, 128), w_in.dtype).at[:, :2 * GDN_HEADS].set(w_in[:, c0:c0 + 2 * GDN_HEADS])
    return jnp.concatenate([w_in[:, :c0], ba, w_in[:, c0 + 2 * GDN_HEADS:]], axis=1).astype(BF16)


def even_mixer(x, gain, p, conv0, s0, re0, im0, bsz, tb, tm, csz, first_block, tt):
    qkv, z, ba, u_t = inproj(x, gain, p["w_in"], EVEN_OUTS, bsz, tb, tm)
    o_gdn, s_new = gdn(qkv, z, ba, p["conv_w"], p["a_log"], p["dt_bias"], p["out_gain"], conv0, s0,
                       bsz, tb, csz, first_block)
    o_s5, re_new, im_new = s5(u_t.reshape(tb, bsz, SSM_WIDTH), re0, im0, p["lam_re"], p["lam_im"], p["log_dt"],
                              p["b_re"], p["b_im"], p["c_re"], p["c_im"], p["d"], p["glu_w"], p["glu_b"],
                              bsz, tb, tt, first_block * csz // tt)
    mixed = [(o_gdn, GDN_HEADS * GDN_DK, "flat"), (o_s5.reshape(tb, bsz * SSM_WIDTH), SSM_WIDTH, "tmajor")]
    return mixed, qkv, s_new, re_new, im_new


def _gate_kernel(f_ref, bias_ref, c0_ref, logf_ref, cum_ref, tot_ref, *, segments, log_sigmoid):
    carry = c0_ref[...]
    for start, size, pad_rows in segments:
        x = f_ref[start:start + size, :]
        if log_sigmoid:
            x = x + bias_ref[...]
            x = jnp.minimum(x, 0.0) - jnp.log(1.0 + jnp.exp(-jnp.abs(x)))
        if pad_rows:
            x = jnp.where(lax.broadcasted_iota(jnp.int32, x.shape, 0) >= pad_rows, x, 0.0)
        r = lax.broadcasted_iota(jnp.int32, (size, size), 0)
        cc = lax.broadcasted_iota(jnp.int32, (size, size), 1)
        cum = _dot_hi((r >= cc).astype(F32), x) + carry
        logf_ref[start:start + size, :] = x
        cum_ref[start:start + size, :] = cum * LOG2E
        carry = cum[size - 1:size, :]
    tot_ref[...] = carry


def gate_cumsum(f, bias, carry0, bsz, tb, segments, log_sigmoid):
    seq = pl.BlockSpec((None, tb, 128), lambda b: (b, 0, 0))
    one = pl.BlockSpec((None, 1, 128), lambda b: (b, 0, 0))
    return pl.pallas_call(
        functools.partial(_gate_kernel, segments=tuple(segments), log_sigmoid=log_sigmoid),
        grid=(bsz,),
        in_specs=[seq, _const_spec((1, 128)), one],
        out_specs=[seq, seq, one],
        out_shape=[jax.ShapeDtypeStruct((bsz, tb, 128), F32), jax.ShapeDtypeStruct((bsz, tb, 128), F32),
                   jax.ShapeDtypeStruct((bsz, 1, 128), F32)],
        compiler_params=_cparams(("parallel",)),
        name="gate_cumsum",
    )(f, bias, carry0)


N_BIAS_TERMS = 3


def _head_padded(x, scale, other):
    lane = lax.broadcasted_iota(jnp.int32, (x.shape[0], 128), 1)
    parts = []
    for h in range(x.shape[1] // HEAD_DIM):
        g = x[:, (h // 2) * 128:(h // 2 + 1) * 128] * scale
        own = (lane >= HEAD_DIM) == bool(h % 2)
        parts.append(jnp.where(own, g, other(h, lane)).astype(BF16))
    return jnp.concatenate(parts, axis=-1)


def _rotary(x, c_ref, s1_ref, s2_ref):
    width = x.shape[-1]
    return (x * c_ref[...] + pltpu.roll(x, width - ROT_DIM // 2, 1) * s1_ref[...]
            + pltpu.roll(x, ROT_DIM // 2, 1) * s2_ref[...])


def _prep_q_kernel(qf_ref, qd_ref, c_ref, s1_ref, s2_ref, qfa_ref, qda_ref):
    qscale = HEAD_DIM ** -0.5 * LOG2E
    ones = lambda h, lane: jnp.where(lane % HEAD_DIM < N_BIAS_TERMS, 1.0, 0.0)
    qfa_ref[...] = _head_padded(qf_ref[...], qscale, ones)
    qda_ref[...] = _head_padded(_rotary(qd_ref[...], c_ref, s1_ref, s2_ref), qscale, lambda h, lane: 0.0)


def _prep_kv_kernel(*refs, rotate):
    if rotate:
        kf_ref, vf_ref, kd_ref, vd_ref, cum_ref, c_ref, s1_ref, s2_ref, kfa_ref, vfa_ref, kdb_ref, vda_ref, kdr_ref = refs
        kd = _rotary(kd_ref[...], c_ref, s1_ref, s2_ref)
        kdr_ref[...] = kd
    else:
        kf_ref, vf_ref, kd_ref, vd_ref, cum_ref, kfa_ref, vfa_ref, kdb_ref, vda_ref = refs
        kd = kd_ref[...]
    cum = cum_ref[...]

    def offset_pieces(h, lane):
        c = jnp.broadcast_to(cum[:, h:h + 1], lane.shape)
        hi = c.astype(BF16).astype(F32)
        mid = (c - hi).astype(BF16).astype(F32)
        lo = c - hi - mid
        sel = lane % HEAD_DIM
        return jnp.where(sel == 0, -hi, jnp.where(sel == 1, -mid, jnp.where(sel == 2, -lo, 0.0)))

    kfa_ref[...] = _head_padded(kf_ref[...], 1.0, offset_pieces)
    vfa_ref[...] = _head_padded(vf_ref[...], 1.0, lambda h, lane: 1.0)
    kdb_ref[...] = kd.astype(BF16)
    vd = vd_ref[...].astype(BF16)
    ones = jnp.ones((vd.shape[0], DIFF_V), BF16)
    vda_ref[...] = jnp.concatenate(
        [a for hd in range(DIFF_HEADS) for a in (vd[:, hd * DIFF_V:(hd + 1) * DIFF_V], ones)], axis=-1)


def attn_prep_q(qf, qd, tables, bsz, tb, tm):
    width = qf.shape[-1]
    xs = pl.BlockSpec((None, tm, width), lambda i, b: (b, i, 0))
    qs = pl.BlockSpec((None, tm, 2 * width), lambda i, b: (b, i, 0))
    ts = pl.BlockSpec((tm, width), lambda i, b: (i, 0))
    shp = jax.ShapeDtypeStruct((bsz, tb, 2 * width), BF16)
    return pl.pallas_call(
        _prep_q_kernel,
        grid=(tb // tm, bsz),
        in_specs=[xs, xs, ts, ts, ts],
        out_specs=[qs, qs],
        out_shape=[shp, shp],
        compiler_params=_cparams(("parallel", "parallel")),
        name="attn_prep_q",
    )(qf, qd, *tables)


def attn_prep_kv(kf, vf, kd, vd, cum2, tables, bsz, tb, tm):
    width = kf.shape[-1]
    rotate = tables is not None
    xs = pl.BlockSpec((None, tm, width), lambda i, b: (b, i, 0))
    ws = pl.BlockSpec((None, tm, 2 * width), lambda i, b: (b, i, 0))
    cs = pl.BlockSpec((None, tm, 128), lambda i, b: (b, i, 0))
    ts = pl.BlockSpec((tm, width), lambda i, b: (i, 0))
    shp = lambda w, dt: jax.ShapeDtypeStruct((bsz, tb, w), dt)
    outs = pl.pallas_call(
        functools.partial(_prep_kv_kernel, rotate=rotate),
        grid=(tb // tm, bsz),
        in_specs=[xs] * 4 + [cs] + ([ts] * 3 if rotate else []),
        out_specs=[ws, ws, xs, ws] + ([xs] if rotate else []),
        out_shape=[shp(2 * width, BF16), shp(2 * width, BF16), shp(width, BF16), shp(2 * width, BF16)]
        + ([shp(width, F32)] if rotate else []),
        compiler_params=_cparams(("parallel", "parallel")),
        name="attn_prep_kv",
    )(kf, vf, kd, vd, cum2, *(tables if rotate else ()))
    return outs if rotate else list(outs) + [None]


def rope_tables(pos, nheads):
    half = ROT_DIM // 2
    inv_freq = ROPE_THETA ** (-jnp.arange(0, ROT_DIM, 2, dtype=F32) / ROT_DIM)
    ang = pos.astype(F32)[:, None] * inv_freq[None, :]
    cos, sin = jnp.cos(ang), jnp.sin(ang)
    t = pos.shape[0]
    rest = jnp.zeros((t, HEAD_DIM - ROT_DIM), F32)
    z = jnp.zeros((t, half), F32)
    ctab = jnp.concatenate([cos, cos, rest + 1.0], axis=1)
    s1 = jnp.concatenate([-sin, z, rest], axis=1)
    s2 = jnp.concatenate([z, sin, rest], axis=1)
    return tuple(jnp.tile(a, (1, nheads)) for a in (ctab, s1, s2))


NEG_INF = -1e30
N_MAPS = FOX_HEADS + 2 * DIFF_HEADS


def _attn_kernel(*refs, tq, tk, te, has_main, main_causal, fox_extra, diff_extra, zero_pad_q, pad_rows,
                 out_scale):
    if has_main:
        (qf_ref, qd_ref, cq_ref, kf_ref, vf_ref, kd_ref, vd_ref,
         kfe_ref, vfe_ref, kde_ref, vde_ref, lam_ref, gain_ref,
         o_ref, m_ref, cqb_ref, accf_ref, accd_ref) = refs
    else:
        (qf_ref, qd_ref, cq_ref, kfe_ref, vfe_ref, kde_ref, vde_ref, lam_ref, gain_ref,
         o_ref, m_ref, cqb_ref, accf_ref, accd_ref) = refs
    i = pl.program_id(1)
    j = pl.program_id(2)
    neg = NEG_INF * LOG2E

    def lanes(x, ncols):
        reps, rem = divmod(ncols, 128)
        return jnp.concatenate([x] * reps + ([x[:, :rem]] if rem else []), axis=-1)

    def one_map(idx, q, k, v, acc_ref, slot, cq, mask, qr):
        t = _dot_nt(q, k)
        nrows, ncols = t.shape
        if mask is not None:
            t = jnp.where(mask, t, neg)
        mt = jnp.broadcast_to(jnp.max(t, axis=-1, keepdims=True), (nrows, 128))
        m_prev = m_ref[idx, qr]
        if cq is None:
            m_new = jnp.maximum(m_prev, mt)
            shift = m_new
        else:
            m_new = jnp.maximum(m_prev, mt + cq)
            shift = m_new - cq
        p = jnp.exp2(t - lanes(shift, ncols))
        alpha = jnp.exp2(m_prev - m_new)
        m_ref[idx, qr] = m_new
        acc_ref[slot, qr] = (lanes(alpha, acc_ref.shape[-1]) * acc_ref[slot, qr]
                             + jnp.dot(p.astype(BF16), v, preferred_element_type=F32))

    def process(k_tiles, fox_mask, diff_mask, qr=slice(None)):
        def rows(which, lo, hi):
            parts = [tile[which][:n, lo:hi] for tile, n in k_tiles]
            return parts[0] if len(parts) == 1 else jnp.concatenate(parts, axis=0)

        for h in range(FOX_HEADS):
            lo, hi = h * 128, (h + 1) * 128
            one_map(h, qf_ref[qr, lo:hi], rows(0, lo, hi), rows(1, lo, hi), accf_ref, h, cqb_ref[h, qr], fox_mask, qr)
        for hm in range(2 * DIFF_HEADS):
            g = hm // 2
            one_map(FOX_HEADS + hm, qd_ref[qr, hm * 128:(hm + 1) * 128], rows(2, g * 128, (g + 1) * 128),
                    rows(3, g * 2 * DIFF_V, (g + 1) * 2 * DIFF_V), accd_ref, hm, None, diff_mask, qr)

    def visible(extra_kind, main_kind, width, r0=0, nrows=tq):
        if extra_kind == "none" and main_kind == "none":
            return None
        r = lax.broadcasted_iota(jnp.int32, (nrows, width), 0) + r0
        c = lax.broadcasted_iota(jnp.int32, (nrows, width), 1)
        cm = c - te
        rules = {"valid": c >= pad_rows, "causal": c <= r,
                 "causal_valid": jnp.logical_and(c <= r, c >= pad_rows),
                 "main_causal": cm <= r,
                 "main_block_causal": cm < (lax.shift_right_logical(r, 6) + 1) * BLOCK}
        in_extra = c < te
        if main_kind == "none":
            return jnp.logical_or(c >= te, rules[extra_kind])
        if extra_kind == "none":
            return jnp.logical_or(in_extra, rules[main_kind])
        return jnp.logical_or(jnp.logical_and(in_extra, rules[extra_kind]),
                              jnp.logical_and(c >= te, rules[main_kind]))

    @pl.when(j == 0)
    def _():
        m_ref[...] = jnp.full(m_ref.shape, NEG_INF, F32)
        accf_ref[...] = jnp.zeros(accf_ref.shape, F32)
        accd_ref[...] = jnp.zeros(accd_ref.shape, F32)
        for h in range(FOX_HEADS):
            cqb_ref[h] = jnp.broadcast_to(cq_ref[:, h:h + 1], (tq, 128))

    extra_tile = (kfe_ref, vfe_ref, kde_ref, vde_ref)
    if has_main:
        main_tile = (kf_ref, vf_ref, kd_ref, vd_ref)
        if main_causal:
            assert tq == tk
            last, before_last = j == i, j < i
            fox_main, diff_main = "main_causal", "main_block_causal"
        else:
            last, before_last = j == pl.num_programs(2) - 1, j < pl.num_programs(2) - 1
            fox_main = diff_main = "none"

        @pl.when(before_last)
        def _():
            process([(main_tile, tk)], None, None)

        @pl.when(last)
        def _():
            process([(extra_tile, te), (main_tile, tk)], visible(fox_extra, fox_main, te + tk),
                    visible(diff_extra, diff_main, te + tk))
    else:
        last = j == 0
        process([(extra_tile, te)], visible(fox_extra, "none", te), visible(diff_extra, "none", te))

    @pl.when(last)
    def _():
        lam = lam_ref[...]
        upper = lax.broadcasted_iota(jnp.int32, (tq, 128), 1) >= HEAD_DIM
        outs = []
        for g in range(FOX_HEADS // 2):
            a0, a1 = accf_ref[2 * g], accf_ref[2 * g + 1]
            outs.append(jnp.where(upper, a1 / pltpu.roll(a1, HEAD_DIM, 1), a0 / pltpu.roll(a0, HEAD_DIM, 1)))
        for hd in range(DIFF_HEADS):
            a1, a2 = accd_ref[2 * hd], accd_ref[2 * hd + 1]
            o = a1[:, :DIFF_V] / a1[:, DIFF_V:] - lam * (a2[:, :DIFF_V] / a2[:, DIFF_V:])
            outs.append(_rms_scale(o) * gain_ref[...] * out_scale)
        out = jnp.concatenate(outs, axis=-1)
        if zero_pad_q:
            out = jnp.where(lax.broadcasted_iota(jnp.int32, out.shape, 0) >= pad_rows, out, 0.0)
        o_ref[...] = out.astype(BF16)


def attention(q_arrays, q_rows, main, extra, lam, gain, out_scale, bsz, tq, nq, tk, te,
              main_causal, fox_extra, diff_extra, zero_pad_q, q_block0=0, pad_rows=BLOCK - N_META):
    qf, qd, cumq = q_arrays
    has_main = main is not None
    nk = main[0].shape[1] // tk if has_main else 1
    kv_widths = (FOX_HEADS * 128, FOX_HEADS * 128, 2 * DIFF_HEADS * HEAD_DIM, DIFF_HEADS * 2 * DIFF_V)

    def qspec(width):
        return pl.BlockSpec((None, tq, width), lambda b, i, j: (b, i + q_block0, 0))

    in_specs = [qspec(FOX_HEADS * 128), qspec(2 * DIFF_HEADS * 128), qspec(128)]
    args = [qf, qd, cumq]
    if has_main:
        if main_causal:
            kmap = lambda b, i, j: (b, jnp.minimum(j, i), 0)
        else:
            kmap = lambda b, i, j: (b, j, 0)
        in_specs += [pl.BlockSpec((None, tk, w), kmap) for w in kv_widths]
        args += list(main)
    eblk = extra[4]
    emap = lambda b, i, j: (b, eblk, 0)
    in_specs += [pl.BlockSpec((None, te, w), emap) for w in kv_widths] + [_const_spec((1, 1)), _const_spec((1, DIFF_V))]
    args += list(extra[:4]) + [lam.reshape(1, 1).astype(F32), gain.reshape(1, DIFF_V).astype(F32)]
    width = FOX_HEADS * HEAD_DIM + DIFF_HEADS * DIFF_V
    return pl.pallas_call(
        functools.partial(_attn_kernel, tq=tq, tk=tk, te=te, has_main=has_main, main_causal=main_causal,
                          fox_extra=fox_extra, diff_extra=diff_extra, zero_pad_q=zero_pad_q, pad_rows=pad_rows,
                          out_scale=out_scale),
        grid=(bsz, nq, nk),
        in_specs=in_specs,
        out_specs=pl.BlockSpec((None, tq, width), lambda b, i, j: (b, i, 0)),
        out_shape=jax.ShapeDtypeStruct((bsz, nq * tq, width), BF16),
        scratch_shapes=[pltpu.VMEM((N_MAPS, tq, 128), F32), pltpu.VMEM((FOX_HEADS, tq, 128), F32),
                        pltpu.VMEM((FOX_HEADS, tq, 128), F32), pltpu.VMEM((2 * DIFF_HEADS, tq, 2 * DIFF_V), F32)],
        compiler_params=_cparams(("parallel", "parallel", "arbitrary")),
        name="attention",
    )(*args)


ODD_OUTS = ((512, "flat"),) * 6 + ((128, "flat"),)


def pack_odd_w_in(w_in):
    c0 = 3 * FOX_HEADS * HEAD_DIM
    f = jnp.zeros((D_MODEL, 128), w_in.dtype).at[:, :FOX_HEADS].set(w_in[:, c0:c0 + FOX_HEADS])
    return jnp.concatenate([w_in[:, :c0], w_in[:, c0 + FOX_HEADS:], f], axis=1).astype(BF16)


def diff_lambda_value(diff_lambda, lam_init):
    lq1, lk1, lq2, lk2 = diff_lambda.astype(F32)
    return jnp.exp(jnp.sum(lq1 * lk1)) - jnp.exp(jnp.sum(lq2 * lk2)) + lam_init


def _logical_copy_kernel(*refs, n, seq, tb):
    srcs, dsts, sem = refs[:n], refs[n:2 * n], refs[2 * n]
    copies = []
    for a in range(n):
        copies.append(pltpu.make_async_copy(srcs[a].at[:, pl.ds(0, seq)], dsts[a].at[:, pl.ds(N_META, seq)],
                                            sem.at[a, 0]))
        copies.append(pltpu.make_async_copy(srcs[a].at[:, pl.ds(tb - N_META, N_META)],
                                            dsts[a].at[:, pl.ds(0, N_META)], sem.at[a, 1]))
    for cp in copies:
        cp.start()
    for cp in copies:
        cp.wait()


def slabs_to_logical(arrays, seq):
    n = len(arrays)
    bsz, tb, _ = arrays[0].shape
    return pl.pallas_call(
        functools.partial(_logical_copy_kernel, n=n, seq=seq, tb=tb),
        in_specs=[pl.BlockSpec(memory_space=pl.ANY)] * n,
        out_specs=[pl.BlockSpec(memory_space=pl.ANY)] * n,
        out_shape=[jax.ShapeDtypeStruct((bsz, N_META + seq, a.shape[-1]), a.dtype) for a in arrays],
        scratch_shapes=[pltpu.SemaphoreType.DMA((n, 2))],
        name="slabs_to_logical",
    )(*arrays)


def odd_mixer_prompt(x, gain, p, lam_init, bsz, tb, nframes, tm, tq):
    qf, kf, vf, qd, kd, vd, f = [a.reshape(bsz, tb, -1) for a in inproj(x, gain, p["w_in"], ODD_OUTS, bsz, tb, tm)]
    meta_blk = tb // BLOCK - 1
    pad = BLOCK - N_META
    segments = [(nframes, BLOCK, pad)] + [(s, tq, 0) for s in range(0, nframes, tq)]
    logf, cum, _ = gate_cumsum(f, p["f_bias"], jnp.zeros((bsz, 1, 128), F32), bsz, tb, segments, True)
    pos = jnp.concatenate([N_META + jnp.arange(nframes, dtype=jnp.int32), jnp.zeros((pad,), jnp.int32),
                           jnp.arange(N_META, dtype=jnp.int32)])
    tables = rope_tables(pos, 2 * DIFF_HEADS)
    qfa, qda = attn_prep_q(qf, qd, tables, bsz, tb, tm)
    kfa, vfa, kdb, vda, kd = attn_prep_kv(kf, vf, kd, vd, cum, tables, bsz, tb, tm)
    nk = nframes // tq
    lam = diff_lambda_value(p["diff_lambda"], lam_init)
    extra = (kfa, vfa, kdb, vda, meta_blk)
    common = dict(lam=lam, gain=p["diff_gain"], out_scale=1.0 - lam_init, bsz=bsz, te=BLOCK, diff_extra="valid")
    frames = attention((qfa, qda, cum), tb, (kfa, vfa, kdb, vda), extra, tq=tq, nq=nk, tk=tq,
                       main_causal=True, fox_extra="valid", zero_pad_q=False, **common)
    meta = attention((qfa, qda, cum), tb, None, extra, tq=BLOCK, nq=1, tk=BLOCK, main_causal=False,
                     fox_extra="causal_valid", zero_pad_q=True, q_block0=meta_blk, **common)
    mixed = jnp.concatenate([frames, meta], axis=1).reshape(bsz * tb, -1)
    return mixed, kf, vf, logf, kd, vd


def odd_mixer_sample(x, gain, p, lam_init, kf_past, vf_past, lf_past, kd_past, vd_past, bsz, tb, tk):
    qf, kf, vf, qd, kd, vd, f = [a.reshape(bsz, tb, -1) for a in inproj(x, gain, p["w_in"], ODD_OUTS, bsz, tb, tb)]
    past = kf_past.shape[1]
    nk = past // tk
    lf128 = jnp.zeros((bsz, past, 128), F32).at[:, :, :FOX_HEADS].set(lf_past.astype(F32))
    zero_c = jnp.zeros((bsz, 1, 128), F32)
    _, cum_past, total = gate_cumsum(lf128, p["f_bias"], zero_c, bsz, past, [(s, tk, 0) for s in range(0, past, tk)],
                                     False)
    logf, cum, _ = gate_cumsum(f, p["f_bias"], total, bsz, tb, [(0, tb, 0)], True)
    tables = rope_tables(past + jnp.arange(tb, dtype=jnp.int32), 2 * DIFF_HEADS)
    qfa, qda = attn_prep_q(qf, qd, tables, bsz, tb, tb)
    kfa, vfa, kdb, vda, kd = attn_prep_kv(kf, vf, kd, vd, cum, tables, bsz, tb, tb)
    flat = lambda a: a.astype(F32).reshape(bsz, past, -1)
    main = tuple(attn_prep_kv(flat(kf_past), flat(vf_past), flat(kd_past), flat(vd_past), cum_past, None,
                              bsz, past, tk)[:4])
    extra = (kfa, vfa, kdb, vda, 0)
    mixed = attention((qfa, qda, cum), tb, main, extra, lam=diff_lambda_value(p["diff_lambda"], lam_init),
                      gain=p["diff_gain"], out_scale=1.0 - lam_init, bsz=bsz, tq=tb, nq=1, tk=tk, te=tb,
                      main_causal=False, fox_extra="causal", diff_extra="none", zero_pad_q=False)
    return mixed.reshape(bsz * tb, -1), kf, vf, logf, kd, vd


def kernel(x_prompt, x_sample, state_conv, state_delta, state_ssm_re, state_ssm_im, cache_fox_k, cache_fox_v,
           cache_fox_logf, cache_diff_k, cache_diff_v, meta_tokens, norm_mix, norm_ffn, norm_final, w_in_even,
           w_out_even, conv_w, gdn_a_log, gdn_dt_bias, gdn_out_norm, ssm_lambda_re, ssm_lambda_im, ssm_log_dt,
           ssm_b_re, ssm_b_im, ssm_c_re, ssm_c_im, ssm_d, ssm_glu_w, ssm_glu_b, w_in_odd, w_out_odd, fox_f_bias,
           diff_lambda, diff_out_norm, ffn_w1, ffn_w3, ffn_w2):
    bp, seq, _ = x_prompt.shape
    bs, ls, _ = x_sample.shape
    depth = norm_mix.shape[0]
    tb = seq + BLOCK
    pad = BLOCK - N_META
    tm_p = 832
    tq = 512
    meta = jnp.broadcast_to(meta_tokens.astype(F32)[None], (bp, N_META, D_MODEL))
    xp = jnp.concatenate([x_prompt.astype(F32), jnp.zeros((bp, pad, D_MODEL), F32), meta], axis=1)
    xp = xp.reshape(bp * tb, D_MODEL)
    xs = x_sample.astype(F32).reshape(bs * ls, D_MODEL)

    def logical(a):
        return jnp.concatenate([a[:, tb - N_META:], a[:, :seq]], axis=1)

    outs = {k: [] for k in ("conv_p", "conv_s", "delta_p", "delta_s", "re_p", "re_s", "im_p", "im_s", "fk_p", "fk_s",
                            "fv_p", "fv_s", "fl_p", "fl_s", "dk_p", "dk_s", "dv_p", "dv_s")}
    nchunk = FFN_HIDDEN // FFN_CHUNK
    for l in range(depth):
        i = l // 2
        if l % 2 == 0:
            pe = dict(w_in=pack_even_w_in(w_in_even[i]), conv_w=conv_w[i], a_log=gdn_a_log[i], dt_bias=gdn_dt_bias[i],
                      out_gain=gdn_out_norm[i], lam_re=ssm_lambda_re[i], lam_im=ssm_lambda_im[i],
                      log_dt=ssm_log_dt[i], b_re=ssm_b_re[i], b_im=ssm_b_im[i], c_re=ssm_c_re[i], c_im=ssm_c_im[i],
                      d=ssm_d[i], glu_w=ssm_glu_w[i], glu_b=ssm_glu_b[i])
            qkv_w = 3 * GDN_HEADS * GDN_DK
            mixed_p, qkv_p, sd_p, re_p, im_p = even_mixer(
                xp, norm_mix[l], pe, jnp.zeros((bp, CONV_W - 1, qkv_w), F32),
                jnp.zeros((bp, GDN_HEADS, GDN_DK, GDN_DK), F32), jnp.zeros((bp, SSM_GROUPS, SSM_STATE), F32),
                jnp.zeros((bp, SSM_GROUPS, SSM_STATE), F32), bp, tb, tm_p, BLOCK, tb // BLOCK - 1, 32)
            mixed_s, qkv_s, sd_s, re_s, im_s = even_mixer(
                xs, norm_mix[l], pe, state_conv[i], state_delta[i], state_ssm_re[i], state_ssm_im[i],
                bs, ls, ls, ls, 0, ls)
            outs["conv_p"].append(qkv_p.reshape(bp, tb, qkv_w)[:, seq - (CONV_W - 1):seq])
            outs["conv_s"].append(jnp.concatenate([state_conv[i].astype(F32), qkv_s.reshape(bs, ls, qkv_w)],
                                                  axis=1)[:, ls:])
            outs["delta_p"].append(sd_p); outs["delta_s"].append(sd_s)
            outs["re_p"].append(re_p); outs["re_s"].append(re_s)
            outs["im_p"].append(im_p); outs["im_s"].append(im_s)
            w_out = w_out_even[i]
        else:
            lam_init = 0.8 - 0.6 * math.exp(-0.3 * l)
            po = dict(w_in=pack_odd_w_in(w_in_odd[i]), diff_lambda=diff_lambda[i], diff_gain=diff_out_norm[i],
                      f_bias=jnp.zeros((1, 128), F32).at[0, :FOX_HEADS].set(fox_f_bias[i].astype(F32)))
            mixed_p, kf, vf, lf, kd, vd = odd_mixer_prompt(xp, norm_mix[l], po, lam_init, bp, tb, seq, tm_p, tq)
            mixed_p = [(mixed_p, mixed_p.shape[-1], "flat")]
            kf, vf, kd, vd = slabs_to_logical([kf, vf, kd, vd], seq)
            outs["fk_p"].append(kf.reshape(bp, -1, FOX_HEADS, HEAD_DIM))
            outs["fv_p"].append(vf.reshape(bp, -1, FOX_HEADS, HEAD_DIM))
            outs["fl_p"].append(logical(lf)[:, :, :FOX_HEADS])
            outs["dk_p"].append(kd.reshape(bp, -1, 2 * DIFF_HEADS, HEAD_DIM))
            outs["dv_p"].append(vd.reshape(bp, -1, DIFF_HEADS, DIFF_V))
            mixed_s, kf, vf, lf, kd, vd = odd_mixer_sample(
                xs, norm_mix[l], po, lam_init, cache_fox_k[i], cache_fox_v[i], cache_fox_logf[i], cache_diff_k[i],
                cache_diff_v[i], bs, ls, 512)
            mixed_s = [(mixed_s, mixed_s.shape[-1], "flat")]
            outs["fk_s"].append(kf.reshape(bs, ls, FOX_HEADS, HEAD_DIM))
            outs["fv_s"].append(vf.reshape(bs, ls, FOX_HEADS, HEAD_DIM))
            outs["fl_s"].append(lf[:, :, :FOX_HEADS])
            outs["dk_s"].append(kd.reshape(bs, ls, 2 * DIFF_HEADS, HEAD_DIM))
            outs["dv_s"].append(vd.reshape(bs, ls, DIFF_HEADS, DIFF_V))
            w_out = w_out_odd[i]
        w1 = ffn_w1[l].astype(BF16).reshape(D_MODEL, nchunk, FFN_CHUNK).transpose(1, 0, 2)
        w3 = ffn_w3[l].astype(BF16).reshape(D_MODEL, nchunk, FFN_CHUNK).transpose(1, 0, 2)
        w2 = ffn_w2[l].astype(BF16).reshape(nchunk, FFN_CHUNK, D_MODEL)
        xp = outproj_ffn(xp, mixed_p, w_out.astype(BF16), norm_ffn[l], w1, w3, w2, bp, tb, tm_p)
        xs = outproj_ffn(xs, mixed_s, w_out.astype(BF16), norm_ffn[l], w1, w3, w2, bs, ls, ls)

    y_prompt = final_norm(xp, norm_final, bp, tb, seq, 512).reshape(bp, seq, D_MODEL)
    y_sample = final_norm(xs, norm_final, bs, ls, ls, ls).reshape(bs, ls, D_MODEL)
    order = ("conv_p", "conv_s", "delta_p", "delta_s", "re_p", "re_s", "im_p", "im_s", "fk_p", "fk_s", "fv_p", "fv_s",
             "fl_p", "fl_s", "dk_p", "dk_s", "dv_p", "dv_s")
    return (y_prompt, y_sample) + tuple(jnp.stack(outs[k]) for k in order)
```

```python
import functools
import math

import jax
import jax.numpy as jnp
import numpy as np
from jax import lax
from jax.experimental import pallas as pl
from jax.experimental.pallas import tpu as pltpu

F32 = jnp.float32
BF16 = jnp.bfloat16
HIGHEST = lax.Precision.HIGHEST
LOG2E = math.log2(math.e)

D_MODEL = 1024
N_META = 16
RMS_EPS = 1e-6
HEAD_DIM = 64
GDN_HEADS = 8
GDN_DK = 128
CONV_W = 4
SSM_WIDTH = 512
SSM_GROUP = 16
SSM_GROUPS = 32
SSM_STATE = 64
FOX_HEADS = 8
DIFF_HEADS = 4
DIFF_V = 128
ROT_DIM = 16
ROPE_THETA = 500000.0
FFN_HIDDEN = 2816
FFN_CHUNK = 256
BLOCK = 64
VMEM_LIMIT = 56 * 1024 * 1024


def _cparams(sem):
    return pltpu.CompilerParams(dimension_semantics=sem, vmem_limit_bytes=VMEM_LIMIT)


def _const_spec(shape, single_buffer=False):
    nd = len(shape)
    if single_buffer:
        return pl.BlockSpec(shape, lambda *_: (0,) * nd, pipeline_mode=pl.Buffered(1))
    return pl.BlockSpec(shape, lambda *_: (0,) * nd)


def _sigmoid(x):
    return 1.0 / (1.0 + jnp.exp(-x))


def _silu(x):
    return x * _sigmoid(x)


def _rms_scale(x):
    return x * lax.rsqrt(jnp.mean(x * x, axis=-1, keepdims=True) + RMS_EPS)


def _row_spec(kind, tm, width, nt):
    if kind == "flat":
        return pl.BlockSpec((tm, width), lambda b, i: (b * nt + i, 0))
    return pl.BlockSpec((tm, width), lambda b, i: (i, b))


def _row_shape(kind, bsz, tb, width, dtype):
    if kind == "flat":
        return jax.ShapeDtypeStruct((bsz * tb, width), dtype)
    return jax.ShapeDtypeStruct((tb, bsz * width), dtype)


def _inproj_kernel(x_ref, g_ref, w_ref, *o_refs, widths):
    xn = (_rms_scale(x_ref[...]) * g_ref[...]).astype(BF16)
    off = 0
    for o_ref, wd in zip(o_refs, widths):
        for c0 in range(0, wd, 512):
            cw = min(512, wd - c0)
            o_ref[:, c0:c0 + cw] = jnp.dot(xn, w_ref[:, off + c0:off + c0 + cw],
                                           preferred_element_type=F32)
        off += wd


def inproj(x, gain, w, outs, bsz, tb, tm):
    nt = tb // tm
    widths = tuple(wd for wd, _ in outs)
    return pl.pallas_call(
        functools.partial(_inproj_kernel, widths=widths),
        grid=(bsz, nt),
        in_specs=[_row_spec("flat", tm, D_MODEL, nt), _const_spec((1, D_MODEL)), _const_spec(w.shape, True)],
        out_specs=[_row_spec(kind, tm, wd, nt) for wd, kind in outs],
        out_shape=[_row_shape(kind, bsz, tb, wd, F32) for wd, kind in outs],
        compiler_params=_cparams(("parallel", "parallel")),
        name="inproj",
    )(x, gain.reshape(1, D_MODEL), w)


def _outffn_kernel(res_ref, *refs, a_widths, nc):
    na = len(a_widths)
    a_refs = refs[:na]
    wout_ref, g_ref, w1_ref, w3_ref, w2_ref, o_ref, acc_ref, xn_ref = refs[na:]
    x1 = res_ref[...]
    off = 0
    for a_ref, wd in zip(a_refs, a_widths):
        x1 = x1 + jnp.dot(a_ref[...].astype(BF16), wout_ref[off:off + wd, :], preferred_element_type=F32)
        off += wd
    xn_ref[...] = (_rms_scale(x1) * g_ref[...]).astype(BF16)
    acc_ref[...] = x1

    def body(c, carry):
        xn = xn_ref[...]
        h1 = jnp.dot(xn, w1_ref[c], preferred_element_type=F32)
        h3 = jnp.dot(xn, w3_ref[c], preferred_element_type=F32)
        act = (_silu(h1) * h3).astype(BF16)
        acc_ref[...] += jnp.dot(act, w2_ref[c], preferred_element_type=F32)
        return carry

    lax.fori_loop(0, nc, body, 0)
    o_ref[...] = acc_ref[...]


def outproj_ffn(res, mixed, w_out, gain, w1, w3, w2, bsz, tb, tm):
    nt = tb // tm
    nc = w1.shape[0]
    a_widths = tuple(wd for _, wd, _ in mixed)
    row = _row_spec("flat", tm, D_MODEL, nt)
    return pl.pallas_call(
        functools.partial(_outffn_kernel, a_widths=a_widths, nc=nc),
        grid=(bsz, nt),
        in_specs=[row] + [_row_spec(kind, tm, wd, nt) for _, wd, kind in mixed]
        + [_const_spec(w_out.shape, True), _const_spec((1, D_MODEL)), _const_spec(w1.shape, True),
           _const_spec(w3.shape, True), _const_spec(w2.shape, True)],
        out_specs=row,
        out_shape=jax.ShapeDtypeStruct(res.shape, F32),
        scratch_shapes=[pltpu.VMEM((tm, D_MODEL), F32), pltpu.VMEM((tm, D_MODEL), BF16)],
        compiler_params=_cparams(("parallel", "parallel")),
        name="outproj_ffn",
    )(res, *[a for a, _, _ in mixed], w_out, gain.reshape(1, D_MODEL), w1, w3, w2)


def _final_norm_kernel(x_ref, g_ref, o_ref):
    o_ref[...] = _rms_scale(x_ref[...]) * g_ref[...]


def final_norm(x, gain, bsz, tb, t_out, tm):
    spec = pl.BlockSpec((None, tm, D_MODEL), lambda b, i: (b, i, 0))
    return pl.pallas_call(
        _final_norm_kernel,
        grid=(bsz, t_out // tm),
        in_specs=[spec, _const_spec((1, D_MODEL))],
        out_specs=spec,
        out_shape=jax.ShapeDtypeStruct((bsz, t_out, D_MODEL), F32),
        compiler_params=_cparams(("parallel", "parallel")),
        name="final_norm",
    )(x.reshape(bsz, tb, D_MODEL), gain.reshape(1, D_MODEL))


def _dot_nt(a, b):
    return lax.dot_general(a, b, (((1,), (1,)), ((), ())), preferred_element_type=F32)


def _dot_hi(a, b):
    return jnp.dot(a, b, precision=HIGHEST, preferred_element_type=F32)


def _bdot(a, b):
    return lax.dot_general(a, b, (((2,), (1,)), ((0,), (0,))), preferred_element_type=F32)


def _bdot_nt(a, b):
    return lax.dot_general(a, b, (((2,), (2,)), ((0,), (0,))), preferred_element_type=F32)


def _split(a):
    hi = a.astype(BF16)
    return hi, (a - hi.astype(F32)).astype(BF16)


def _bdot_split(a, b):
    return _bdot(a[0], b[0]) + _bdot(a[1], b[0]) + _bdot(a[0], b[1])


def _unit_lower_solve(a_low, rhs, csz):
    sub = 16
    r = lax.broadcasted_iota(jnp.int32, (csz, csz), 0)
    c = lax.broadcasted_iota(jnp.int32, (csz, csz), 1)
    same = ((r // sub) == (c // sub))[None]
    eye = (r == c).astype(F32)[None]
    dg = jnp.where(same, a_low, 0.0)
    off = a_low - dg
    d1 = _split(dg)
    d2 = _split(_bdot_split(d1, d1))
    d4 = _split(_bdot_split(d2, d2))
    d8 = _split(_bdot_split(d4, d4))
    t0 = eye - dg
    for dp in (d2, d4, d8):
        t0 = t0 + _bdot_split(_split(t0), dp)
    t0 = _split(t0)
    g1 = _split(_bdot_split(t0, _split(off)))
    x = _bdot_split(t0, _split(rhs))
    nblk = csz // sub
    powers = []
    gp = g1
    k = 2
    while k < nblk:
        gp = _split(_bdot_split(gp, gp))
        powers.append(gp)
        k *= 2
    for gp in powers:
        x = x + _bdot_split(gp, _split(x))
    return x - _bdot_split(g1, _split(x))


def _gdn_kernel(qkv_ref, z_ref, ba_ref, cw_ref, alog_ref, dtb_ref, gain_ref, conv0_ref, s0_ref,
                o_ref, sout_ref, xbuf, s_ref, *, csz, nb):
    c = pl.program_id(1)
    nh = nb * GDN_HEADS

    @pl.when(c == 0)
    def _():
        xbuf[:, 0:8, :] = conv0_ref[...]
        s_ref[...] = s0_ref[...].reshape(nh, GDN_DK, GDN_DK)

    xbuf[:, 8:8 + csz, :] = qkv_ref[...]

    r = lax.broadcasted_iota(jnp.int32, (csz, csz), 0)
    cc = lax.broadcasted_iota(jnp.int32, (csz, csz), 1)
    causal = r >= cc
    strict = r > cc
    gain = gain_ref[...]
    beta, gc, gct = [], [], []
    for bb in range(nb):
        ba = ba_ref[bb]
        beta.append(_sigmoid(ba))
        xs = ba + dtb_ref[...]
        softplus = jnp.maximum(xs, 0.0) + jnp.log(1.0 + jnp.exp(-jnp.abs(xs)))
        logg = -jnp.exp(alog_ref[...]) * softplus
        gc.append(_dot_hi(causal.astype(F32), logg))
        gct.append(gc[bb].T)

    def conv_silu(bb, col):
        y = cw_ref[3:4, col:col + GDN_DK] * xbuf[bb, 8:8 + csz, col:col + GDN_DK]
        for j in range(3):
            y = y + cw_ref[j:j + 1, col:col + GDN_DK] * xbuf[bb, 5 + j:5 + j + csz, col:col + GDN_DK]
        return _silu(y)

    heads = [(bb, h) for bb in range(nb) for h in range(GDN_HEADS)]
    q = jnp.stack([conv_silu(bb, h * GDN_DK) for bb, h in heads])
    k = jnp.stack([conv_silu(bb, (GDN_HEADS + h) * GDN_DK) for bb, h in heads])
    v = jnp.stack([conv_silu(bb, (2 * GDN_HEADS + h) * GDN_DK) for bb, h in heads])
    q = q * lax.rsqrt(jnp.sum(q * q, axis=-1, keepdims=True) + 1e-6) * (GDN_DK ** -0.5)
    k = k * lax.rsqrt(jnp.sum(k * k, axis=-1, keepdims=True) + 1e-6)
    bh = jnp.stack([beta[bb][:, h:h + 1] for bb, h in heads])
    gch = jnp.stack([gc[bb][:, 8 + h:9 + h] for bb, h in heads])
    grow = jnp.stack([gct[bb][8 + h:9 + h, :] for bb, h in heads])
    decay = jnp.where(causal[None], jnp.exp(gch - grow), 0.0)
    kb = k.astype(BF16)
    kk = _bdot_nt(kb, kb)
    qk = _bdot_nt(q.astype(BF16), kb)
    a_low = jnp.where(strict[None], bh * kk * decay, 0.0)
    eg = jnp.exp(gch)
    rhs = jnp.concatenate([v * bh, k * (bh * eg)], axis=-1)
    sol = _unit_lower_solve(a_low, rhs, csz)
    u = sol[:, :, :GDN_DK]
    w = sol[:, :, GDN_DK:]
    s_old = s_ref[...]
    sb = s_old.astype(BF16)
    v_new = u - _bdot(w.astype(BF16), sb)
    vb = v_new.astype(BF16)
    o = _bdot((q * eg).astype(BF16), sb) + _bdot((qk * decay).astype(BF16), vb)
    g_last = gch[:, csz - 1:csz, :]
    k2 = (k * jnp.exp(g_last - gch)).astype(BF16)
    on = _rms_scale(o) * gain
    for n, (bb, h) in enumerate(heads):
        s_ref[n] = s_old[n] * jnp.exp(g_last[n]) + jnp.dot(k2[n].T, vb[n], preferred_element_type=F32)
        zz = z_ref[bb, :, h * GDN_DK:(h + 1) * GDN_DK]
        o_ref[bb, :, h * GDN_DK:(h + 1) * GDN_DK] = (on[n] * _silu(zz)).astype(BF16)

    xbuf[:, 0:8, :] = xbuf[:, csz:csz + 8, :]

    @pl.when(c == pl.num_programs(1) - 1)
    def _():
        sout_ref[...] = s_ref[...].reshape(nb, GDN_HEADS, GDN_DK, GDN_DK)


GDN_BATCHES_PER_STEP = 4


def gdn(qkv, z, ba, conv_w, a_log, dt_bias, out_gain, conv0, s0, bsz, tb, csz, first_block):
    nblk = tb // csz
    nb = GDN_BATCHES_PER_STEP
    qk_w = 3 * GDN_HEADS * GDN_DK
    o_w = GDN_HEADS * GDN_DK

    def rows(b, c):
        return (b, (c + first_block) % nblk, 0)

    def lane_pad(vec):
        return jnp.zeros((1, 128), F32).at[0, 8:16].set(vec.astype(F32))

    conv0p = jnp.concatenate([jnp.zeros((bsz, 8 - (CONV_W - 1), qk_w), F32), conv0.astype(F32)], axis=1)
    state = pl.BlockSpec((nb, GDN_HEADS, GDN_DK, GDN_DK), lambda b, c: (b, 0, 0, 0))
    o, s_new = pl.pallas_call(
        functools.partial(_gdn_kernel, csz=csz, nb=nb),
        grid=(bsz // nb, nblk),
        in_specs=[pl.BlockSpec((nb, csz, qk_w), rows), pl.BlockSpec((nb, csz, o_w), rows),
                  pl.BlockSpec((nb, csz, 128), rows), _const_spec((CONV_W, qk_w)), _const_spec((1, 128)),
                  _const_spec((1, 128)), _const_spec((1, GDN_DK)),
                  pl.BlockSpec((nb, 8, qk_w), lambda b, c: (b, 0, 0)), state],
        out_specs=[pl.BlockSpec((nb, csz, o_w), rows), state],
        out_shape=[jax.ShapeDtypeStruct((bsz, tb, o_w), BF16),
                   jax.ShapeDtypeStruct((bsz, GDN_HEADS, GDN_DK, GDN_DK), F32)],
        scratch_shapes=[pltpu.VMEM((nb, 8 + csz, qk_w), F32), pltpu.VMEM((nb * GDN_HEADS, GDN_DK, GDN_DK), F32)],
        compiler_params=_cparams(("parallel", "arbitrary")),
        name="gdn",
    )(qkv.reshape(bsz, tb, qk_w), z.reshape(bsz, tb, o_w), ba.reshape(bsz, tb, 128), conv_w.astype(F32),
      lane_pad(a_log), lane_pad(dt_bias), out_gain.reshape(1, GDN_DK).astype(F32), conv0p, s0.astype(F32))
    return o.reshape(bsz * tb, o_w), s_new


S5_CHUNKS = 4
S5_CH_U = SSM_WIDTH // S5_CHUNKS
S5_CH_P = S5_CH_U // SSM_GROUP * SSM_STATE


def _s5_kernel(u_ref, wb_ref, wc_ref, a_ref, d_ref, gw_ref, gb_ref, st0_ref, o_ref, stout_ref, xs, st_ref,
               *, tt, bsz):
    c = pl.program_id(0)

    @pl.when(c == 0)
    def _():
        st_ref[...] = st0_ref[...]

    rows = tt * bsz
    u = u_ref[...].reshape(rows, SSM_WIDTH)
    ub = u.astype(BF16)
    for j in range(S5_CHUNKS):
        xs[j] = jnp.dot(ub[:, j * S5_CH_U:(j + 1) * S5_CH_U], wb_ref[j], preferred_element_type=F32)

    for j in range(S5_CHUNKS):
        ar = jnp.broadcast_to(a_ref[j, 0:1, :], (bsz, S5_CH_P))
        ai = jnp.broadcast_to(a_ref[j, 1:2, :], (bsz, S5_CH_P))

        def step(t, carry, j=j, ar=ar, ai=ai):
            re, im = carry
            row = pl.multiple_of(t * bsz, bsz)
            nre = ar * re - ai * im + xs[j, pl.ds(row, bsz), 0:S5_CH_P]
            nim = ar * im + ai * re + xs[j, pl.ds(row, bsz), S5_CH_P:2 * S5_CH_P]
            xs[j, pl.ds(row, bsz), 0:S5_CH_P] = nre
            xs[j, pl.ds(row, bsz), S5_CH_P:2 * S5_CH_P] = nim
            return nre, nim

        re, im = lax.fori_loop(0, tt, step, (st_ref[j, :, 0:S5_CH_P], st_ref[j, :, S5_CH_P:2 * S5_CH_P]))
        st_ref[j, :, 0:S5_CH_P] = re
        st_ref[j, :, S5_CH_P:2 * S5_CH_P] = im

    ys = [jnp.dot(xs[j].astype(BF16), wc_ref[j], preferred_element_type=F32) for j in range(S5_CHUNKS)]
    y = jnp.concatenate(ys, axis=-1) + u * d_ref[...]
    hg = 0.5 * y * (1.0 + jnp.tanh(math.sqrt(2.0 / math.pi) * (y + 0.044715 * (y * y * y))))
    gate = _sigmoid(jnp.dot(hg.astype(BF16), gw_ref[...], preferred_element_type=F32) + gb_ref[...])
    o_ref[...] = (hg * gate).reshape(tt, bsz, SSM_WIDTH)

    @pl.when(c == pl.num_programs(0) - 1)
    def _():
        stout_ref[...] = st_ref[...]


def s5(u_tb, re0, im0, lam_re, lam_im, log_dt, b_re, b_im, c_re, c_im, d_skip, glu_w, glu_b,
       bsz, tb, tt, first_tile):
    ntile = tb // tt
    lr = jnp.minimum(lam_re.astype(F32), -1e-4)
    li = lam_im.astype(F32)
    dt = jnp.exp(log_dt.astype(F32))[:, None]
    mag = jnp.exp(lr * dt)
    ar, ai = mag * jnp.cos(li * dt), mag * jnp.sin(li * dt)
    den = lr * lr + li * li
    nr, ni = ar - 1.0, ai
    cr, ci = (nr * lr + ni * li) / den, (ni * lr - nr * li) / den
    br, bi = b_re.astype(F32), b_im.astype(F32)
    bbr = cr[..., None] * br - ci[..., None] * bi
    bbi = cr[..., None] * bi + ci[..., None] * br
    gpc = S5_CH_U // SSM_GROUP

    def block_diag_in(bb):
        bb = bb.reshape(S5_CHUNKS, gpc, SSM_STATE, SSM_GROUP)
        eye = jnp.eye(gpc, dtype=F32)
        return jnp.einsum("jgpm,gh->jgmhp", bb, eye).reshape(S5_CHUNKS, S5_CH_U, S5_CH_P)

    def block_diag_out(cm):
        cm = cm.reshape(S5_CHUNKS, gpc, SSM_GROUP, SSM_STATE)
        eye = jnp.eye(gpc, dtype=F32)
        return jnp.einsum("jgmp,gh->jgphm", cm, eye).reshape(S5_CHUNKS, S5_CH_P, S5_CH_U)

    wb = jnp.concatenate([block_diag_in(bbr), block_diag_in(bbi)], axis=-1).astype(BF16)
    wc = jnp.concatenate([block_diag_out(c_re.astype(F32)), -block_diag_out(c_im.astype(F32))],
                         axis=1).astype(BF16)
    a_pack = jnp.stack([ar.reshape(S5_CHUNKS, S5_CH_P), ai.reshape(S5_CHUNKS, S5_CH_P)], axis=1)
    st0 = jnp.concatenate([re0.astype(F32).reshape(bsz, S5_CHUNKS, S5_CH_P),
                           im0.astype(F32).reshape(bsz, S5_CHUNKS, S5_CH_P)], axis=-1).transpose(1, 0, 2)

    def tile(c):
        return ((c + first_tile) % ntile, 0, 0)

    out, st = pl.pallas_call(
        functools.partial(_s5_kernel, tt=tt, bsz=bsz),
        grid=(ntile,),
        in_specs=[pl.BlockSpec((tt, bsz, SSM_WIDTH), tile), _const_spec(wb.shape), _const_spec(wc.shape),
                  _const_spec(a_pack.shape), _const_spec((1, SSM_WIDTH)), _const_spec((SSM_WIDTH, SSM_WIDTH)),
                  _const_spec((1, SSM_WIDTH)), _const_spec(st0.shape)],
        out_specs=[pl.BlockSpec((tt, bsz, SSM_WIDTH), tile), _const_spec(st0.shape)],
        out_shape=[jax.ShapeDtypeStruct((tb, bsz, SSM_WIDTH), F32), jax.ShapeDtypeStruct(st0.shape, F32)],
        scratch_shapes=[pltpu.VMEM((S5_CHUNKS, tt * bsz, 2 * S5_CH_P), F32),
                        pltpu.VMEM((S5_CHUNKS, bsz, 2 * S5_CH_P), F32)],
        compiler_params=_cparams(("arbitrary",)),
        name="s5",
    )(u_tb, wb, wc, a_pack, d_skip.reshape(1, SSM_WIDTH).astype(F32), glu_w.astype(BF16),
      glu_b.reshape(1, SSM_WIDTH).astype(F32), st0)
    st = st.transpose(1, 0, 2)
    re_new = st[..., :S5_CH_P].reshape(bsz, SSM_GROUPS, SSM_STATE)
    im_new = st[..., S5_CH_P:].reshape(bsz, SSM_GROUPS, SSM_STATE)
    return out, re_new, im_new


EVEN_OUTS = ((3 * GDN_HEADS * GDN_DK, "flat"), (GDN_HEADS * GDN_DK, "flat"), (128, "flat"), (SSM_WIDTH, "tmajor"))


def pack_even_w_in(w_in):
    qkv_w = 3 * GDN_HEADS * GDN_DK
    z_w = GDN_HEADS * GDN_DK
    c0 = qkv_w + z_w
    ba = jnp.zeros((D_MODEL, 128), w_in.dtype).at[:, :2 * GDN_HEADS].set(w_in[:, c0:c0 + 2 * GDN_HEADS])
    return jnp.concatenate([w_in[:, :c0], ba, w_in[:, c0 + 2 * GDN_HEADS:]], axis=1).astype(BF16)


def even_mixer(x, gain, p, conv0, s0, re0, im0, bsz, tb, tm, csz, first_block, tt):
    qkv, z, ba, u_t = inproj(x, gain, p["w_in"], EVEN_OUTS, bsz, tb, tm)
    o_gdn, s_new = gdn(qkv, z, ba, p["conv_w"], p["a_log"], p["dt_bias"], p["out_gain"], conv0, s0,
                       bsz, tb, csz, first_block)
    o_s5, re_new, im_new = s5(u_t.reshape(tb, bsz, SSM_WIDTH), re0, im0, p["lam_re"], p["lam_im"], p["log_dt"],
                              p["b_re"], p["b_im"], p["c_re"], p["c_im"], p["d"], p["glu_w"], p["glu_b"],
                              bsz, tb, tt, first_block * csz // tt)
    mixed = [(o_gdn, GDN_HEADS * GDN_DK, "flat"), (o_s5.reshape(tb, bsz * SSM_WIDTH), SSM_WIDTH, "tmajor")]
    return mixed, qkv, s_new, re_new, im_new


def _gate_kernel(f_ref, bias_ref, c0_ref, logf_ref, cum_ref, tot_ref, *, segments, log_sigmoid):
    carry = c0_ref[...]
    for start, size, pad_rows in segments:
        x = f_ref[start:start + size, :]
        if log_sigmoid:
            x = x + bias_ref[...]
            x = jnp.minimum(x, 0.0) - jnp.log(1.0 + jnp.exp(-jnp.abs(x)))
        if pad_rows:
            x = jnp.where(lax.broadcasted_iota(jnp.int32, x.shape, 0) >= pad_rows, x, 0.0)
        r = lax.broadcasted_iota(jnp.int32, (size, size), 0)
        cc = lax.broadcasted_iota(jnp.int32, (size, size), 1)
        cum = _dot_hi((r >= cc).astype(F32), x) + carry
        logf_ref[start:start + size, :] = x
        cum_ref[start:start + size, :] = cum * LOG2E
        carry = cum[size - 1:size, :]
    tot_ref[...] = carry


def gate_cumsum(f, bias, carry0, bsz, tb, segments, log_sigmoid):
    seq = pl.BlockSpec((None, tb, 128), lambda b: (b, 0, 0))
    one = pl.BlockSpec((None, 1, 128), lambda b: (b, 0, 0))
    return pl.pallas_call(
        functools.partial(_gate_kernel, segments=tuple(segments), log_sigmoid=log_sigmoid),
        grid=(bsz,),
        in_specs=[seq, _const_spec((1, 128)), one],
        out_specs=[seq, seq, one],
        out_shape=[jax.ShapeDtypeStruct((bsz, tb, 128), F32), jax.ShapeDtypeStruct((bsz, tb, 128), F32),
                   jax.ShapeDtypeStruct((bsz, 1, 128), F32)],
        compiler_params=_cparams(("parallel",)),
        name="gate_cumsum",
    )(f, bias, carry0)


N_BIAS_TERMS = 3


def _head_padded(x, scale, other):
    lane = lax.broadcasted_iota(jnp.int32, (x.shape[0], 128), 1)
    parts = []
    for h in range(x.shape[1] // HEAD_DIM):
        g = x[:, (h // 2) * 128:(h // 2 + 1) * 128] * scale
        own = (lane >= HEAD_DIM) == bool(h % 2)
        parts.append(jnp.where(own, g, other(h, lane)).astype(BF16))
    return jnp.concatenate(parts, axis=-1)


def _rotary(x, c_ref, s1_ref, s2_ref):
    width = x.shape[-1]
    return (x * c_ref[...] + pltpu.roll(x, width - ROT_DIM // 2, 1) * s1_ref[...]
            + pltpu.roll(x, ROT_DIM // 2, 1) * s2_ref[...])


def _prep_q_kernel(qf_ref, qd_ref, c_ref, s1_ref, s2_ref, qfa_ref, qda_ref):
    qscale = HEAD_DIM ** -0.5 * LOG2E
    ones = lambda h, lane: jnp.where(lane % HEAD_DIM < N_BIAS_TERMS, 1.0, 0.0)
    qfa_ref[...] = _head_padded(qf_ref[...], qscale, ones)
    qda_ref[...] = _head_padded(_rotary(qd_ref[...], c_ref, s1_ref, s2_ref), qscale, lambda h, lane: 0.0)


def _prep_kv_kernel(*refs, rotate):
    if rotate:
        kf_ref, vf_ref, kd_ref, vd_ref, cum_ref, c_ref, s1_ref, s2_ref, kfa_ref, vfa_ref, kdb_ref, vda_ref, kdr_ref = refs
        kd = _rotary(kd_ref[...], c_ref, s1_ref, s2_ref)
        kdr_ref[...] = kd
    else:
        kf_ref, vf_ref, kd_ref, vd_ref, cum_ref, kfa_ref, vfa_ref, kdb_ref, vda_ref = refs
        kd = kd_ref[...]
    cum = cum_ref[...]

    def offset_pieces(h, lane):
        c = jnp.broadcast_to(cum[:, h:h + 1], lane.shape)
        hi = c.astype(BF16).astype(F32)
        mid = (c - hi).astype(BF16).astype(F32)
        lo = c - hi - mid
        sel = lane % HEAD_DIM
        return jnp.where(sel == 0, -hi, jnp.where(sel == 1, -mid, jnp.where(sel == 2, -lo, 0.0)))

    kfa_ref[...] = _head_padded(kf_ref[...], 1.0, offset_pieces)
    vfa_ref[...] = _head_padded(vf_ref[...], 1.0, lambda h, lane: 1.0)
    kdb_ref[...] = kd.astype(BF16)
    vd = vd_ref[...].astype(BF16)
    ones = jnp.ones((vd.shape[0], DIFF_V), BF16)
    vda_ref[...] = jnp.concatenate(
        [a for hd in range(DIFF_HEADS) for a in (vd[:, hd * DIFF_V:(hd + 1) * DIFF_V], ones)], axis=-1)


def attn_prep_q(qf, qd, tables, bsz, tb, tm):
    width = qf.shape[-1]
    xs = pl.BlockSpec((None, tm, width), lambda i, b: (b, i, 0))
    qs = pl.BlockSpec((None, tm, 2 * width), lambda i, b: (b, i, 0))
    ts = pl.BlockSpec((tm, width), lambda i, b: (i, 0))
    shp = jax.ShapeDtypeStruct((bsz, tb, 2 * width), BF16)
    return pl.pallas_call(
        _prep_q_kernel,
        grid=(tb // tm, bsz),
        in_specs=[xs, xs, ts, ts, ts],
        out_specs=[qs, qs],
        out_shape=[shp, shp],
        compiler_params=_cparams(("parallel", "parallel")),
        name="attn_prep_q",
    )(qf, qd, *tables)


def attn_prep_kv(kf, vf, kd, vd, cum2, tables, bsz, tb, tm):
    width = kf.shape[-1]
    rotate = tables is not None
    xs = pl.BlockSpec((None, tm, width), lambda i, b: (b, i, 0))
    ws = pl.BlockSpec((None, tm, 2 * width), lambda i, b: (b, i, 0))
    cs = pl.BlockSpec((None, tm, 128), lambda i, b: (b, i, 0))
    ts = pl.BlockSpec((tm, width), lambda i, b: (i, 0))
    shp = lambda w, dt: jax.ShapeDtypeStruct((bsz, tb, w), dt)
    outs = pl.pallas_call(
        functools.partial(_prep_kv_kernel, rotate=rotate),
        grid=(tb // tm, bsz),
        in_specs=[xs] * 4 + [cs] + ([ts] * 3 if rotate else []),
        out_specs=[ws, ws, xs, ws] + ([xs] if rotate else []),
        out_shape=[shp(2 * width, BF16), shp(2 * width, BF16), shp(width, BF16), shp(2 * width, BF16)]
        + ([shp(width, F32)] if rotate else []),
        compiler_params=_cparams(("parallel", "parallel")),
        name="attn_prep_kv",
    )(kf, vf, kd, vd, cum2, *(tables if rotate else ()))
    return outs if rotate else list(outs) + [None]


def rope_tables(pos, nheads):
    half = ROT_DIM // 2
    inv_freq = ROPE_THETA ** (-jnp.arange(0, ROT_DIM, 2, dtype=F32) / ROT_DIM)
    ang = pos.astype(F32)[:, None] * inv_freq[None, :]
    cos, sin = jnp.cos(ang), jnp.sin(ang)
    t = pos.shape[0]
    rest = jnp.zeros((t, HEAD_DIM - ROT_DIM), F32)
    z = jnp.zeros((t, half), F32)
    ctab = jnp.concatenate([cos, cos, rest + 1.0], axis=1)
    s1 = jnp.concatenate([-sin, z, rest], axis=1)
    s2 = jnp.concatenate([z, sin, rest], axis=1)
    return tuple(jnp.tile(a, (1, nheads)) for a in (ctab, s1, s2))


NEG_INF = -1e30
N_MAPS = FOX_HEADS + 2 * DIFF_HEADS


def _attn_kernel(*refs, tq, tk, te, has_main, main_causal, fox_extra, diff_extra, zero_pad_q, pad_rows,
                 out_scale):
    if has_main:
        (qf_ref, qd_ref, cq_ref, kf_ref, vf_ref, kd_ref, vd_ref,
         kfe_ref, vfe_ref, kde_ref, vde_ref, lam_ref, gain_ref,
         o_ref, m_ref, cqb_ref, accf_ref, accd_ref) = refs
    else:
        (qf_ref, qd_ref, cq_ref, kfe_ref, vfe_ref, kde_ref, vde_ref, lam_ref, gain_ref,
         o_ref, m_ref, cqb_ref, accf_ref, accd_ref) = refs
    i = pl.program_id(1)
    j = pl.program_id(2)
    neg = NEG_INF * LOG2E

    def lanes(x, ncols):
        reps, rem = divmod(ncols, 128)
        return jnp.concatenate([x] * reps + ([x[:, :rem]] if rem else []), axis=-1)

    def one_map(idx, q, k, v, acc_ref, slot, cq, mask, qr):
        t = _dot_nt(q, k)
        nrows, ncols = t.shape
        if mask is not None:
            t = jnp.where(mask, t, neg)
        mt = jnp.broadcast_to(jnp.max(t, axis=-1, keepdims=True), (nrows, 128))
        m_prev = m_ref[idx, qr]
        if cq is None:
            m_new = jnp.maximum(m_prev, mt)
            shift = m_new
        else:
            m_new = jnp.maximum(m_prev, mt + cq)
            shift = m_new - cq
        p = jnp.exp2(t - lanes(shift, ncols))
        alpha = jnp.exp2(m_prev - m_new)
        m_ref[idx, qr] = m_new
        acc_ref[slot, qr] = (lanes(alpha, acc_ref.shape[-1]) * acc_ref[slot, qr]
                             + jnp.dot(p.astype(BF16), v, preferred_element_type=F32))

    def process(k_tiles, fox_mask, diff_mask, qr=slice(None)):
        def rows(which, lo, hi):
            parts = [tile[which][:n, lo:hi] for tile, n in k_tiles]
            return parts[0] if len(parts) == 1 else jnp.concatenate(parts, axis=0)

        for h in range(FOX_HEADS):
            lo, hi = h * 128, (h + 1) * 128
            one_map(h, qf_ref[qr, lo:hi], rows(0, lo, hi), rows(1, lo, hi), accf_ref, h, cqb_ref[h, qr], fox_mask, qr)
        for hm in range(2 * DIFF_HEADS):
            g = hm // 2
            one_map(FOX_HEADS + hm, qd_ref[qr, hm * 128:(hm + 1) * 128], rows(2, g * 128, (g + 1) * 128),
                    rows(3, g * 2 * DIFF_V, (g + 1) * 2 * DIFF_V), accd_ref, hm, None, diff_mask, qr)

    def visible(extra_kind, main_kind, width, r0=0, nrows=tq):
        if extra_kind == "none" and main_kind == "none":
            return None
        r = lax.broadcasted_iota(jnp.int32, (nrows, width), 0) + r0
        c = lax.broadcasted_iota(jnp.int32, (nrows, width), 1)
        cm = c - te
        rules = {"valid": c >= pad_rows, "causal": c <= r,
                 "causal_valid": jnp.logical_and(c <= r, c >= pad_rows),
                 "main_causal": cm <= r,
                 "main_block_causal": cm < (lax.shift_right_logical(r, 6) + 1) * BLOCK}
        in_extra = c < te
        if main_kind == "none":
            return jnp.logical_or(c >= te, rules[extra_kind])
        if extra_kind == "none":
            return jnp.logical_or(in_extra, rules[main_kind])
        return jnp.logical_or(jnp.logical_and(in_extra, rules[extra_kind]),
                              jnp.logical_and(c >= te, rules[main_kind]))

    @pl.when(j == 0)
    def _():
        m_ref[...] = jnp.full(m_ref.shape, NEG_INF, F32)
        accf_ref[...] = jnp.zeros(accf_ref.shape, F32)
        accd_ref[...] = jnp.zeros(accd_ref.shape, F32)
        for h in range(FOX_HEADS):
            cqb_ref[h] = jnp.broadcast_to(cq_ref[:, h:h + 1], (tq, 128))

    extra_tile = (kfe_ref, vfe_ref, kde_ref, vde_ref)
    if has_main:
        main_tile = (kf_ref, vf_ref, kd_ref, vd_ref)
        if main_causal:
            assert tq == tk
            last, before_last = j == i, j < i
            fox_main, diff_main = "main_causal", "main_block_causal"
        else:
            last, before_last = j == pl.num_programs(2) - 1, j < pl.num_programs(2) - 1
            fox_main = diff_main = "none"

        @pl.when(before_last)
        def _():
            process([(main_tile, tk)], None, None)

        @pl.when(last)
        def _():
            process([(extra_tile, te), (main_tile, tk)], visible(fox_extra, fox_main, te + tk),
                    visible(diff_extra, diff_main, te + tk))
    else:
        last = j == 0
        process([(extra_tile, te)], visible(fox_extra, "none", te), visible(diff_extra, "none", te))

    @pl.when(last)
    def _():
        lam = lam_ref[...]
        upper = lax.broadcasted_iota(jnp.int32, (tq, 128), 1) >= HEAD_DIM
        outs = []
        for g in range(FOX_HEADS // 2):
            a0, a1 = accf_ref[2 * g], accf_ref[2 * g + 1]
            outs.append(jnp.where(upper, a1 / pltpu.roll(a1, HEAD_DIM, 1), a0 / pltpu.roll(a0, HEAD_DIM, 1)))
        for hd in range(DIFF_HEADS):
            a1, a2 = accd_ref[2 * hd], accd_ref[2 * hd + 1]
            o = a1[:, :DIFF_V] / a1[:, DIFF_V:] - lam * (a2[:, :DIFF_V] / a2[:, DIFF_V:])
            outs.append(_rms_scale(o) * gain_ref[...] * out_scale)
        out = jnp.concatenate(outs, axis=-1)
        if zero_pad_q:
            out = jnp.where(lax.broadcasted_iota(jnp.int32, out.shape, 0) >= pad_rows, out, 0.0)
        o_ref[...] = out.astype(BF16)


def attention(q_arrays, q_rows, main, extra, lam, gain, out_scale, bsz, tq, nq, tk, te,
              main_causal, fox_extra, diff_extra, zero_pad_q, q_block0=0, pad_rows=BLOCK - N_META):
    qf, qd, cumq = q_arrays
    has_main = main is not None
    nk = main[0].shape[1] // tk if has_main else 1
    kv_widths = (FOX_HEADS * 128, FOX_HEADS * 128, 2 * DIFF_HEADS * HEAD_DIM, DIFF_HEADS * 2 * DIFF_V)

    def qspec(width):
        return pl.BlockSpec((None, tq, width), lambda b, i, j: (b, i + q_block0, 0))

    in_specs = [qspec(FOX_HEADS * 128), qspec(2 * DIFF_HEADS * 128), qspec(128)]
    args = [qf, qd, cumq]
    if has_main:
        if main_causal:
            kmap = lambda b, i, j: (b, jnp.minimum(j, i), 0)
        else:
            kmap = lambda b, i, j: (b, j, 0)
        in_specs += [pl.BlockSpec((None, tk, w), kmap) for w in kv_widths]
        args += list(main)
    eblk = extra[4]
    emap = lambda b, i, j: (b, eblk, 0)
    in_specs += [pl.BlockSpec((None, te, w), emap) for w in kv_widths] + [_const_spec((1, 1)), _const_spec((1, DIFF_V))]
    args += list(extra[:4]) + [lam.reshape(1, 1).astype(F32), gain.reshape(1, DIFF_V).astype(F32)]
    width = FOX_HEADS * HEAD_DIM + DIFF_HEADS * DIFF_V
    return pl.pallas_call(
        functools.partial(_attn_kernel, tq=tq, tk=tk, te=te, has_main=has_main, main_causal=main_causal,
                          fox_extra=fox_extra, diff_extra=diff_extra, zero_pad_q=zero_pad_q, pad_rows=pad_rows,
                          out_scale=out_scale),
        grid=(bsz, nq, nk),
        in_specs=in_specs,
        out_specs=pl.BlockSpec((None, tq, width), lambda b, i, j: (b, i, 0)),
        out_shape=jax.ShapeDtypeStruct((bsz, nq * tq, width), BF16),
        scratch_shapes=[pltpu.VMEM((N_MAPS, tq, 128), F32), pltpu.VMEM((FOX_HEADS, tq, 128), F32),
                        pltpu.VMEM((FOX_HEADS, tq, 128), F32), pltpu.VMEM((2 * DIFF_HEADS, tq, 2 * DIFF_V), F32)],
        compiler_params=_cparams(("parallel", "parallel", "arbitrary")),
        name="attention",
    )(*args)


ODD_OUTS = ((512, "flat"),) * 6 + ((128, "flat"),)


def pack_odd_w_in(w_in):
    c0 = 3 * FOX_HEADS * HEAD_DIM
    f = jnp.zeros((D_MODEL, 128), w_in.dtype).at[:, :FOX_HEADS].set(w_in[:, c0:c0 + FOX_HEADS])
    return jnp.concatenate([w_in[:, :c0], w_in[:, c0 + FOX_HEADS:], f], axis=1).astype(BF16)


def diff_lambda_value(diff_lambda, lam_init):
    lq1, lk1, lq2, lk2 = diff_lambda.astype(F32)
    return jnp.exp(jnp.sum(lq1 * lk1)) - jnp.exp(jnp.sum(lq2 * lk2)) + lam_init


def _logical_kernel(x_ref, o_ref, *, seq):
    o_ref[0:N_META, :] = x_ref[x_ref.shape[0] - N_META:, :]
    o_ref[N_META:, :] = x_ref[0:seq, :]


def slab_to_logical(x, seq):
    bsz, tb, width = x.shape
    return pl.pallas_call(
        functools.partial(_logical_kernel, seq=seq),
        grid=(bsz,),
        in_specs=[pl.BlockSpec((None, tb, width), lambda b: (b, 0, 0))],
        out_specs=pl.BlockSpec((None, N_META + seq, width), lambda b: (b, 0, 0)),
        out_shape=jax.ShapeDtypeStruct((bsz, N_META + seq, width), x.dtype),
        compiler_params=_cparams(("parallel",)),
        name="slab_to_logical",
    )(x)


def odd_mixer_prompt(x, gain, p, lam_init, bsz, tb, nframes, tm, tq):
    qf, kf, vf, qd, kd, vd, f = [a.reshape(bsz, tb, -1) for a in inproj(x, gain, p["w_in"], ODD_OUTS, bsz, tb, tm)]
    meta_blk = tb // BLOCK - 1
    pad = BLOCK - N_META
    segments = [(nframes, BLOCK, pad)] + [(s, tq, 0) for s in range(0, nframes, tq)]
    logf, cum, _ = gate_cumsum(f, p["f_bias"], jnp.zeros((bsz, 1, 128), F32), bsz, tb, segments, True)
    pos = jnp.concatenate([N_META + jnp.arange(nframes, dtype=jnp.int32), jnp.zeros((pad,), jnp.int32),
                           jnp.arange(N_META, dtype=jnp.int32)])
    tables = rope_tables(pos, 2 * DIFF_HEADS)
    qfa, qda = attn_prep_q(qf, qd, tables, bsz, tb, tm)
    kfa, vfa, kdb, vda, kd = attn_prep_kv(kf, vf, kd, vd, cum, tables, bsz, tb, tm)
    nk = nframes // tq
    lam = diff_lambda_value(p["diff_lambda"], lam_init)
    extra = (kfa, vfa, kdb, vda, meta_blk)
    common = dict(lam=lam, gain=p["diff_gain"], out_scale=1.0 - lam_init, bsz=bsz, te=BLOCK, diff_extra="valid")
    frames = attention((qfa, qda, cum), tb, (kfa, vfa, kdb, vda), extra, tq=tq, nq=nk, tk=tq,
                       main_causal=True, fox_extra="valid", zero_pad_q=False, **common)
    meta = attention((qfa, qda, cum), tb, None, extra, tq=BLOCK, nq=1, tk=BLOCK, main_causal=False,
                     fox_extra="causal_valid", zero_pad_q=True, q_block0=meta_blk, **common)
    mixed = jnp.concatenate([frames, meta], axis=1).reshape(bsz * tb, -1)
    return mixed, kf, vf, logf, kd, vd


def odd_mixer_sample(x, gain, p, lam_init, kf_past, vf_past, lf_past, kd_past, vd_past, bsz, tb, tk):
    qf, kf, vf, qd, kd, vd, f = [a.reshape(bsz, tb, -1) for a in inproj(x, gain, p["w_in"], ODD_OUTS, bsz, tb, tb)]
    past = kf_past.shape[1]
    nk = past // tk
    lf128 = jnp.zeros((bsz, past, 128), F32).at[:, :, :FOX_HEADS].set(lf_past.astype(F32))
    zero_c = jnp.zeros((bsz, 1, 128), F32)
    _, cum_past, total = gate_cumsum(lf128, p["f_bias"], zero_c, bsz, past, [(s, tk, 0) for s in range(0, past, tk)],
                                     False)
    logf, cum, _ = gate_cumsum(f, p["f_bias"], total, bsz, tb, [(0, tb, 0)], True)
    tables = rope_tables(past + jnp.arange(tb, dtype=jnp.int32), 2 * DIFF_HEADS)
    qfa, qda = attn_prep_q(qf, qd, tables, bsz, tb, tb)
    kfa, vfa, kdb, vda, kd = attn_prep_kv(kf, vf, kd, vd, cum, tables, bsz, tb, tb)
    flat = lambda a: a.astype(F32).reshape(bsz, past, -1)
    main = tuple(attn_prep_kv(flat(kf_past), flat(vf_past), flat(kd_past), flat(vd_past), cum_past, None,
                              bsz, past, tk)[:4])
    extra = (kfa, vfa, kdb, vda, 0)
    mixed = attention((qfa, qda, cum), tb, main, extra, lam=diff_lambda_value(p["diff_lambda"], lam_init),
                      gain=p["diff_gain"], out_scale=1.0 - lam_init, bsz=bsz, tq=tb, nq=1, tk=tk, te=tb,
                      main_causal=False, fox_extra="causal", diff_extra="none", zero_pad_q=False)
    return mixed.reshape(bsz * tb, -1), kf, vf, logf, kd, vd


def kernel(x_prompt, x_sample, state_conv, state_delta, state_ssm_re, state_ssm_im, cache_fox_k, cache_fox_v,
           cache_fox_logf, cache_diff_k, cache_diff_v, meta_tokens, norm_mix, norm_ffn, norm_final, w_in_even,
           w_out_even, conv_w, gdn_a_log, gdn_dt_bias, gdn_out_norm, ssm_lambda_re, ssm_lambda_im, ssm_log_dt,
           ssm_b_re, ssm_b_im, ssm_c_re, ssm_c_im, ssm_d, ssm_glu_w, ssm_glu_b, w_in_odd, w_out_odd, fox_f_bias,
           diff_lambda, diff_out_norm, ffn_w1, ffn_w3, ffn_w2):
    bp, seq, _ = x_prompt.shape
    bs, ls, _ = x_sample.shape
    depth = norm_mix.shape[0]
    tb = seq + BLOCK
    pad = BLOCK - N_META
    tm_p = 832
    tq = 512
    meta = jnp.broadcast_to(meta_tokens.astype(F32)[None], (bp, N_META, D_MODEL))
    xp = jnp.concatenate([x_prompt.astype(F32), jnp.zeros((bp, pad, D_MODEL), F32), meta], axis=1)
    xp = xp.reshape(bp * tb, D_MODEL)
    xs = x_sample.astype(F32).reshape(bs * ls, D_MODEL)

    def logical(a):
        return jnp.concatenate([a[:, tb - N_META:], a[:, :seq]], axis=1)

    outs = {k: [] for k in ("conv_p", "conv_s", "delta_p", "delta_s", "re_p", "re_s", "im_p", "im_s", "fk_p", "fk_s",
                            "fv_p", "fv_s", "fl_p", "fl_s", "dk_p", "dk_s", "dv_p", "dv_s")}
    nchunk = FFN_HIDDEN // FFN_CHUNK
    for l in range(depth):
        i = l // 2
        if l % 2 == 0:
            pe = dict(w_in=pack_even_w_in(w_in_even[i]), conv_w=conv_w[i], a_log=gdn_a_log[i], dt_bias=gdn_dt_bias[i],
                      out_gain=gdn_out_norm[i], lam_re=ssm_lambda_re[i], lam_im=ssm_lambda_im[i],
                      log_dt=ssm_log_dt[i], b_re=ssm_b_re[i], b_im=ssm_b_im[i], c_re=ssm_c_re[i], c_im=ssm_c_im[i],
                      d=ssm_d[i], glu_w=ssm_glu_w[i], glu_b=ssm_glu_b[i])
            qkv_w = 3 * GDN_HEADS * GDN_DK
            mixed_p, qkv_p, sd_p, re_p, im_p = even_mixer(
                xp, norm_mix[l], pe, jnp.zeros((bp, CONV_W - 1, qkv_w), F32),
                jnp.zeros((bp, GDN_HEADS, GDN_DK, GDN_DK), F32), jnp.zeros((bp, SSM_GROUPS, SSM_STATE), F32),
                jnp.zeros((bp, SSM_GROUPS, SSM_STATE), F32), bp, tb, tm_p, BLOCK, tb // BLOCK - 1, 32)
            mixed_s, qkv_s, sd_s, re_s, im_s = even_mixer(
                xs, norm_mix[l], pe, state_conv[i], state_delta[i], state_ssm_re[i], state_ssm_im[i],
                bs, ls, ls, ls, 0, ls)
            outs["conv_p"].append(qkv_p.reshape(bp, tb, qkv_w)[:, seq - (CONV_W - 1):seq])
            outs["conv_s"].append(jnp.concatenate([state_conv[i].astype(F32), qkv_s.reshape(bs, ls, qkv_w)],
                                                  axis=1)[:, ls:])
            outs["delta_p"].append(sd_p); outs["delta_s"].append(sd_s)
            outs["re_p"].append(re_p); outs["re_s"].append(re_s)
            outs["im_p"].append(im_p); outs["im_s"].append(im_s)
            w_out = w_out_even[i]
        else:
            lam_init = 0.8 - 0.6 * math.exp(-0.3 * l)
            po = dict(w_in=pack_odd_w_in(w_in_odd[i]), diff_lambda=diff_lambda[i], diff_gain=diff_out_norm[i],
                      f_bias=jnp.zeros((1, 128), F32).at[0, :FOX_HEADS].set(fox_f_bias[i].astype(F32)))
            mixed_p, kf, vf, lf, kd, vd = odd_mixer_prompt(xp, norm_mix[l], po, lam_init, bp, tb, seq, tm_p, tq)
            mixed_p = [(mixed_p, mixed_p.shape[-1], "flat")]
            outs["fk_p"].append(slab_to_logical(kf, seq).reshape(bp, -1, FOX_HEADS, HEAD_DIM))
            outs["fv_p"].append(slab_to_logical(vf, seq).reshape(bp, -1, FOX_HEADS, HEAD_DIM))
            outs["fl_p"].append(logical(lf)[:, :, :FOX_HEADS])
            outs["dk_p"].append(slab_to_logical(kd, seq).reshape(bp, -1, 2 * DIFF_HEADS, HEAD_DIM))
            outs["dv_p"].append(slab_to_logical(vd, seq).reshape(bp, -1, DIFF_HEADS, DIFF_V))
            mixed_s, kf, vf, lf, kd, vd = odd_mixer_sample(
                xs, norm_mix[l], po, lam_init, cache_fox_k[i], cache_fox_v[i], cache_fox_logf[i], cache_diff_k[i],
                cache_diff_v[i], bs, ls, 512)
            mixed_s = [(mixed_s, mixed_s.shape[-1], "flat")]
            outs["fk_s"].append(kf.reshape(bs, ls, FOX_HEADS, HEAD_DIM))
            outs["fv_s"].append(vf.reshape(bs, ls, FOX_HEADS, HEAD_DIM))
            outs["fl_s"].append(lf[:, :, :FOX_HEADS])
            outs["dk_s"].append(kd.reshape(bs, ls, 2 * DIFF_HEADS, HEAD_DIM))
            outs["dv_s"].append(vd.reshape(bs, ls, DIFF_HEADS, DIFF_V))
            w_out = w_out_odd[i]
        w1 = ffn_w1[l].astype(BF16).reshape(D_MODEL, nchunk, FFN_CHUNK).transpose(1, 0, 2)
        w3 = ffn_w3[l].astype(BF16).reshape(D_MODEL, nchunk, FFN_CHUNK).transpose(1, 0, 2)
        w2 = ffn_w2[l].astype(BF16).reshape(nchunk, FFN_CHUNK, D_MODEL)
        xp = outproj_ffn(xp, mixed_p, w_out.astype(BF16), norm_ffn[l], w1, w3, w2, bp, tb, tm_p)
        xs = outproj_ffn(xs, mixed_s, w_out.astype(BF16), norm_ffn[l], w1, w3, w2, bs, ls, ls)

    y_prompt = final_norm(xp, norm_final, bp, tb, seq, 512).reshape(bp, seq, D_MODEL)
    y_sample = final_norm(xs, norm_final, bs, ls, ls, ls).reshape(bs, ls, D_MODEL)
    order = ("conv_p", "conv_s", "delta_p", "delta_s", "re_p", "re_s", "im_p", "im_s", "fk_p", "fk_s", "fv_p", "fv_s",
             "fl_p", "fl_s", "dk_p", "dk_s", "dv_p", "dv_s")
    return (y_prompt, y_sample) + tuple(jnp.stack(outs[k]) for k in order)
```

```python
import functools
import math

import jax
import jax.numpy as jnp
import numpy as np
from jax import lax
from jax.experimental import pallas as pl
from jax.experimental.pallas import tpu as pltpu

F32 = jnp.float32
BF16 = jnp.bfloat16
HIGHEST = lax.Precision.HIGHEST
LOG2E = math.log2(math.e)

D_MODEL = 1024
N_META = 16
RMS_EPS = 1e-6
HEAD_DIM = 64
GDN_HEADS = 8
GDN_DK = 128
CONV_W = 4
SSM_WIDTH = 512
SSM_GROUP = 16
SSM_GROUPS = 32
SSM_STATE = 64
FOX_HEADS = 8
DIFF_HEADS = 4
DIFF_V = 128
ROT_DIM = 16
ROPE_THETA = 500000.0
FFN_HIDDEN = 2816
FFN_CHUNK = 256
BLOCK = 64
VMEM_LIMIT = 56 * 1024 * 1024


def _cparams(sem):
    return pltpu.CompilerParams(dimension_semantics=sem, vmem_limit_bytes=VMEM_LIMIT)


def _const_spec(shape, single_buffer=False):
    nd = len(shape)
    if single_buffer:
        return pl.BlockSpec(shape, lambda *_: (0,) * nd, pipeline_mode=pl.Buffered(1))
    return pl.BlockSpec(shape, lambda *_: (0,) * nd)


def _sigmoid(x):
    return 1.0 / (1.0 + jnp.exp(-x))


def _silu(x):
    return x * _sigmoid(x)


def _rms_scale(x):
    return x * lax.rsqrt(jnp.mean(x * x, axis=-1, keepdims=True) + RMS_EPS)


def _row_spec(kind, tm, width, nt):
    if kind == "flat":
        return pl.BlockSpec((tm, width), lambda b, i: (b * nt + i, 0))
    return pl.BlockSpec((tm, width), lambda b, i: (i, b))


def _row_shape(kind, bsz, tb, width, dtype):
    if kind == "flat":
        return jax.ShapeDtypeStruct((bsz * tb, width), dtype)
    return jax.ShapeDtypeStruct((tb, bsz * width), dtype)


def _inproj_kernel(x_ref, g_ref, w_ref, *o_refs, widths):
    xn = (_rms_scale(x_ref[...]) * g_ref[...]).astype(BF16)
    off = 0
    for o_ref, wd in zip(o_refs, widths):
        for c0 in range(0, wd, 512):
            cw = min(512, wd - c0)
            o_ref[:, c0:c0 + cw] = jnp.dot(xn, w_ref[:, off + c0:off + c0 + cw],
                                           preferred_element_type=F32)
        off += wd


def inproj(x, gain, w, outs, bsz, tb, tm):
    nt = tb // tm
    widths = tuple(wd for wd, _ in outs)
    return pl.pallas_call(
        functools.partial(_inproj_kernel, widths=widths),
        grid=(bsz, nt),
        in_specs=[_row_spec("flat", tm, D_MODEL, nt), _const_spec((1, D_MODEL)), _const_spec(w.shape, True)],
        out_specs=[_row_spec(kind, tm, wd, nt) for wd, kind in outs],
        out_shape=[_row_shape(kind, bsz, tb, wd, F32) for wd, kind in outs],
        compiler_params=_cparams(("parallel", "parallel")),
        name="inproj",
    )(x, gain.reshape(1, D_MODEL), w)


def _outffn_kernel(res_ref, *refs, a_widths, nc):
    na = len(a_widths)
    a_refs = refs[:na]
    wout_ref, g_ref, w1_ref, w3_ref, w2_ref, o_ref, acc_ref, xn_ref = refs[na:]
    x1 = res_ref[...]
    off = 0
    for a_ref, wd in zip(a_refs, a_widths):
        x1 = x1 + jnp.dot(a_ref[...].astype(BF16), wout_ref[off:off + wd, :], preferred_element_type=F32)
        off += wd
    xn_ref[...] = (_rms_scale(x1) * g_ref[...]).astype(BF16)
    acc_ref[...] = x1

    def body(c, carry):
        xn = xn_ref[...]
        h1 = jnp.dot(xn, w1_ref[c], preferred_element_type=F32)
        h3 = jnp.dot(xn, w3_ref[c], preferred_element_type=F32)
        act = (_silu(h1) * h3).astype(BF16)
        acc_ref[...] += jnp.dot(act, w2_ref[c], preferred_element_type=F32)
        return carry

    lax.fori_loop(0, nc, body, 0)
    o_ref[...] = acc_ref[...]


def outproj_ffn(res, mixed, w_out, gain, w1, w3, w2, bsz, tb, tm):
    nt = tb // tm
    nc = w1.shape[0]
    a_widths = tuple(wd for _, wd, _ in mixed)
    row = _row_spec("flat", tm, D_MODEL, nt)
    return pl.pallas_call(
        functools.partial(_outffn_kernel, a_widths=a_widths, nc=nc),
        grid=(bsz, nt),
        in_specs=[row] + [_row_spec(kind, tm, wd, nt) for _, wd, kind in mixed]
        + [_const_spec(w_out.shape, True), _const_spec((1, D_MODEL)), _const_spec(w1.shape, True),
           _const_spec(w3.shape, True), _const_spec(w2.shape, True)],
        out_specs=row,
        out_shape=jax.ShapeDtypeStruct(res.shape, F32),
        scratch_shapes=[pltpu.VMEM((tm, D_MODEL), F32), pltpu.VMEM((tm, D_MODEL), BF16)],
        compiler_params=_cparams(("parallel", "parallel")),
        name="outproj_ffn",
    )(res, *[a for a, _, _ in mixed], w_out, gain.reshape(1, D_MODEL), w1, w3, w2)


def _final_norm_kernel(x_ref, g_ref, o_ref):
    o_ref[...] = _rms_scale(x_ref[...]) * g_ref[...]


def final_norm(x, gain, bsz, tb, t_out, tm):
    spec = pl.BlockSpec((None, tm, D_MODEL), lambda b, i: (b, i, 0))
    return pl.pallas_call(
        _final_norm_kernel,
        grid=(bsz, t_out // tm),
        in_specs=[spec, _const_spec((1, D_MODEL))],
        out_specs=spec,
        out_shape=jax.ShapeDtypeStruct((bsz, t_out, D_MODEL), F32),
        compiler_params=_cparams(("parallel", "parallel")),
        name="final_norm",
    )(x.reshape(bsz, tb, D_MODEL), gain.reshape(1, D_MODEL))


def _dot_nt(a, b):
    return lax.dot_general(a, b, (((1,), (1,)), ((), ())), preferred_element_type=F32)


def _dot_hi(a, b):
    return jnp.dot(a, b, precision=HIGHEST, preferred_element_type=F32)


def _bdot(a, b):
    return lax.dot_general(a, b, (((2,), (1,)), ((0,), (0,))), preferred_element_type=F32)


def _bdot_nt(a, b):
    return lax.dot_general(a, b, (((2,), (2,)), ((0,), (0,))), preferred_element_type=F32)


def _split(a):
    hi = a.astype(BF16)
    return hi, (a - hi.astype(F32)).astype(BF16)


def _bdot_split(a, b):
    return _bdot(a[0], b[0]) + _bdot(a[1], b[0]) + _bdot(a[0], b[1])


def _unit_lower_solve(a_low, rhs, csz):
    sub = 16
    r = lax.broadcasted_iota(jnp.int32, (csz, csz), 0)
    c = lax.broadcasted_iota(jnp.int32, (csz, csz), 1)
    same = ((r // sub) == (c // sub))[None]
    eye = (r == c).astype(F32)[None]
    dg = jnp.where(same, a_low, 0.0)
    off = a_low - dg
    d1 = _split(dg)
    d2 = _split(_bdot_split(d1, d1))
    d4 = _split(_bdot_split(d2, d2))
    d8 = _split(_bdot_split(d4, d4))
    t0 = eye - dg
    for dp in (d2, d4, d8):
        t0 = t0 + _bdot_split(_split(t0), dp)
    t0 = _split(t0)
    g1 = _split(_bdot_split(t0, _split(off)))
    x = _bdot_split(t0, _split(rhs))
    nblk = csz // sub
    powers = []
    gp = g1
    k = 2
    while k < nblk:
        gp = _split(_bdot_split(gp, gp))
        powers.append(gp)
        k *= 2
    for gp in powers:
        x = x + _bdot_split(gp, _split(x))
    return x - _bdot_split(g1, _split(x))


def _gdn_kernel(qkv_ref, z_ref, ba_ref, cw_ref, alog_ref, dtb_ref, gain_ref, conv0_ref, s0_ref,
                o_ref, sout_ref, xbuf, s_ref, *, csz, nb):
    c = pl.program_id(1)
    nh = nb * GDN_HEADS

    @pl.when(c == 0)
    def _():
        xbuf[:, 0:8, :] = conv0_ref[...]
        s_ref[...] = s0_ref[...].reshape(nh, GDN_DK, GDN_DK)

    xbuf[:, 8:8 + csz, :] = qkv_ref[...]

    r = lax.broadcasted_iota(jnp.int32, (csz, csz), 0)
    cc = lax.broadcasted_iota(jnp.int32, (csz, csz), 1)
    causal = r >= cc
    strict = r > cc
    gain = gain_ref[...]
    beta, gc, gct = [], [], []
    for bb in range(nb):
        ba = ba_ref[bb]
        beta.append(_sigmoid(ba))
        xs = ba + dtb_ref[...]
        softplus = jnp.maximum(xs, 0.0) + jnp.log(1.0 + jnp.exp(-jnp.abs(xs)))
        logg = -jnp.exp(alog_ref[...]) * softplus
        gc.append(_dot_hi(causal.astype(F32), logg))
        gct.append(gc[bb].T)

    def conv_silu(bb, col):
        y = cw_ref[3:4, col:col + GDN_DK] * xbuf[bb, 8:8 + csz, col:col + GDN_DK]
        for j in range(3):
            y = y + cw_ref[j:j + 1, col:col + GDN_DK] * xbuf[bb, 5 + j:5 + j + csz, col:col + GDN_DK]
        return _silu(y)

    heads = [(bb, h) for bb in range(nb) for h in range(GDN_HEADS)]
    q = jnp.stack([conv_silu(bb, h * GDN_DK) for bb, h in heads])
    k = jnp.stack([conv_silu(bb, (GDN_HEADS + h) * GDN_DK) for bb, h in heads])
    v = jnp.stack([conv_silu(bb, (2 * GDN_HEADS + h) * GDN_DK) for bb, h in heads])
    q = q * lax.rsqrt(jnp.sum(q * q, axis=-1, keepdims=True) + 1e-6) * (GDN_DK ** -0.5)
    k = k * lax.rsqrt(jnp.sum(k * k, axis=-1, keepdims=True) + 1e-6)
    bh = jnp.stack([beta[bb][:, h:h + 1] for bb, h in heads])
    gch = jnp.stack([gc[bb][:, 8 + h:9 + h] for bb, h in heads])
    grow = jnp.stack([gct[bb][8 + h:9 + h, :] for bb, h in heads])
    decay = jnp.where(causal[None], jnp.exp(gch - grow), 0.0)
    kb = k.astype(BF16)
    kk = _bdot_nt(kb, kb)
    qk = _bdot_nt(q.astype(BF16), kb)
    a_low = jnp.where(strict[None], bh * kk * decay, 0.0)
    eg = jnp.exp(gch)
    rhs = jnp.concatenate([v * bh, k * (bh * eg)], axis=-1)
    sol = _unit_lower_solve(a_low, rhs, csz)
    u = sol[:, :, :GDN_DK]
    w = sol[:, :, GDN_DK:]
    s_old = s_ref[...]
    sb = s_old.astype(BF16)
    v_new = u - _bdot(w.astype(BF16), sb)
    vb = v_new.astype(BF16)
    o = _bdot((q * eg).astype(BF16), sb) + _bdot((qk * decay).astype(BF16), vb)
    g_last = gch[:, csz - 1:csz, :]
    k2 = (k * jnp.exp(g_last - gch)).astype(BF16)
    on = _rms_scale(o) * gain
    for n, (bb, h) in enumerate(heads):
        s_ref[n] = s_old[n] * jnp.exp(g_last[n]) + jnp.dot(k2[n].T, vb[n], preferred_element_type=F32)
        zz = z_ref[bb, :, h * GDN_DK:(h + 1) * GDN_DK]
        o_ref[bb, :, h * GDN_DK:(h + 1) * GDN_DK] = (on[n] * _silu(zz)).astype(BF16)

    xbuf[:, 0:8, :] = xbuf[:, csz:csz + 8, :]

    @pl.when(c == pl.num_programs(1) - 1)
    def _():
        sout_ref[...] = s_ref[...].reshape(nb, GDN_HEADS, GDN_DK, GDN_DK)


GDN_BATCHES_PER_STEP = 4


def gdn(qkv, z, ba, conv_w, a_log, dt_bias, out_gain, conv0, s0, bsz, tb, csz, first_block):
    nblk = tb // csz
    nb = GDN_BATCHES_PER_STEP
    qk_w = 3 * GDN_HEADS * GDN_DK
    o_w = GDN_HEADS * GDN_DK

    def rows(b, c):
        return (b, (c + first_block) % nblk, 0)

    def lane_pad(vec):
        return jnp.zeros((1, 128), F32).at[0, 8:16].set(vec.astype(F32))

    conv0p = jnp.concatenate([jnp.zeros((bsz, 8 - (CONV_W - 1), qk_w), F32), conv0.astype(F32)], axis=1)
    state = pl.BlockSpec((nb, GDN_HEADS, GDN_DK, GDN_DK), lambda b, c: (b, 0, 0, 0))
    o, s_new = pl.pallas_call(
        functools.partial(_gdn_kernel, csz=csz, nb=nb),
        grid=(bsz // nb, nblk),
        in_specs=[pl.BlockSpec((nb, csz, qk_w), rows), pl.BlockSpec((nb, csz, o_w), rows),
                  pl.BlockSpec((nb, csz, 128), rows), _const_spec((CONV_W, qk_w)), _const_spec((1, 128)),
                  _const_spec((1, 128)), _const_spec((1, GDN_DK)),
                  pl.BlockSpec((nb, 8, qk_w), lambda b, c: (b, 0, 0)), state],
        out_specs=[pl.BlockSpec((nb, csz, o_w), rows), state],
        out_shape=[jax.ShapeDtypeStruct((bsz, tb, o_w), BF16),
                   jax.ShapeDtypeStruct((bsz, GDN_HEADS, GDN_DK, GDN_DK), F32)],
        scratch_shapes=[pltpu.VMEM((nb, 8 + csz, qk_w), F32), pltpu.VMEM((nb * GDN_HEADS, GDN_DK, GDN_DK), F32)],
        compiler_params=_cparams(("parallel", "arbitrary")),
        name="gdn",
    )(qkv.reshape(bsz, tb, qk_w), z.reshape(bsz, tb, o_w), ba.reshape(bsz, tb, 128), conv_w.astype(F32),
      lane_pad(a_log), lane_pad(dt_bias), out_gain.reshape(1, GDN_DK).astype(F32), conv0p, s0.astype(F32))
    return o.reshape(bsz * tb, o_w), s_new


S5_CHUNKS = 4
S5_CH_U = SSM_WIDTH // S5_CHUNKS
S5_CH_P = S5_CH_U // SSM_GROUP * SSM_STATE


def _s5_kernel(u_ref, wb_ref, wc_ref, a_ref, d_ref, gw_ref, gb_ref, st0_ref, o_ref, stout_ref, xs, st_ref,
               *, tt, bsz):
    c = pl.program_id(0)

    @pl.when(c == 0)
    def _():
        st_ref[...] = st0_ref[...]

    rows = tt * bsz
    u = u_ref[...].reshape(rows, SSM_WIDTH)
    ub = u.astype(BF16)
    for j in range(S5_CHUNKS):
        xs[j] = jnp.dot(ub[:, j * S5_CH_U:(j + 1) * S5_CH_U], wb_ref[j], preferred_element_type=F32)

    for j in range(S5_CHUNKS):
        ar = jnp.broadcast_to(a_ref[j, 0:1, :], (bsz, S5_CH_P))
        ai = jnp.broadcast_to(a_ref[j, 1:2, :], (bsz, S5_CH_P))

        def step(t, carry, j=j, ar=ar, ai=ai):
            re, im = carry
            row = pl.multiple_of(t * bsz, bsz)
            nre = ar * re - ai * im + xs[j, pl.ds(row, bsz), 0:S5_CH_P]
            nim = ar * im + ai * re + xs[j, pl.ds(row, bsz), S5_CH_P:2 * S5_CH_P]
            xs[j, pl.ds(row, bsz), 0:S5_CH_P] = nre
            xs[j, pl.ds(row, bsz), S5_CH_P:2 * S5_CH_P] = nim
            return nre, nim

        re, im = lax.fori_loop(0, tt, step, (st_ref[j, :, 0:S5_CH_P], st_ref[j, :, S5_CH_P:2 * S5_CH_P]))
        st_ref[j, :, 0:S5_CH_P] = re
        st_ref[j, :, S5_CH_P:2 * S5_CH_P] = im

    ys = [jnp.dot(xs[j].astype(BF16), wc_ref[j], preferred_element_type=F32) for j in range(S5_CHUNKS)]
    y = jnp.concatenate(ys, axis=-1) + u * d_ref[...]
    hg = 0.5 * y * (1.0 + jnp.tanh(math.sqrt(2.0 / math.pi) * (y + 0.044715 * (y * y * y))))
    gate = _sigmoid(jnp.dot(hg.astype(BF16), gw_ref[...], preferred_element_type=F32) + gb_ref[...])
    o_ref[...] = (hg * gate).reshape(tt, bsz, SSM_WIDTH)

    @pl.when(c == pl.num_programs(0) - 1)
    def _():
        stout_ref[...] = st_ref[...]


def s5(u_tb, re0, im0, lam_re, lam_im, log_dt, b_re, b_im, c_re, c_im, d_skip, glu_w, glu_b,
       bsz, tb, tt, first_tile):
    ntile = tb // tt
    lr = jnp.minimum(lam_re.astype(F32), -1e-4)
    li = lam_im.astype(F32)
    dt = jnp.exp(log_dt.astype(F32))[:, None]
    mag = jnp.exp(lr * dt)
    ar, ai = mag * jnp.cos(li * dt), mag * jnp.sin(li * dt)
    den = lr * lr + li * li
    nr, ni = ar - 1.0, ai
    cr, ci = (nr * lr + ni * li) / den, (ni * lr - nr * li) / den
    br, bi = b_re.astype(F32), b_im.astype(F32)
    bbr = cr[..., None] * br - ci[..., None] * bi
    bbi = cr[..., None] * bi + ci[..., None] * br
    gpc = S5_CH_U // SSM_GROUP

    def block_diag_in(bb):
        bb = bb.reshape(S5_CHUNKS, gpc, SSM_STATE, SSM_GROUP)
        eye = jnp.eye(gpc, dtype=F32)
        return jnp.einsum("jgpm,gh->jgmhp", bb, eye).reshape(S5_CHUNKS, S5_CH_U, S5_CH_P)

    def block_diag_out(cm):
        cm = cm.reshape(S5_CHUNKS, gpc, SSM_GROUP, SSM_STATE)
        eye = jnp.eye(gpc, dtype=F32)
        return jnp.einsum("jgmp,gh->jgphm", cm, eye).reshape(S5_CHUNKS, S5_CH_P, S5_CH_U)

    wb = jnp.concatenate([block_diag_in(bbr), block_diag_in(bbi)], axis=-1).astype(BF16)
    wc = jnp.concatenate([block_diag_out(c_re.astype(F32)), -block_diag_out(c_im.astype(F32))],
                         axis=1).astype(BF16)
    a_pack = jnp.stack([ar.reshape(S5_CHUNKS, S5_CH_P), ai.reshape(S5_CHUNKS, S5_CH_P)], axis=1)
    st0 = jnp.concatenate([re0.astype(F32).reshape(bsz, S5_CHUNKS, S5_CH_P),
                           im0.astype(F32).reshape(bsz, S5_CHUNKS, S5_CH_P)], axis=-1).transpose(1, 0, 2)

    def tile(c):
        return ((c + first_tile) % ntile, 0, 0)

    out, st = pl.pallas_call(
        functools.partial(_s5_kernel, tt=tt, bsz=bsz),
        grid=(ntile,),
        in_specs=[pl.BlockSpec((tt, bsz, SSM_WIDTH), tile), _const_spec(wb.shape), _const_spec(wc.shape),
                  _const_spec(a_pack.shape), _const_spec((1, SSM_WIDTH)), _const_spec((SSM_WIDTH, SSM_WIDTH)),
                  _const_spec((1, SSM_WIDTH)), _const_spec(st0.shape)],
        out_specs=[pl.BlockSpec((tt, bsz, SSM_WIDTH), tile), _const_spec(st0.shape)],
        out_shape=[jax.ShapeDtypeStruct((tb, bsz, SSM_WIDTH), F32), jax.ShapeDtypeStruct(st0.shape, F32)],
        scratch_shapes=[pltpu.VMEM((S5_CHUNKS, tt * bsz, 2 * S5_CH_P), F32),
                        pltpu.VMEM((S5_CHUNKS, bsz, 2 * S5_CH_P), F32)],
        compiler_params=_cparams(("arbitrary",)),
        name="s5",
    )(u_tb, wb, wc, a_pack, d_skip.reshape(1, SSM_WIDTH).astype(F32), glu_w.astype(BF16),
      glu_b.reshape(1, SSM_WIDTH).astype(F32), st0)
    st = st.transpose(1, 0, 2)
    re_new = st[..., :S5_CH_P].reshape(bsz, SSM_GROUPS, SSM_STATE)
    im_new = st[..., S5_CH_P:].reshape(bsz, SSM_GROUPS, SSM_STATE)
    return out, re_new, im_new


EVEN_OUTS = ((3 * GDN_HEADS * GDN_DK, "flat"), (GDN_HEADS * GDN_DK, "flat"), (128, "flat"), (SSM_WIDTH, "tmajor"))


def pack_even_w_in(w_in):
    qkv_w = 3 * GDN_HEADS * GDN_DK
    z_w = GDN_HEADS * GDN_DK
    c0 = qkv_w + z_w
    ba = jnp.zeros((D_MODEL, 128), w_in.dtype).at[:, :2 * GDN_HEADS].set(w_in[:, c0:c0 + 2 * GDN_HEADS])
    return jnp.concatenate([w_in[:, :c0], ba, w_in[:, c0 + 2 * GDN_HEADS:]], axis=1).astype(BF16)


def even_mixer(x, gain, p, conv0, s0, re0, im0, bsz, tb, tm, csz, first_block, tt):
    qkv, z, ba, u_t = inproj(x, gain, p["w_in"], EVEN_OUTS, bsz, tb, tm)
    o_gdn, s_new = gdn(qkv, z, ba, p["conv_w"], p["a_log"], p["dt_bias"], p["out_gain"], conv0, s0,
                       bsz, tb, csz, first_block)
    o_s5, re_new, im_new = s5(u_t.reshape(tb, bsz, SSM_WIDTH), re0, im0, p["lam_re"], p["lam_im"], p["log_dt"],
                              p["b_re"], p["b_im"], p["c_re"], p["c_im"], p["d"], p["glu_w"], p["glu_b"],
                              bsz, tb, tt, first_block * csz // tt)
    mixed = [(o_gdn, GDN_HEADS * GDN_DK, "flat"), (o_s5.reshape(tb, bsz * SSM_WIDTH), SSM_WIDTH, "tmajor")]
    return mixed, qkv, s_new, re_new, im_new


def _gate_kernel(f_ref, bias_ref, c0_ref, logf_ref, cum_ref, tot_ref, *, segments, log_sigmoid):
    carry = c0_ref[...]
    for start, size, pad_rows in segments:
        x = f_ref[start:start + size, :]
        if log_sigmoid:
            x = x + bias_ref[...]
            x = jnp.minimum(x, 0.0) - jnp.log(1.0 + jnp.exp(-jnp.abs(x)))
        if pad_rows:
            x = jnp.where(lax.broadcasted_iota(jnp.int32, x.shape, 0) >= pad_rows, x, 0.0)
        r = lax.broadcasted_iota(jnp.int32, (size, size), 0)
        cc = lax.broadcasted_iota(jnp.int32, (size, size), 1)
        cum = _dot_hi((r >= cc).astype(F32), x) + carry
        logf_ref[start:start + size, :] = x
        cum_ref[start:start + size, :] = cum * LOG2E
        carry = cum[size - 1:size, :]
    tot_ref[...] = carry


def gate_cumsum(f, bias, carry0, bsz, tb, segments, log_sigmoid):
    seq = pl.BlockSpec((None, tb, 128), lambda b: (b, 0, 0))
    one = pl.BlockSpec((None, 1, 128), lambda b: (b, 0, 0))
    return pl.pallas_call(
        functools.partial(_gate_kernel, segments=tuple(segments), log_sigmoid=log_sigmoid),
        grid=(bsz,),
        in_specs=[seq, _const_spec((1, 128)), one],
        out_specs=[seq, seq, one],
        out_shape=[jax.ShapeDtypeStruct((bsz, tb, 128), F32), jax.ShapeDtypeStruct((bsz, tb, 128), F32),
                   jax.ShapeDtypeStruct((bsz, 1, 128), F32)],
        compiler_params=_cparams(("parallel",)),
        name="gate_cumsum",
    )(f, bias, carry0)


N_BIAS_TERMS = 3


def _head_padded(x, scale, other):
    lane = lax.broadcasted_iota(jnp.int32, (x.shape[0], 128), 1)
    parts = []
    for h in range(x.shape[1] // HEAD_DIM):
        g = x[:, (h // 2) * 128:(h // 2 + 1) * 128] * scale
        own = (lane >= HEAD_DIM) == bool(h % 2)
        parts.append(jnp.where(own, g, other(h, lane)).astype(BF16))
    return jnp.concatenate(parts, axis=-1)


def _rotary(x, c_ref, s1_ref, s2_ref):
    width = x.shape[-1]
    return (x * c_ref[...] + pltpu.roll(x, width - ROT_DIM // 2, 1) * s1_ref[...]
            + pltpu.roll(x, ROT_DIM // 2, 1) * s2_ref[...])


def _prep_q_kernel(qf_ref, qd_ref, c_ref, s1_ref, s2_ref, qfa_ref, qda_ref):
    qscale = HEAD_DIM ** -0.5 * LOG2E
    ones = lambda h, lane: jnp.where(lane % HEAD_DIM < N_BIAS_TERMS, 1.0, 0.0)
    qfa_ref[...] = _head_padded(qf_ref[...], qscale, ones)
    qda_ref[...] = _head_padded(_rotary(qd_ref[...], c_ref, s1_ref, s2_ref), qscale, lambda h, lane: 0.0)


def _prep_kv_kernel(*refs, rotate):
    if rotate:
        kf_ref, vf_ref, kd_ref, vd_ref, cum_ref, c_ref, s1_ref, s2_ref, kfa_ref, vfa_ref, kdb_ref, vda_ref, kdr_ref = refs
        kd = _rotary(kd_ref[...], c_ref, s1_ref, s2_ref)
        kdr_ref[...] = kd
    else:
        kf_ref, vf_ref, kd_ref, vd_ref, cum_ref, kfa_ref, vfa_ref, kdb_ref, vda_ref = refs
        kd = kd_ref[...]
    cum = cum_ref[...]

    def offset_pieces(h, lane):
        c = jnp.broadcast_to(cum[:, h:h + 1], lane.shape)
        hi = c.astype(BF16).astype(F32)
        mid = (c - hi).astype(BF16).astype(F32)
        lo = c - hi - mid
        sel = lane % HEAD_DIM
        return jnp.where(sel == 0, -hi, jnp.where(sel == 1, -mid, jnp.where(sel == 2, -lo, 0.0)))

    kfa_ref[...] = _head_padded(kf_ref[...], 1.0, offset_pieces)
    vfa_ref[...] = _head_padded(vf_ref[...], 1.0, lambda h, lane: 1.0)
    kdb_ref[...] = kd.astype(BF16)
    vd = vd_ref[...].astype(BF16)
    ones = jnp.ones((vd.shape[0], DIFF_V), BF16)
    vda_ref[...] = jnp.concatenate(
        [a for hd in range(DIFF_HEADS) for a in (vd[:, hd * DIFF_V:(hd + 1) * DIFF_V], ones)], axis=-1)


def _prep_kv_cached_kernel(kft_ref, vft_ref, kdt_ref, vd_ref, cum_ref, kfa_ref, vfa_ref, kdb_ref, vda_ref):
    rows = kft_ref.shape[-1]
    cumt = cum_ref[...].T
    sub = lax.broadcasted_iota(jnp.int32, (HEAD_DIM, rows), 0)
    ones = jnp.ones((HEAD_DIM, rows), F32)
    for h in range(FOX_HEADS):
        c = jnp.broadcast_to(cumt[h:h + 1, :], (HEAD_DIM, rows))
        hi = c.astype(BF16).astype(F32)
        mid = (c - hi).astype(BF16).astype(F32)
        lo = c - hi - mid
        pieces = jnp.where(sub == 0, -hi, jnp.where(sub == 1, -mid, jnp.where(sub == 2, -lo, 0.0)))
        pair_k = (kft_ref[h], pieces) if h % 2 == 0 else (pieces, kft_ref[h])
        pair_v = (vft_ref[h], ones) if h % 2 == 0 else (ones, vft_ref[h])
        kfa_ref[:, h * 128:(h + 1) * 128] = jnp.concatenate(pair_k, axis=0).T.astype(BF16)
        vfa_ref[:, h * 128:(h + 1) * 128] = jnp.concatenate(pair_v, axis=0).T.astype(BF16)
    for g in range(DIFF_HEADS):
        kdb_ref[:, g * 128:(g + 1) * 128] = jnp.concatenate([kdt_ref[2 * g], kdt_ref[2 * g + 1]],
                                                            axis=0).T.astype(BF16)
    vd = vd_ref[...].astype(BF16)
    ones_v = jnp.ones((rows, DIFF_V), BF16)
    vda_ref[...] = jnp.concatenate(
        [a for hd in range(DIFF_HEADS) for a in (vd[:, hd * DIFF_V:(hd + 1) * DIFF_V], ones_v)], axis=-1)


def attn_prep_kv_cached(kft, vft, kdt, vd, cum2, layer, bsz, tb, tm):
    width = vd.shape[-1]
    ts = pl.BlockSpec((None, None, FOX_HEADS, HEAD_DIM, tm), lambda i, b: (layer, b, 0, 0, i))
    xs_in = pl.BlockSpec((None, None, tm, width), lambda i, b: (layer, b, i, 0))
    xs = pl.BlockSpec((None, tm, width), lambda i, b: (b, i, 0))
    ws = pl.BlockSpec((None, tm, 2 * width), lambda i, b: (b, i, 0))
    cs = pl.BlockSpec((None, tm, 128), lambda i, b: (b, i, 0))
    shp = lambda w: jax.ShapeDtypeStruct((bsz, tb, w), BF16)
    return pl.pallas_call(
        _prep_kv_cached_kernel,
        grid=(tb // tm, bsz),
        in_specs=[ts, ts, ts, xs_in, cs],
        out_specs=[ws, ws, xs, ws],
        out_shape=[shp(2 * width), shp(2 * width), shp(width), shp(2 * width)],
        compiler_params=_cparams(("parallel", "parallel")),
        name="attn_prep_kv_cached",
    )(kft, vft, kdt, vd, cum2)


def attn_prep_q(qf, qd, tables, bsz, tb, tm):
    width = qf.shape[-1]
    xs = pl.BlockSpec((None, tm, width), lambda i, b: (b, i, 0))
    qs = pl.BlockSpec((None, tm, 2 * width), lambda i, b: (b, i, 0))
    ts = pl.BlockSpec((tm, width), lambda i, b: (i, 0))
    shp = jax.ShapeDtypeStruct((bsz, tb, 2 * width), BF16)
    return pl.pallas_call(
        _prep_q_kernel,
        grid=(tb // tm, bsz),
        in_specs=[xs, xs, ts, ts, ts],
        out_specs=[qs, qs],
        out_shape=[shp, shp],
        compiler_params=_cparams(("parallel", "parallel")),
        name="attn_prep_q",
    )(qf, qd, *tables)


def attn_prep_kv(kf, vf, kd, vd, cum2, tables, bsz, tb, tm):
    width = kf.shape[-1]
    rotate = tables is not None
    xs = pl.BlockSpec((None, tm, width), lambda i, b: (b, i, 0))
    ws = pl.BlockSpec((None, tm, 2 * width), lambda i, b: (b, i, 0))
    cs = pl.BlockSpec((None, tm, 128), lambda i, b: (b, i, 0))
    ts = pl.BlockSpec((tm, width), lambda i, b: (i, 0))
    shp = lambda w, dt: jax.ShapeDtypeStruct((bsz, tb, w), dt)
    outs = pl.pallas_call(
        functools.partial(_prep_kv_kernel, rotate=rotate),
        grid=(tb // tm, bsz),
        in_specs=[xs] * 4 + [cs] + ([ts] * 3 if rotate else []),
        out_specs=[ws, ws, xs, ws] + ([xs] if rotate else []),
        out_shape=[shp(2 * width, BF16), shp(2 * width, BF16), shp(width, BF16), shp(2 * width, BF16)]
        + ([shp(width, F32)] if rotate else []),
        compiler_params=_cparams(("parallel", "parallel")),
        name="attn_prep_kv",
    )(kf, vf, kd, vd, cum2, *(tables if rotate else ()))
    return outs if rotate else list(outs) + [None]


def rope_tables(pos, nheads):
    half = ROT_DIM // 2
    inv_freq = ROPE_THETA ** (-jnp.arange(0, ROT_DIM, 2, dtype=F32) / ROT_DIM)
    ang = pos.astype(F32)[:, None] * inv_freq[None, :]
    cos, sin = jnp.cos(ang), jnp.sin(ang)
    t = pos.shape[0]
    rest = jnp.zeros((t, HEAD_DIM - ROT_DIM), F32)
    z = jnp.zeros((t, half), F32)
    ctab = jnp.concatenate([cos, cos, rest + 1.0], axis=1)
    s1 = jnp.concatenate([-sin, z, rest], axis=1)
    s2 = jnp.concatenate([z, sin, rest], axis=1)
    return tuple(jnp.tile(a, (1, nheads)) for a in (ctab, s1, s2))


NEG_INF = -1e30
N_MAPS = FOX_HEADS + 2 * DIFF_HEADS


def _attn_kernel(*refs, tq, tk, te, has_main, main_causal, fox_extra, diff_extra, zero_pad_q, pad_rows,
                 out_scale):
    if has_main:
        (qf_ref, qd_ref, cq_ref, kf_ref, vf_ref, kd_ref, vd_ref,
         kfe_ref, vfe_ref, kde_ref, vde_ref, lam_ref, gain_ref,
         o_ref, m_ref, cqb_ref, accf_ref, accd_ref) = refs
    else:
        (qf_ref, qd_ref, cq_ref, kfe_ref, vfe_ref, kde_ref, vde_ref, lam_ref, gain_ref,
         o_ref, m_ref, cqb_ref, accf_ref, accd_ref) = refs
    i = pl.program_id(1)
    j = pl.program_id(2)
    neg = NEG_INF * LOG2E

    def lanes(x, ncols):
        reps, rem = divmod(ncols, 128)
        return jnp.concatenate([x] * reps + ([x[:, :rem]] if rem else []), axis=-1)

    def one_map(idx, q, k, v, acc_ref, slot, cq, mask, qr):
        t = _dot_nt(q, k)
        nrows, ncols = t.shape
        if mask is not None:
            t = jnp.where(mask, t, neg)
        mt = jnp.broadcast_to(jnp.max(t, axis=-1, keepdims=True), (nrows, 128))
        m_prev = m_ref[idx, qr]
        if cq is None:
            m_new = jnp.maximum(m_prev, mt)
            shift = m_new
        else:
            m_new = jnp.maximum(m_prev, mt + cq)
            shift = m_new - cq
        p = jnp.exp2(t - lanes(shift, ncols))
        alpha = jnp.exp2(m_prev - m_new)
        m_ref[idx, qr] = m_new
        acc_ref[slot, qr] = (lanes(alpha, acc_ref.shape[-1]) * acc_ref[slot, qr]
                             + jnp.dot(p.astype(BF16), v, preferred_element_type=F32))

    def process(k_tiles, fox_mask, diff_mask, qr=slice(None)):
        def rows(which, lo, hi):
            parts = [tile[which][:n, lo:hi] for tile, n in k_tiles]
            return parts[0] if len(parts) == 1 else jnp.concatenate(parts, axis=0)

        for h in range(FOX_HEADS):
            lo, hi = h * 128, (h + 1) * 128
            one_map(h, qf_ref[qr, lo:hi], rows(0, lo, hi), rows(1, lo, hi), accf_ref, h, cqb_ref[h, qr], fox_mask, qr)
        for hm in range(2 * DIFF_HEADS):
            g = hm // 2
            one_map(FOX_HEADS + hm, qd_ref[qr, hm * 128:(hm + 1) * 128], rows(2, g * 128, (g + 1) * 128),
                    rows(3, g * 2 * DIFF_V, (g + 1) * 2 * DIFF_V), accd_ref, hm, None, diff_mask, qr)

    def visible(extra_kind, main_kind, width, r0=0, nrows=tq):
        if extra_kind == "none" and main_kind == "none":
            return None
        r = lax.broadcasted_iota(jnp.int32, (nrows, width), 0) + r0
        c = lax.broadcasted_iota(jnp.int32, (nrows, width), 1)
        cm = c - te
        rules = {"valid": c >= pad_rows, "causal": c <= r,
                 "causal_valid": jnp.logical_and(c <= r, c >= pad_rows),
                 "main_causal": cm <= r,
                 "main_block_causal": cm < (lax.shift_right_logical(r, 6) + 1) * BLOCK}
        in_extra = c < te
        if main_kind == "none":
            return jnp.logical_or(c >= te, rules[extra_kind])
        if extra_kind == "none":
            return jnp.logical_or(in_extra, rules[main_kind])
        return jnp.logical_or(jnp.logical_and(in_extra, rules[extra_kind]),
                              jnp.logical_and(c >= te, rules[main_kind]))

    @pl.when(j == 0)
    def _():
        m_ref[...] = jnp.full(m_ref.shape, NEG_INF, F32)
        accf_ref[...] = jnp.zeros(accf_ref.shape, F32)
        accd_ref[...] = jnp.zeros(accd_ref.shape, F32)
        for h in range(FOX_HEADS):
            cqb_ref[h] = jnp.broadcast_to(cq_ref[:, h:h + 1], (tq, 128))

    extra_tile = (kfe_ref, vfe_ref, kde_ref, vde_ref)
    if has_main:
        main_tile = (kf_ref, vf_ref, kd_ref, vd_ref)
        if main_causal:
            assert tq == tk
            last, before_last = j == i, j < i
            fox_main, diff_main = "main_causal", "main_block_causal"
        else:
            last, before_last = j == pl.num_programs(2) - 1, j < pl.num_programs(2) - 1
            fox_main = diff_main = "none"

        @pl.when(before_last)
        def _():
            process([(main_tile, tk)], None, None)

        @pl.when(last)
        def _():
            process([(extra_tile, te), (main_tile, tk)], visible(fox_extra, fox_main, te + tk),
                    visible(diff_extra, diff_main, te + tk))
    else:
        last = j == 0
        process([(extra_tile, te)], visible(fox_extra, "none", te), visible(diff_extra, "none", te))

    @pl.when(last)
    def _():
        lam = lam_ref[...]
        upper = lax.broadcasted_iota(jnp.int32, (tq, 128), 1) >= HEAD_DIM
        outs = []
        for g in range(FOX_HEADS // 2):
            a0, a1 = accf_ref[2 * g], accf_ref[2 * g + 1]
            outs.append(jnp.where(upper, a1 / pltpu.roll(a1, HEAD_DIM, 1), a0 / pltpu.roll(a0, HEAD_DIM, 1)))
        for hd in range(DIFF_HEADS):
            a1, a2 = accd_ref[2 * hd], accd_ref[2 * hd + 1]
            o = a1[:, :DIFF_V] / a1[:, DIFF_V:] - lam * (a2[:, :DIFF_V] / a2[:, DIFF_V:])
            outs.append(_rms_scale(o) * gain_ref[...] * out_scale)
        out = jnp.concatenate(outs, axis=-1)
        if zero_pad_q:
            out = jnp.where(lax.broadcasted_iota(jnp.int32, out.shape, 0) >= pad_rows, out, 0.0)
        o_ref[...] = out.astype(BF16)


def attention(q_arrays, q_rows, main, extra, lam, gain, out_scale, bsz, tq, nq, tk, te,
              main_causal, fox_extra, diff_extra, zero_pad_q, q_block0=0, pad_rows=BLOCK - N_META):
    qf, qd, cumq = q_arrays
    has_main = main is not None
    nk = main[0].shape[1] // tk if has_main else 1
    kv_widths = (FOX_HEADS * 128, FOX_HEADS * 128, 2 * DIFF_HEADS * HEAD_DIM, DIFF_HEADS * 2 * DIFF_V)

    def qspec(width):
        return pl.BlockSpec((None, tq, width), lambda b, i, j: (b, i + q_block0, 0))

    in_specs = [qspec(FOX_HEADS * 128), qspec(2 * DIFF_HEADS * 128), qspec(128)]
    args = [qf, qd, cumq]
    if has_main:
        if main_causal:
            kmap = lambda b, i, j: (b, jnp.minimum(j, i), 0)
        else:
            kmap = lambda b, i, j: (b, j, 0)
        in_specs += [pl.BlockSpec((None, tk, w), kmap) for w in kv_widths]
        args += list(main)
    eblk = extra[4]
    emap = lambda b, i, j: (b, eblk, 0)
    in_specs += [pl.BlockSpec((None, te, w), emap) for w in kv_widths] + [_const_spec((1, 1)), _const_spec((1, DIFF_V))]
    args += list(extra[:4]) + [lam.reshape(1, 1).astype(F32), gain.reshape(1, DIFF_V).astype(F32)]
    width = FOX_HEADS * HEAD_DIM + DIFF_HEADS * DIFF_V
    return pl.pallas_call(
        functools.partial(_attn_kernel, tq=tq, tk=tk, te=te, has_main=has_main, main_causal=main_causal,
                          fox_extra=fox_extra, diff_extra=diff_extra, zero_pad_q=zero_pad_q, pad_rows=pad_rows,
                          out_scale=out_scale),
        grid=(bsz, nq, nk),
        in_specs=in_specs,
        out_specs=pl.BlockSpec((None, tq, width), lambda b, i, j: (b, i, 0)),
        out_shape=jax.ShapeDtypeStruct((bsz, nq * tq, width), BF16),
        scratch_shapes=[pltpu.VMEM((N_MAPS, tq, 128), F32), pltpu.VMEM((FOX_HEADS, tq, 128), F32),
                        pltpu.VMEM((FOX_HEADS, tq, 128), F32), pltpu.VMEM((2 * DIFF_HEADS, tq, 2 * DIFF_V), F32)],
        compiler_params=_cparams(("parallel", "parallel", "arbitrary")),
        name="attention",
    )(*args)


ODD_OUTS = ((512, "flat"),) * 6 + ((128, "flat"),)


def pack_odd_w_in(w_in):
    c0 = 3 * FOX_HEADS * HEAD_DIM
    f = jnp.zeros((D_MODEL, 128), w_in.dtype).at[:, :FOX_HEADS].set(w_in[:, c0:c0 + FOX_HEADS])
    return jnp.concatenate([w_in[:, :c0], w_in[:, c0 + FOX_HEADS:], f], axis=1).astype(BF16)


def diff_lambda_value(diff_lambda, lam_init):
    lq1, lk1, lq2, lk2 = diff_lambda.astype(F32)
    return jnp.exp(jnp.sum(lq1 * lk1)) - jnp.exp(jnp.sum(lq2 * lk2)) + lam_init


def _logical_kernel(x_ref, *refs, seq):
    o_ref = refs[-1]
    o_ref[0:N_META, :] = x_ref[x_ref.shape[0] - N_META:, :]
    o_ref[N_META:, :] = x_ref[0:seq, :]


def slab_to_logical(x, seq, layer, nlayers, stacked=None):
    bsz, tb, width = x.shape
    x_spec = pl.BlockSpec((None, tb, width), lambda b: (b, 0, 0))
    return pl.pallas_call(
        functools.partial(_logical_kernel, seq=seq),
        grid=(bsz,),
        in_specs=[x_spec] if stacked is None else [x_spec, pl.BlockSpec(memory_space=pl.ANY)],
        out_specs=pl.BlockSpec((None, None, N_META + seq, width), lambda b: (layer, b, 0, 0)),
        out_shape=jax.ShapeDtypeStruct((nlayers, bsz, N_META + seq, width), x.dtype),
        input_output_aliases={} if stacked is None else {1: 0},
        compiler_params=_cparams(("parallel",)),
        name="slab_to_logical",
    )(*([x] if stacked is None else [x, stacked]))


def odd_mixer_prompt(x, gain, p, lam_init, bsz, tb, nframes, tm, tq):
    qf, kf, vf, qd, kd, vd, f = [a.reshape(bsz, tb, -1) for a in inproj(x, gain, p["w_in"], ODD_OUTS, bsz, tb, tm)]
    meta_blk = tb // BLOCK - 1
    pad = BLOCK - N_META
    segments = [(nframes, BLOCK, pad)] + [(s, tq, 0) for s in range(0, nframes, tq)]
    logf, cum, _ = gate_cumsum(f, p["f_bias"], jnp.zeros((bsz, 1, 128), F32), bsz, tb, segments, True)
    pos = jnp.concatenate([N_META + jnp.arange(nframes, dtype=jnp.int32), jnp.zeros((pad,), jnp.int32),
                           jnp.arange(N_META, dtype=jnp.int32)])
    tables = rope_tables(pos, 2 * DIFF_HEADS)
    qfa, qda = attn_prep_q(qf, qd, tables, bsz, tb, tm)
    kfa, vfa, kdb, vda, kd = attn_prep_kv(kf, vf, kd, vd, cum, tables, bsz, tb, tm)
    nk = nframes // tq
    lam = diff_lambda_value(p["diff_lambda"], lam_init)
    extra = (kfa, vfa, kdb, vda, meta_blk)
    common = dict(lam=lam, gain=p["diff_gain"], out_scale=1.0 - lam_init, bsz=bsz, te=BLOCK, diff_extra="valid")
    frames = attention((qfa, qda, cum), tb, (kfa, vfa, kdb, vda), extra, tq=tq, nq=nk, tk=tq,
                       main_causal=True, fox_extra="valid", zero_pad_q=False, **common)
    meta = attention((qfa, qda, cum), tb, None, extra, tq=BLOCK, nq=1, tk=BLOCK, main_causal=False,
                     fox_extra="causal_valid", zero_pad_q=True, q_block0=meta_blk, **common)
    mixed = jnp.concatenate([frames, meta], axis=1).reshape(bsz * tb, -1)
    return mixed, kf, vf, logf, kd, vd


def odd_mixer_sample(x, gain, p, lam_init, layer, kf_cache, vf_cache, lf_cache, kd_cache, vd_cache, bsz, tb, tk):
    qf, kf, vf, qd, kd, vd, f = [a.reshape(bsz, tb, -1) for a in inproj(x, gain, p["w_in"], ODD_OUTS, bsz, tb, tb)]
    past = kf_cache.shape[2]
    lf128 = jnp.zeros((bsz, past, 128), F32).at[:, :, :FOX_HEADS].set(lf_cache[layer].astype(F32))
    zero_c = jnp.zeros((bsz, 1, 128), F32)
    _, cum_past, total = gate_cumsum(lf128, p["f_bias"], zero_c, bsz, past, [(s, tk, 0) for s in range(0, past, tk)],
                                     False)
    logf, cum, _ = gate_cumsum(f, p["f_bias"], total, bsz, tb, [(0, tb, 0)], True)
    tables = rope_tables(past + jnp.arange(tb, dtype=jnp.int32), 2 * DIFF_HEADS)
    qfa, qda = attn_prep_q(qf, qd, tables, bsz, tb, tb)
    kfa, vfa, kdb, vda, kd = attn_prep_kv(kf, vf, kd, vd, cum, tables, bsz, tb, tb)
    tview = lambda a: jnp.transpose(a.astype(F32), (0, 1, 3, 4, 2))
    vd_flat = vd_cache.astype(F32).reshape(vd_cache.shape[0], bsz, past, -1)
    main = tuple(attn_prep_kv_cached(tview(kf_cache), tview(vf_cache), tview(kd_cache), vd_flat, cum_past, layer,
                                     bsz, past, tk))
    extra = (kfa, vfa, kdb, vda, 0)
    mixed = attention((qfa, qda, cum), tb, main, extra, lam=diff_lambda_value(p["diff_lambda"], lam_init),
                      gain=p["diff_gain"], out_scale=1.0 - lam_init, bsz=bsz, tq=tb, nq=1, tk=tk, te=tb,
                      main_causal=False, fox_extra="causal", diff_extra="none", zero_pad_q=False)
    return mixed.reshape(bsz * tb, -1), kf, vf, logf, kd, vd


def kernel(x_prompt, x_sample, state_conv, state_delta, state_ssm_re, state_ssm_im, cache_fox_k, cache_fox_v,
           cache_fox_logf, cache_diff_k, cache_diff_v, meta_tokens, norm_mix, norm_ffn, norm_final, w_in_even,
           w_out_even, conv_w, gdn_a_log, gdn_dt_bias, gdn_out_norm, ssm_lambda_re, ssm_lambda_im, ssm_log_dt,
           ssm_b_re, ssm_b_im, ssm_c_re, ssm_c_im, ssm_d, ssm_glu_w, ssm_glu_b, w_in_odd, w_out_odd, fox_f_bias,
           diff_lambda, diff_out_norm, ffn_w1, ffn_w3, ffn_w2):
    bp, seq, _ = x_prompt.shape
    bs, ls, _ = x_sample.shape
    depth = norm_mix.shape[0]
    tb = seq + BLOCK
    pad = BLOCK - N_META
    tm_p = 832
    tq = 512
    meta = jnp.broadcast_to(meta_tokens.astype(F32)[None], (bp, N_META, D_MODEL))
    xp = jnp.concatenate([x_prompt.astype(F32), jnp.zeros((bp, pad, D_MODEL), F32), meta], axis=1)
    xp = xp.reshape(bp * tb, D_MODEL)
    xs = x_sample.astype(F32).reshape(bs * ls, D_MODEL)

    def logical(a):
        return jnp.concatenate([a[:, tb - N_META:], a[:, :seq]], axis=1)

    outs = {k: [] for k in ("conv_p", "conv_s", "delta_p", "delta_s", "re_p", "re_s", "im_p", "im_s", "fk_p", "fk_s",
                            "fv_p", "fv_s", "fl_p", "fl_s", "dk_p", "dk_s", "dv_p", "dv_s")}
    head_split = {"fk_p": (FOX_HEADS, HEAD_DIM), "fv_p": (FOX_HEADS, HEAD_DIM), "dk_p": (2 * DIFF_HEADS, HEAD_DIM),
                  "dv_p": (DIFF_HEADS, DIFF_V)}
    nchunk = FFN_HIDDEN // FFN_CHUNK
    for l in range(depth):
        i = l // 2
        if l % 2 == 0:
            pe = dict(w_in=pack_even_w_in(w_in_even[i]), conv_w=conv_w[i], a_log=gdn_a_log[i], dt_bias=gdn_dt_bias[i],
                      out_gain=gdn_out_norm[i], lam_re=ssm_lambda_re[i], lam_im=ssm_lambda_im[i],
                      log_dt=ssm_log_dt[i], b_re=ssm_b_re[i], b_im=ssm_b_im[i], c_re=ssm_c_re[i], c_im=ssm_c_im[i],
                      d=ssm_d[i], glu_w=ssm_glu_w[i], glu_b=ssm_glu_b[i])
            qkv_w = 3 * GDN_HEADS * GDN_DK
            mixed_p, qkv_p, sd_p, re_p, im_p = even_mixer(
                xp, norm_mix[l], pe, jnp.zeros((bp, CONV_W - 1, qkv_w), F32),
                jnp.zeros((bp, GDN_HEADS, GDN_DK, GDN_DK), F32), jnp.zeros((bp, SSM_GROUPS, SSM_STATE), F32),
                jnp.zeros((bp, SSM_GROUPS, SSM_STATE), F32), bp, tb, tm_p, BLOCK, tb // BLOCK - 1, 32)
            mixed_s, qkv_s, sd_s, re_s, im_s = even_mixer(
                xs, norm_mix[l], pe, state_conv[i], state_delta[i], state_ssm_re[i], state_ssm_im[i],
                bs, ls, ls, ls, 0, ls)
            outs["conv_p"].append(qkv_p.reshape(bp, tb, qkv_w)[:, seq - (CONV_W - 1):seq])
            outs["conv_s"].append(jnp.concatenate([state_conv[i].astype(F32), qkv_s.reshape(bs, ls, qkv_w)],
                                                  axis=1)[:, ls:])
            outs["delta_p"].append(sd_p); outs["delta_s"].append(sd_s)
            outs["re_p"].append(re_p); outs["re_s"].append(re_s)
            outs["im_p"].append(im_p); outs["im_s"].append(im_s)
            w_out = w_out_even[i]
        else:
            lam_init = 0.8 - 0.6 * math.exp(-0.3 * l)
            po = dict(w_in=pack_odd_w_in(w_in_odd[i]), diff_lambda=diff_lambda[i], diff_gain=diff_out_norm[i],
                      f_bias=jnp.zeros((1, 128), F32).at[0, :FOX_HEADS].set(fox_f_bias[i].astype(F32)))
            mixed_p, kf, vf, lf, kd, vd = odd_mixer_prompt(xp, norm_mix[l], po, lam_init, bp, tb, seq, tm_p, tq)
            mixed_p = [(mixed_p, mixed_p.shape[-1], "flat")]
            for key, slab in (("fk_p", kf), ("fv_p", vf), ("dk_p", kd), ("dv_p", vd)):
                outs[key].append(slab_to_logical(slab, seq, 0, 1).reshape((bp, N_META + seq) + head_split[key]))
            outs["fl_p"].append(logical(lf)[:, :, :FOX_HEADS])
            mixed_s, kf, vf, lf, kd, vd = odd_mixer_sample(
                xs, norm_mix[l], po, lam_init, i, cache_fox_k, cache_fox_v, cache_fox_logf, cache_diff_k,
                cache_diff_v, bs, ls, 512)
            mixed_s = [(mixed_s, mixed_s.shape[-1], "flat")]
            outs["fk_s"].append(kf.reshape(bs, ls, FOX_HEADS, HEAD_DIM))
            outs["fv_s"].append(vf.reshape(bs, ls, FOX_HEADS, HEAD_DIM))
            outs["fl_s"].append(lf[:, :, :FOX_HEADS])
            outs["dk_s"].append(kd.reshape(bs, ls, 2 * DIFF_HEADS, HEAD_DIM))
            outs["dv_s"].append(vd.reshape(bs, ls, DIFF_HEADS, DIFF_V))
            w_out = w_out_odd[i]
        w1 = ffn_w1[l].astype(BF16).reshape(D_MODEL, nchunk, FFN_CHUNK).transpose(1, 0, 2)
        w3 = ffn_w3[l].astype(BF16).reshape(D_MODEL, nchunk, FFN_CHUNK).transpose(1, 0, 2)
        w2 = ffn_w2[l].astype(BF16).reshape(nchunk, FFN_CHUNK, D_MODEL)
        xp = outproj_ffn(xp, mixed_p, w_out.astype(BF16), norm_ffn[l], w1, w3, w2, bp, tb, tm_p)
        xs = outproj_ffn(xs, mixed_s, w_out.astype(BF16), norm_ffn[l], w1, w3, w2, bs, ls, ls)

    y_prompt = final_norm(xp, norm_final, bp, tb, seq, 512).reshape(bp, seq, D_MODEL)
    y_sample = final_norm(xs, norm_final, bs, ls, ls, ls).reshape(bs, ls, D_MODEL)
    order = ("conv_p", "conv_s", "delta_p", "delta_s", "re_p", "re_s", "im_p", "im_s", "fk_p", "fk_s", "fv_p", "fv_s",
             "fl_p", "fl_s", "dk_p", "dk_s", "dv_p", "dv_s")
    return (y_prompt, y_sample) + tuple(jnp.stack(outs[k]) for k in order)
```

```python
import functools
import math

import jax
import jax.numpy as jnp
import numpy as np
from jax import lax
from jax.experimental import pallas as pl
from jax.experimental.pallas import tpu as pltpu

F32 = jnp.float32
BF16 = jnp.bfloat16
HIGHEST = lax.Precision.HIGHEST
LOG2E = math.log2(math.e)

D_MODEL = 1024
N_META = 16
RMS_EPS = 1e-6
HEAD_DIM = 64
GDN_HEADS = 8
GDN_DK = 128
CONV_W = 4
SSM_WIDTH = 512
SSM_GROUP = 16
SSM_GROUPS = 32
SSM_STATE = 64
FOX_HEADS = 8
DIFF_HEADS = 4
DIFF_V = 128
ROT_DIM = 16
ROPE_THETA = 500000.0
FFN_HIDDEN = 2816
FFN_CHUNK = 256
BLOCK = 64
VMEM_LIMIT = 56 * 1024 * 1024


def _cparams(sem):
    return pltpu.CompilerParams(dimension_semantics=sem, vmem_limit_bytes=VMEM_LIMIT)


def _const_spec(shape, single_buffer=False):
    nd = len(shape)
    if single_buffer:
        return pl.BlockSpec(shape, lambda *_: (0,) * nd, pipeline_mode=pl.Buffered(1))
    return pl.BlockSpec(shape, lambda *_: (0,) * nd)


def _sigmoid(x):
    return 1.0 / (1.0 + jnp.exp(-x))


def _silu(x):
    return x * _sigmoid(x)


def _rms_scale(x):
    return x * lax.rsqrt(jnp.mean(x * x, axis=-1, keepdims=True) + RMS_EPS)


def _row_spec(kind, tm, width, nt):
    if kind == "flat":
        return pl.BlockSpec((tm, width), lambda b, i: (b * nt + i, 0))
    return pl.BlockSpec((tm, width), lambda b, i: (i, b))


def _row_shape(kind, bsz, tb, width, dtype):
    if kind == "flat":
        return jax.ShapeDtypeStruct((bsz * tb, width), dtype)
    return jax.ShapeDtypeStruct((tb, bsz * width), dtype)


def _inproj_kernel(x_ref, g_ref, w_ref, *o_refs, widths):
    xn = (_rms_scale(x_ref[...]) * g_ref[...]).astype(BF16)
    off = 0
    for o_ref, wd in zip(o_refs, widths):
        for c0 in range(0, wd, 512):
            cw = min(512, wd - c0)
            o_ref[:, c0:c0 + cw] = jnp.dot(xn, w_ref[:, off + c0:off + c0 + cw],
                                           preferred_element_type=F32)
        off += wd


def inproj(x, gain, w, outs, bsz, tb, tm):
    nt = tb // tm
    widths = tuple(wd for wd, _ in outs)
    return pl.pallas_call(
        functools.partial(_inproj_kernel, widths=widths),
        grid=(bsz, nt),
        in_specs=[_row_spec("flat", tm, D_MODEL, nt), _const_spec((1, D_MODEL)), _const_spec(w.shape, True)],
        out_specs=[_row_spec(kind, tm, wd, nt) for wd, kind in outs],
        out_shape=[_row_shape(kind, bsz, tb, wd, F32) for wd, kind in outs],
        compiler_params=_cparams(("parallel", "parallel")),
        name="inproj",
    )(x, gain.reshape(1, D_MODEL), w)


def _outffn_kernel(res_ref, *refs, a_widths, nc):
    na = len(a_widths)
    a_refs = refs[:na]
    wout_ref, g_ref, w1_ref, w3_ref, w2_ref, o_ref, acc_ref, xn_ref = refs[na:]
    x1 = res_ref[...]
    off = 0
    for a_ref, wd in zip(a_refs, a_widths):
        x1 = x1 + jnp.dot(a_ref[...].astype(BF16), wout_ref[off:off + wd, :], preferred_element_type=F32)
        off += wd
    xn_ref[...] = (_rms_scale(x1) * g_ref[...]).astype(BF16)
    acc_ref[...] = x1

    def body(c, carry):
        xn = xn_ref[...]
        h1 = jnp.dot(xn, w1_ref[c], preferred_element_type=F32)
        h3 = jnp.dot(xn, w3_ref[c], preferred_element_type=F32)
        act = (_silu(h1) * h3).astype(BF16)
        acc_ref[...] += jnp.dot(act, w2_ref[c], preferred_element_type=F32)
        return carry

    lax.fori_loop(0, nc, body, 0)
    o_ref[...] = acc_ref[...]


def outproj_ffn(res, mixed, w_out, gain, w1, w3, w2, bsz, tb, tm):
    nt = tb // tm
    nc = w1.shape[0]
    a_widths = tuple(wd for _, wd, _ in mixed)
    row = _row_spec("flat", tm, D_MODEL, nt)
    return pl.pallas_call(
        functools.partial(_outffn_kernel, a_widths=a_widths, nc=nc),
        grid=(bsz, nt),
        in_specs=[row] + [_row_spec(kind, tm, wd, nt) for _, wd, kind in mixed]
        + [_const_spec(w_out.shape, True), _const_spec((1, D_MODEL)), _const_spec(w1.shape, True),
           _const_spec(w3.shape, True), _const_spec(w2.shape, True)],
        out_specs=row,
        out_shape=jax.ShapeDtypeStruct(res.shape, F32),
        scratch_shapes=[pltpu.VMEM((tm, D_MODEL), F32), pltpu.VMEM((tm, D_MODEL), BF16)],
        compiler_params=_cparams(("parallel", "parallel")),
        name="outproj_ffn",
    )(res, *[a for a, _, _ in mixed], w_out, gain.reshape(1, D_MODEL), w1, w3, w2)


def _final_norm_kernel(x_ref, g_ref, o_ref):
    o_ref[...] = _rms_scale(x_ref[...]) * g_ref[...]


def final_norm(x, gain, bsz, tb, t_out, tm):
    spec = pl.BlockSpec((None, tm, D_MODEL), lambda b, i: (b, i, 0))
    return pl.pallas_call(
        _final_norm_kernel,
        grid=(bsz, t_out // tm),
        in_specs=[spec, _const_spec((1, D_MODEL))],
        out_specs=spec,
        out_shape=jax.ShapeDtypeStruct((bsz, t_out, D_MODEL), F32),
        compiler_params=_cparams(("parallel", "parallel")),
        name="final_norm",
    )(x.reshape(bsz, tb, D_MODEL), gain.reshape(1, D_MODEL))


def _dot_nt(a, b):
    return lax.dot_general(a, b, (((1,), (1,)), ((), ())), preferred_element_type=F32)


def _dot_hi(a, b):
    return jnp.dot(a, b, precision=HIGHEST, preferred_element_type=F32)


def _bdot(a, b):
    return lax.dot_general(a, b, (((2,), (1,)), ((0,), (0,))), preferred_element_type=F32)


def _bdot_nt(a, b):
    return lax.dot_general(a, b, (((2,), (2,)), ((0,), (0,))), preferred_element_type=F32)


def _split(a):
    hi = a.astype(BF16)
    return hi, (a - hi.astype(F32)).astype(BF16)


def _bdot_split(a, b):
    return _bdot(a[0], b[0]) + _bdot(a[1], b[0]) + _bdot(a[0], b[1])


def _unit_lower_solve(a_low, rhs, csz):
    sub = 16
    r = lax.broadcasted_iota(jnp.int32, (csz, csz), 0)
    c = lax.broadcasted_iota(jnp.int32, (csz, csz), 1)
    same = ((r // sub) == (c // sub))[None]
    eye = (r == c).astype(F32)[None]
    dg = jnp.where(same, a_low, 0.0)
    off = a_low - dg
    d1 = _split(dg)
    d2 = _split(_bdot_split(d1, d1))
    d4 = _split(_bdot_split(d2, d2))
    d8 = _split(_bdot_split(d4, d4))
    t0 = eye - dg
    for dp in (d2, d4, d8):
        t0 = t0 + _bdot_split(_split(t0), dp)
    t0 = _split(t0)
    g1 = _split(_bdot_split(t0, _split(off)))
    x = _bdot_split(t0, _split(rhs))
    nblk = csz // sub
    powers = []
    gp = g1
    k = 2
    while k < nblk:
        gp = _split(_bdot_split(gp, gp))
        powers.append(gp)
        k *= 2
    for gp in powers:
        x = x + _bdot_split(gp, _split(x))
    return x - _bdot_split(g1, _split(x))


def _gdn_kernel(qkv_ref, z_ref, ba_ref, cw_ref, alog_ref, dtb_ref, gain_ref, conv0_ref, s0_ref,
                o_ref, sout_ref, xbuf, s_ref, *, csz, nb):
    c = pl.program_id(1)
    nh = nb * GDN_HEADS

    @pl.when(c == 0)
    def _():
        xbuf[:, 0:8, :] = conv0_ref[...]
        s_ref[...] = s0_ref[...].reshape(nh, GDN_DK, GDN_DK)

    xbuf[:, 8:8 + csz, :] = qkv_ref[...]

    r = lax.broadcasted_iota(jnp.int32, (csz, csz), 0)
    cc = lax.broadcasted_iota(jnp.int32, (csz, csz), 1)
    causal = r >= cc
    strict = r > cc
    gain = gain_ref[...]
    beta, gc, gct = [], [], []
    for bb in range(nb):
        ba = ba_ref[bb]
        beta.append(_sigmoid(ba))
        xs = ba + dtb_ref[...]
        softplus = jnp.maximum(xs, 0.0) + jnp.log(1.0 + jnp.exp(-jnp.abs(xs)))
        logg = -jnp.exp(alog_ref[...]) * softplus
        gc.append(_dot_hi(causal.astype(F32), logg))
        gct.append(gc[bb].T)

    def conv_silu(bb, col):
        y = cw_ref[3:4, col:col + GDN_DK] * xbuf[bb, 8:8 + csz, col:col + GDN_DK]
        for j in range(3):
            y = y + cw_ref[j:j + 1, col:col + GDN_DK] * xbuf[bb, 5 + j:5 + j + csz, col:col + GDN_DK]
        return _silu(y)

    heads = [(bb, h) for bb in range(nb) for h in range(GDN_HEADS)]
    q = jnp.stack([conv_silu(bb, h * GDN_DK) for bb, h in heads])
    k = jnp.stack([conv_silu(bb, (GDN_HEADS + h) * GDN_DK) for bb, h in heads])
    v = jnp.stack([conv_silu(bb, (2 * GDN_HEADS + h) * GDN_DK) for bb, h in heads])
    q = q * lax.rsqrt(jnp.sum(q * q, axis=-1, keepdims=True) + 1e-6) * (GDN_DK ** -0.5)
    k = k * lax.rsqrt(jnp.sum(k * k, axis=-1, keepdims=True) + 1e-6)
    bh = jnp.stack([beta[bb][:, h:h + 1] for bb, h in heads])
    gch = jnp.stack([gc[bb][:, 8 + h:9 + h] for bb, h in heads])
    grow = jnp.stack([gct[bb][8 + h:9 + h, :] for bb, h in heads])
    decay = jnp.where(causal[None], jnp.exp(gch - grow), 0.0)
    kb = k.astype(BF16)
    kk = _bdot_nt(kb, kb)
    qk = _bdot_nt(q.astype(BF16), kb)
    a_low = jnp.where(strict[None], bh * kk * decay, 0.0)
    eg = jnp.exp(gch)
    rhs = jnp.concatenate([v * bh, k * (bh * eg)], axis=-1)
    sol = _unit_lower_solve(a_low, rhs, csz)
    u = sol[:, :, :GDN_DK]
    w = sol[:, :, GDN_DK:]
    s_old = s_ref[...]
    sb = s_old.astype(BF16)
    v_new = u - _bdot(w.astype(BF16), sb)
    vb = v_new.astype(BF16)
    o = _bdot((q * eg).astype(BF16), sb) + _bdot((qk * decay).astype(BF16), vb)
    g_last = gch[:, csz - 1:csz, :]
    k2 = (k * jnp.exp(g_last - gch)).astype(BF16)
    on = _rms_scale(o) * gain
    for n, (bb, h) in enumerate(heads):
        s_ref[n] = s_old[n] * jnp.exp(g_last[n]) + jnp.dot(k2[n].T, vb[n], preferred_element_type=F32)
        zz = z_ref[bb, :, h * GDN_DK:(h + 1) * GDN_DK]
        o_ref[bb, :, h * GDN_DK:(h + 1) * GDN_DK] = (on[n] * _silu(zz)).astype(BF16)

    xbuf[:, 0:8, :] = xbuf[:, csz:csz + 8, :]

    @pl.when(c == pl.num_programs(1) - 1)
    def _():
        sout_ref[...] = s_ref[...].reshape(nb, GDN_HEADS, GDN_DK, GDN_DK)


GDN_BATCHES_PER_STEP = 4


def gdn(qkv, z, ba, conv_w, a_log, dt_bias, out_gain, conv0, s0, bsz, tb, csz, first_block):
    nblk = tb // csz
    nb = GDN_BATCHES_PER_STEP
    qk_w = 3 * GDN_HEADS * GDN_DK
    o_w = GDN_HEADS * GDN_DK

    def rows(b, c):
        return (b, (c + first_block) % nblk, 0)

    def lane_pad(vec):
        return jnp.zeros((1, 128), F32).at[0, 8:16].set(vec.astype(F32))

    conv0p = jnp.concatenate([jnp.zeros((bsz, 8 - (CONV_W - 1), qk_w), F32), conv0.astype(F32)], axis=1)
    state = pl.BlockSpec((nb, GDN_HEADS, GDN_DK, GDN_DK), lambda b, c: (b, 0, 0, 0))
    o, s_new = pl.pallas_call(
        functools.partial(_gdn_kernel, csz=csz, nb=nb),
        grid=(bsz // nb, nblk),
        in_specs=[pl.BlockSpec((nb, csz, qk_w), rows), pl.BlockSpec((nb, csz, o_w), rows),
                  pl.BlockSpec((nb, csz, 128), rows), _const_spec((CONV_W, qk_w)), _const_spec((1, 128)),
                  _const_spec((1, 128)), _const_spec((1, GDN_DK)),
                  pl.BlockSpec((nb, 8, qk_w), lambda b, c: (b, 0, 0)), state],
        out_specs=[pl.BlockSpec((nb, csz, o_w), rows), state],
        out_shape=[jax.ShapeDtypeStruct((bsz, tb, o_w), BF16),
                   jax.ShapeDtypeStruct((bsz, GDN_HEADS, GDN_DK, GDN_DK), F32)],
        scratch_shapes=[pltpu.VMEM((nb, 8 + csz, qk_w), F32), pltpu.VMEM((nb * GDN_HEADS, GDN_DK, GDN_DK), F32)],
        compiler_params=_cparams(("parallel", "arbitrary")),
        name="gdn",
    )(qkv.reshape(bsz, tb, qk_w), z.reshape(bsz, tb, o_w), ba.reshape(bsz, tb, 128), conv_w.astype(F32),
      lane_pad(a_log), lane_pad(dt_bias), out_gain.reshape(1, GDN_DK).astype(F32), conv0p, s0.astype(F32))
    return o.reshape(bsz * tb, o_w), s_new


S5_CHUNKS = 4
S5_CH_U = SSM_WIDTH // S5_CHUNKS
S5_CH_P = S5_CH_U // SSM_GROUP * SSM_STATE


def _s5_kernel(u_ref, wb_ref, wc_ref, a_ref, d_ref, gw_ref, gb_ref, st0_ref, o_ref, stout_ref, xs, st_ref,
               *, tt, bsz):
    c = pl.program_id(0)

    @pl.when(c == 0)
    def _():
        st_ref[...] = st0_ref[...]

    rows = tt * bsz
    u = u_ref[...].reshape(rows, SSM_WIDTH)
    ub = u.astype(BF16)
    for j in range(S5_CHUNKS):
        xs[j] = jnp.dot(ub[:, j * S5_CH_U:(j + 1) * S5_CH_U], wb_ref[j], preferred_element_type=F32)

    for j in range(S5_CHUNKS):
        ar = jnp.broadcast_to(a_ref[j, 0:1, :], (bsz, S5_CH_P))
        ai = jnp.broadcast_to(a_ref[j, 1:2, :], (bsz, S5_CH_P))

        def step(t, carry, j=j, ar=ar, ai=ai):
            re, im = carry
            row = pl.multiple_of(t * bsz, bsz)
            nre = ar * re - ai * im + xs[j, pl.ds(row, bsz), 0:S5_CH_P]
            nim = ar * im + ai * re + xs[j, pl.ds(row, bsz), S5_CH_P:2 * S5_CH_P]
            xs[j, pl.ds(row, bsz), 0:S5_CH_P] = nre
            xs[j, pl.ds(row, bsz), S5_CH_P:2 * S5_CH_P] = nim
            return nre, nim

        re, im = lax.fori_loop(0, tt, step, (st_ref[j, :, 0:S5_CH_P], st_ref[j, :, S5_CH_P:2 * S5_CH_P]))
        st_ref[j, :, 0:S5_CH_P] = re
        st_ref[j, :, S5_CH_P:2 * S5_CH_P] = im

    ys = [jnp.dot(xs[j].astype(BF16), wc_ref[j], preferred_element_type=F32) for j in range(S5_CHUNKS)]
    y = jnp.concatenate(ys, axis=-1) + u * d_ref[...]
    hg = 0.5 * y * (1.0 + jnp.tanh(math.sqrt(2.0 / math.pi) * (y + 0.044715 * (y * y * y))))
    gate = _sigmoid(jnp.dot(hg.astype(BF16), gw_ref[...], preferred_element_type=F32) + gb_ref[...])
    o_ref[...] = (hg * gate).reshape(tt, bsz, SSM_WIDTH)

    @pl.when(c == pl.num_programs(0) - 1)
    def _():
        stout_ref[...] = st_ref[...]


def s5(u_tb, re0, im0, lam_re, lam_im, log_dt, b_re, b_im, c_re, c_im, d_skip, glu_w, glu_b,
       bsz, tb, tt, first_tile):
    ntile = tb // tt
    lr = jnp.minimum(lam_re.astype(F32), -1e-4)
    li = lam_im.astype(F32)
    dt = jnp.exp(log_dt.astype(F32))[:, None]
    mag = jnp.exp(lr * dt)
    ar, ai = mag * jnp.cos(li * dt), mag * jnp.sin(li * dt)
    den = lr * lr + li * li
    nr, ni = ar - 1.0, ai
    cr, ci = (nr * lr + ni * li) / den, (ni * lr - nr * li) / den
    br, bi = b_re.astype(F32), b_im.astype(F32)
    bbr = cr[..., None] * br - ci[..., None] * bi
    bbi = cr[..., None] * bi + ci[..., None] * br
    gpc = S5_CH_U // SSM_GROUP

    def block_diag_in(bb):
        bb = bb.reshape(S5_CHUNKS, gpc, SSM_STATE, SSM_GROUP)
        eye = jnp.eye(gpc, dtype=F32)
        return jnp.einsum("jgpm,gh->jgmhp", bb, eye).reshape(S5_CHUNKS, S5_CH_U, S5_CH_P)

    def block_diag_out(cm):
        cm = cm.reshape(S5_CHUNKS, gpc, SSM_GROUP, SSM_STATE)
        eye = jnp.eye(gpc, dtype=F32)
        return jnp.einsum("jgmp,gh->jgphm", cm, eye).reshape(S5_CHUNKS, S5_CH_P, S5_CH_U)

    wb = jnp.concatenate([block_diag_in(bbr), block_diag_in(bbi)], axis=-1).astype(BF16)
    wc = jnp.concatenate([block_diag_out(c_re.astype(F32)), -block_diag_out(c_im.astype(F32))],
                         axis=1).astype(BF16)
    a_pack = jnp.stack([ar.reshape(S5_CHUNKS, S5_CH_P), ai.reshape(S5_CHUNKS, S5_CH_P)], axis=1)
    st0 = jnp.concatenate([re0.astype(F32).reshape(bsz, S5_CHUNKS, S5_CH_P),
                           im0.astype(F32).reshape(bsz, S5_CHUNKS, S5_CH_P)], axis=-1).transpose(1, 0, 2)

    def tile(c):
        return ((c + first_tile) % ntile, 0, 0)

    out, st = pl.pallas_call(
        functools.partial(_s5_kernel, tt=tt, bsz=bsz),
        grid=(ntile,),
        in_specs=[pl.BlockSpec((tt, bsz, SSM_WIDTH), tile), _const_spec(wb.shape), _const_spec(wc.shape),
                  _const_spec(a_pack.shape), _const_spec((1, SSM_WIDTH)), _const_spec((SSM_WIDTH, SSM_WIDTH)),
                  _const_spec((1, SSM_WIDTH)), _const_spec(st0.shape)],
        out_specs=[pl.BlockSpec((tt, bsz, SSM_WIDTH), tile), _const_spec(st0.shape)],
        out_shape=[jax.ShapeDtypeStruct((tb, bsz, SSM_WIDTH), F32), jax.ShapeDtypeStruct(st0.shape, F32)],
        scratch_shapes=[pltpu.VMEM((S5_CHUNKS, tt * bsz, 2 * S5_CH_P), F32),
                        pltpu.VMEM((S5_CHUNKS, bsz, 2 * S5_CH_P), F32)],
        compiler_params=_cparams(("arbitrary",)),
        name="s5",
    )(u_tb, wb, wc, a_pack, d_skip.reshape(1, SSM_WIDTH).astype(F32), glu_w.astype(BF16),
      glu_b.reshape(1, SSM_WIDTH).astype(F32), st0)
    st = st.transpose(1, 0, 2)
    re_new = st[..., :S5_CH_P].reshape(bsz, SSM_GROUPS, SSM_STATE)
    im_new = st[..., S5_CH_P:].reshape(bsz, SSM_GROUPS, SSM_STATE)
    return out, re_new, im_new


EVEN_OUTS = ((3 * GDN_HEADS * GDN_DK, "flat"), (GDN_HEADS * GDN_DK, "flat"), (128, "flat"), (SSM_WIDTH, "tmajor"))


def pack_even_w_in(w_in):
    qkv_w = 3 * GDN_HEADS * GDN_DK
    z_w = GDN_HEADS * GDN_DK
    c0 = qkv_w + z_w
    ba = jnp.zeros((D_MODEL, 128), w_in.dtype).at[:, :2 * GDN_HEADS].set(w_in[:, c0:c0 + 2 * GDN_HEADS])
    return jnp.concatenate([w_in[:, :c0], ba, w_in[:, c0 + 2 * GDN_HEADS:]], axis=1).astype(BF16)


def even_mixer(x, gain, p, conv0, s0, re0, im0, bsz, tb, tm, csz, first_block, tt):
    qkv, z, ba, u_t = inproj(x, gain, p["w_in"], EVEN_OUTS, bsz, tb, tm)
    o_gdn, s_new = gdn(qkv, z, ba, p["conv_w"], p["a_log"], p["dt_bias"], p["out_gain"], conv0, s0,
                       bsz, tb, csz, first_block)
    o_s5, re_new, im_new = s5(u_t.reshape(tb, bsz, SSM_WIDTH), re0, im0, p["lam_re"], p["lam_im"], p["log_dt"],
                              p["b_re"], p["b_im"], p["c_re"], p["c_im"], p["d"], p["glu_w"], p["glu_b"],
                              bsz, tb, tt, first_block * csz // tt)
    mixed = [(o_gdn, GDN_HEADS * GDN_DK, "flat"), (o_s5.reshape(tb, bsz * SSM_WIDTH), SSM_WIDTH, "tmajor")]
    return mixed, qkv, s_new, re_new, im_new


def _gate_kernel(f_ref, bias_ref, c0_ref, logf_ref, cum_ref, tot_ref, *, segments, log_sigmoid):
    carry = c0_ref[...]
    for start, size, pad_rows in segments:
        x = f_ref[start:start + size, :]
        if log_sigmoid:
            x = x + bias_ref[...]
            x = jnp.minimum(x, 0.0) - jnp.log(1.0 + jnp.exp(-jnp.abs(x)))
        if pad_rows:
            x = jnp.where(lax.broadcasted_iota(jnp.int32, x.shape, 0) >= pad_rows, x, 0.0)
        r = lax.broadcasted_iota(jnp.int32, (size, size), 0)
        cc = lax.broadcasted_iota(jnp.int32, (size, size), 1)
        cum = _dot_hi((r >= cc).astype(F32), x) + carry
        logf_ref[start:start + size, :] = x
        cum_ref[start:start + size, :] = cum * LOG2E
        carry = cum[size - 1:size, :]
    tot_ref[...] = carry


def gate_cumsum(f, bias, carry0, bsz, tb, segments, log_sigmoid):
    seq = pl.BlockSpec((None, tb, 128), lambda b: (b, 0, 0))
    one = pl.BlockSpec((None, 1, 128), lambda b: (b, 0, 0))
    return pl.pallas_call(
        functools.partial(_gate_kernel, segments=tuple(segments), log_sigmoid=log_sigmoid),
        grid=(bsz,),
        in_specs=[seq, _const_spec((1, 128)), one],
        out_specs=[seq, seq, one],
        out_shape=[jax.ShapeDtypeStruct((bsz, tb, 128), F32), jax.ShapeDtypeStruct((bsz, tb, 128), F32),
                   jax.ShapeDtypeStruct((bsz, 1, 128), F32)],
        compiler_params=_cparams(("parallel",)),
        name="gate_cumsum",
    )(f, bias, carry0)


N_BIAS_TERMS = 3


def _head_padded(x, scale, other):
    lane = lax.broadcasted_iota(jnp.int32, (x.shape[0], 128), 1)
    parts = []
    for h in range(x.shape[1] // HEAD_DIM):
        g = x[:, (h // 2) * 128:(h // 2 + 1) * 128] * scale
        own = (lane >= HEAD_DIM) == bool(h % 2)
        parts.append(jnp.where(own, g, other(h, lane)).astype(BF16))
    return jnp.concatenate(parts, axis=-1)


def _rotary(x, c_ref, s1_ref, s2_ref):
    width = x.shape[-1]
    return (x * c_ref[...] + pltpu.roll(x, width - ROT_DIM // 2, 1) * s1_ref[...]
            + pltpu.roll(x, ROT_DIM // 2, 1) * s2_ref[...])


def _prep_kv_kernel(*refs, rotate):
    if rotate:
        kf_ref, vf_ref, kd_ref, vd_ref, cum_ref, c_ref, s1_ref, s2_ref, kfa_ref, vfa_ref, kdb_ref, vda_ref, kdr_ref = refs
        kd = _rotary(kd_ref[...], c_ref, s1_ref, s2_ref)
        kdr_ref[...] = kd
    else:
        kf_ref, vf_ref, kd_ref, vd_ref, cum_ref, kfa_ref, vfa_ref, kdb_ref, vda_ref = refs
        kd = kd_ref[...]
    cum = cum_ref[...]

    def offset_pieces(h, lane):
        c = jnp.broadcast_to(cum[:, h:h + 1], lane.shape)
        hi = c.astype(BF16).astype(F32)
        mid = (c - hi).astype(BF16).astype(F32)
        lo = c - hi - mid
        sel = lane % HEAD_DIM
        return jnp.where(sel == 0, -hi, jnp.where(sel == 1, -mid, jnp.where(sel == 2, -lo, 0.0)))

    kfa_ref[...] = _head_padded(kf_ref[...], 1.0, offset_pieces)
    vfa_ref[...] = _head_padded(vf_ref[...], 1.0, lambda h, lane: 1.0)
    kdb_ref[...] = kd.astype(BF16)
    vd = vd_ref[...].astype(BF16)
    ones = jnp.ones((vd.shape[0], DIFF_V), BF16)
    vda_ref[...] = jnp.concatenate(
        [a for hd in range(DIFF_HEADS) for a in (vd[:, hd * DIFF_V:(hd + 1) * DIFF_V], ones)], axis=-1)


def _prep_kv_cached_kernel(kft_ref, vft_ref, kdt_ref, vd_ref, cum_ref, kfa_ref, vfa_ref, kdb_ref, vda_ref):
    rows = kft_ref.shape[-1]
    cumt = cum_ref[...].T
    sub = lax.broadcasted_iota(jnp.int32, (HEAD_DIM, rows), 0)
    ones = jnp.ones((HEAD_DIM, rows), F32)
    for h in range(FOX_HEADS):
        c = jnp.broadcast_to(cumt[h:h + 1, :], (HEAD_DIM, rows))
        hi = c.astype(BF16).astype(F32)
        mid = (c - hi).astype(BF16).astype(F32)
        lo = c - hi - mid
        pieces = jnp.where(sub == 0, -hi, jnp.where(sub == 1, -mid, jnp.where(sub == 2, -lo, 0.0)))
        pair_k = (kft_ref[h], pieces) if h % 2 == 0 else (pieces, kft_ref[h])
        pair_v = (vft_ref[h], ones) if h % 2 == 0 else (ones, vft_ref[h])
        kfa_ref[:, h * 128:(h + 1) * 128] = jnp.concatenate(pair_k, axis=0).T.astype(BF16)
        vfa_ref[:, h * 128:(h + 1) * 128] = jnp.concatenate(pair_v, axis=0).T.astype(BF16)
    for g in range(DIFF_HEADS):
        kdb_ref[:, g * 128:(g + 1) * 128] = jnp.concatenate([kdt_ref[2 * g], kdt_ref[2 * g + 1]],
                                                            axis=0).T.astype(BF16)
    vd = vd_ref[...].astype(BF16)
    ones_v = jnp.ones((rows, DIFF_V), BF16)
    vda_ref[...] = jnp.concatenate(
        [a for hd in range(DIFF_HEADS) for a in (vd[:, hd * DIFF_V:(hd + 1) * DIFF_V], ones_v)], axis=-1)


def attn_prep_kv_cached(kft, vft, kdt, vd, cum2, layer, bsz, tb, tm):
    width = vd.shape[-1]
    ts = pl.BlockSpec((None, None, FOX_HEADS, HEAD_DIM, tm), lambda i, b: (layer, b, 0, 0, i))
    xs_in = pl.BlockSpec((None, None, tm, width), lambda i, b: (layer, b, i, 0))
    xs = pl.BlockSpec((None, tm, width), lambda i, b: (b, i, 0))
    ws = pl.BlockSpec((None, tm, 2 * width), lambda i, b: (b, i, 0))
    cs = pl.BlockSpec((None, tm, 128), lambda i, b: (b, i, 0))
    shp = lambda w: jax.ShapeDtypeStruct((bsz, tb, w), BF16)
    return pl.pallas_call(
        _prep_kv_cached_kernel,
        grid=(tb // tm, bsz),
        in_specs=[ts, ts, ts, xs_in, cs],
        out_specs=[ws, ws, xs, ws],
        out_shape=[shp(2 * width), shp(2 * width), shp(width), shp(2 * width)],
        compiler_params=_cparams(("parallel", "parallel")),
        name="attn_prep_kv_cached",
    )(kft, vft, kdt, vd, cum2)


def attn_prep_kv(kf, vf, kd, vd, cum2, tables, bsz, tb, tm):
    width = kf.shape[-1]
    rotate = tables is not None
    xs = pl.BlockSpec((None, tm, width), lambda i, b: (b, i, 0))
    ws = pl.BlockSpec((None, tm, 2 * width), lambda i, b: (b, i, 0))
    cs = pl.BlockSpec((None, tm, 128), lambda i, b: (b, i, 0))
    ts = pl.BlockSpec((tm, width), lambda i, b: (i, 0))
    shp = lambda w, dt: jax.ShapeDtypeStruct((bsz, tb, w), dt)
    outs = pl.pallas_call(
        functools.partial(_prep_kv_kernel, rotate=rotate),
        grid=(tb // tm, bsz),
        in_specs=[xs] * 4 + [cs] + ([ts] * 3 if rotate else []),
        out_specs=[ws, ws, xs, ws] + ([xs] if rotate else []),
        out_shape=[shp(2 * width, BF16), shp(2 * width, BF16), shp(width, BF16), shp(2 * width, BF16)]
        + ([shp(width, F32)] if rotate else []),
        compiler_params=_cparams(("parallel", "parallel")),
        name="attn_prep_kv",
    )(kf, vf, kd, vd, cum2, *(tables if rotate else ()))
    return outs if rotate else list(outs) + [None]


def rope_tables(pos, nheads):
    half = ROT_DIM // 2
    inv_freq = ROPE_THETA ** (-jnp.arange(0, ROT_DIM, 2, dtype=F32) / ROT_DIM)
    ang = pos.astype(F32)[:, None] * inv_freq[None, :]
    cos, sin = jnp.cos(ang), jnp.sin(ang)
    t = pos.shape[0]
    rest = jnp.zeros((t, HEAD_DIM - ROT_DIM), F32)
    z = jnp.zeros((t, half), F32)
    ctab = jnp.concatenate([cos, cos, rest + 1.0], axis=1)
    s1 = jnp.concatenate([-sin, z, rest], axis=1)
    s2 = jnp.concatenate([z, sin, rest], axis=1)
    return tuple(jnp.tile(a, (1, nheads)) for a in (ctab, s1, s2))


NEG_INF = -1e30
N_MAPS = FOX_HEADS + 2 * DIFF_HEADS


def _attn_kernel(*refs, tq, tk, te, has_main, main_causal, fox_extra, diff_extra, zero_pad_q, pad_rows,
                 out_scale):
    if has_main:
        (qf_ref, qd_ref, cq_ref, kf_ref, vf_ref, kd_ref, vd_ref,
         kfe_ref, vfe_ref, kde_ref, vde_ref, lam_ref, gain_ref,
         o_ref, m_ref, cqb_ref, accf_ref, accd_ref) = refs
    else:
        (qf_ref, qd_ref, cq_ref, kfe_ref, vfe_ref, kde_ref, vde_ref, lam_ref, gain_ref,
         o_ref, m_ref, cqb_ref, accf_ref, accd_ref) = refs
    i = pl.program_id(1)
    j = pl.program_id(2)
    neg = NEG_INF * LOG2E

    def lanes(x, ncols):
        reps, rem = divmod(ncols, 128)
        return jnp.concatenate([x] * reps + ([x[:, :rem]] if rem else []), axis=-1)

    def one_map(idx, q, k, v, acc_ref, slot, cq, mask, qr):
        t = _dot_nt(q, k)
        nrows, ncols = t.shape
        if mask is not None:
            t = jnp.where(mask, t, neg)
        mt = jnp.broadcast_to(jnp.max(t, axis=-1, keepdims=True), (nrows, 128))
        m_prev = m_ref[idx, qr]
        if cq is None:
            m_new = jnp.maximum(m_prev, mt)
            shift = m_new
        else:
            m_new = jnp.maximum(m_prev, mt + cq)
            shift = m_new - cq
        p = jnp.exp2(t - lanes(shift, ncols))
        alpha = jnp.exp2(m_prev - m_new)
        m_ref[idx, qr] = m_new
        acc_ref[slot, qr] = (lanes(alpha, acc_ref.shape[-1]) * acc_ref[slot, qr]
                             + jnp.dot(p.astype(BF16), v, preferred_element_type=F32))

    def process(k_tiles, fox_mask, diff_mask, qr=slice(None)):
        def rows(which, lo, hi):
            parts = [tile[which][:n, lo:hi] for tile, n in k_tiles]
            return parts[0] if len(parts) == 1 else jnp.concatenate(parts, axis=0)

        for h in range(FOX_HEADS):
            lo, hi = h * 128, (h + 1) * 128
            one_map(h, qf_ref[qr, lo:hi], rows(0, lo, hi), rows(1, lo, hi), accf_ref, h, cqb_ref[h, qr], fox_mask, qr)
        for hm in range(2 * DIFF_HEADS):
            g = hm // 2
            one_map(FOX_HEADS + hm, qd_ref[qr, hm * 128:(hm + 1) * 128], rows(2, g * 128, (g + 1) * 128),
                    rows(3, g * 2 * DIFF_V, (g + 1) * 2 * DIFF_V), accd_ref, hm, None, diff_mask, qr)

    def visible(extra_kind, main_kind, width, r0=0, nrows=tq):
        if extra_kind == "none" and main_kind == "none":
            return None
        r = lax.broadcasted_iota(jnp.int32, (nrows, width), 0) + r0
        c = lax.broadcasted_iota(jnp.int32, (nrows, width), 1)
        cm = c - te
        rules = {"valid": c >= pad_rows, "causal": c <= r,
                 "causal_valid": jnp.logical_and(c <= r, c >= pad_rows),
                 "main_causal": cm <= r,
                 "main_block_causal": cm < (lax.shift_right_logical(r, 6) + 1) * BLOCK}
        in_extra = c < te
        if main_kind == "none":
            return jnp.logical_or(c >= te, rules[extra_kind])
        if extra_kind == "none":
            return jnp.logical_or(in_extra, rules[main_kind])
        return jnp.logical_or(jnp.logical_and(in_extra, rules[extra_kind]),
                              jnp.logical_and(c >= te, rules[main_kind]))

    @pl.when(j == 0)
    def _():
        m_ref[...] = jnp.full(m_ref.shape, NEG_INF, F32)
        accf_ref[...] = jnp.zeros(accf_ref.shape, F32)
        accd_ref[...] = jnp.zeros(accd_ref.shape, F32)
        for h in range(FOX_HEADS):
            cqb_ref[h] = jnp.broadcast_to(cq_ref[:, h:h + 1], (tq, 128))

    extra_tile = (kfe_ref, vfe_ref, kde_ref, vde_ref)
    if has_main:
        main_tile = (kf_ref, vf_ref, kd_ref, vd_ref)
        if main_causal:
            assert tq == tk
            last, before_last = j == i, j < i
            fox_main, diff_main = "main_causal", "main_block_causal"
        else:
            last, before_last = j == pl.num_programs(2) - 1, j < pl.num_programs(2) - 1
            fox_main = diff_main = "none"

        @pl.when(before_last)
        def _():
            process([(main_tile, tk)], None, None)

        @pl.when(last)
        def _():
            process([(extra_tile, te), (main_tile, tk)], visible(fox_extra, fox_main, te + tk),
                    visible(diff_extra, diff_main, te + tk))
    else:
        last = j == 0
        process([(extra_tile, te)], visible(fox_extra, "none", te), visible(diff_extra, "none", te))

    @pl.when(last)
    def _():
        lam = lam_ref[...]
        upper = lax.broadcasted_iota(jnp.int32, (tq, 128), 1) >= HEAD_DIM
        outs = []
        for g in range(FOX_HEADS // 2):
            a0, a1 = accf_ref[2 * g], accf_ref[2 * g + 1]
            outs.append(jnp.where(upper, a1 / pltpu.roll(a1, HEAD_DIM, 1), a0 / pltpu.roll(a0, HEAD_DIM, 1)))
        for hd in range(DIFF_HEADS):
            a1, a2 = accd_ref[2 * hd], accd_ref[2 * hd + 1]
            o = a1[:, :DIFF_V] / a1[:, DIFF_V:] - lam * (a2[:, :DIFF_V] / a2[:, DIFF_V:])
            outs.append(_rms_scale(o) * gain_ref[...] * out_scale)
        out = jnp.concatenate(outs, axis=-1)
        if zero_pad_q:
            out = jnp.where(lax.broadcasted_iota(jnp.int32, out.shape, 0) >= pad_rows, out, 0.0)
        o_ref[...] = out.astype(BF16)


def attention(q_arrays, q_rows, main, extra, lam, gain, out_scale, bsz, tq, nq, tk, te,
              main_causal, fox_extra, diff_extra, zero_pad_q, q_block0=0, pad_rows=BLOCK - N_META):
    qf, qd, cumq = q_arrays
    has_main = main is not None
    nk = main[0].shape[1] // tk if has_main else 1
    kv_widths = (FOX_HEADS * 128, FOX_HEADS * 128, 2 * DIFF_HEADS * HEAD_DIM, DIFF_HEADS * 2 * DIFF_V)

    def qspec(width):
        return pl.BlockSpec((None, tq, width), lambda b, i, j: (b, i + q_block0, 0))

    in_specs = [qspec(FOX_HEADS * 128), qspec(2 * DIFF_HEADS * 128), qspec(128)]
    args = [qf, qd, cumq]
    if has_main:
        if main_causal:
            kmap = lambda b, i, j: (b, jnp.minimum(j, i), 0)
        else:
            kmap = lambda b, i, j: (b, j, 0)
        in_specs += [pl.BlockSpec((None, tk, w), kmap) for w in kv_widths]
        args += list(main)
    eblk = extra[4]
    emap = lambda b, i, j: (b, eblk, 0)
    in_specs += [pl.BlockSpec((None, te, w), emap) for w in kv_widths] + [_const_spec((1, 1)), _const_spec((1, DIFF_V))]
    args += list(extra[:4]) + [lam.reshape(1, 1).astype(F32), gain.reshape(1, DIFF_V).astype(F32)]
    width = FOX_HEADS * HEAD_DIM + DIFF_HEADS * DIFF_V
    return pl.pallas_call(
        functools.partial(_attn_kernel, tq=tq, tk=tk, te=te, has_main=has_main, main_causal=main_causal,
                          fox_extra=fox_extra, diff_extra=diff_extra, zero_pad_q=zero_pad_q, pad_rows=pad_rows,
                          out_scale=out_scale),
        grid=(bsz, nq, nk),
        in_specs=in_specs,
        out_specs=pl.BlockSpec((None, tq, width), lambda b, i, j: (b, i, 0)),
        out_shape=jax.ShapeDtypeStruct((bsz, nq * tq, width), BF16),
        scratch_shapes=[pltpu.VMEM((N_MAPS, tq, 128), F32), pltpu.VMEM((FOX_HEADS, tq, 128), F32),
                        pltpu.VMEM((FOX_HEADS, tq, 128), F32), pltpu.VMEM((2 * DIFF_HEADS, tq, 2 * DIFF_V), F32)],
        compiler_params=_cparams(("parallel", "parallel", "arbitrary")),
        name="attention",
    )(*args)


def _inproj_odd_kernel(x_ref, g_ref, w_ref, c_ref, s1_ref, s2_ref,
                       qfa_ref, kf_ref, vf_ref, qda_ref, kd_ref, vd_ref, f_ref):
    xn = (_rms_scale(x_ref[...]) * g_ref[...]).astype(BF16)
    width = FOX_HEADS * HEAD_DIM
    proj = lambda n: jnp.dot(xn, w_ref[:, n * width:(n + 1) * width], preferred_element_type=F32)
    qscale = HEAD_DIM ** -0.5 * LOG2E
    ones = lambda h, lane: jnp.where(lane % HEAD_DIM < N_BIAS_TERMS, 1.0, 0.0)
    qfa_ref[...] = _head_padded(proj(0), qscale, ones)
    kf_ref[...] = proj(1)
    vf_ref[...] = proj(2)
    qda_ref[...] = _head_padded(_rotary(proj(3), c_ref, s1_ref, s2_ref), qscale, lambda h, lane: 0.0)
    kd_ref[...] = proj(4)
    vd_ref[...] = proj(5)
    f_ref[...] = jnp.dot(xn, w_ref[:, 6 * width:], preferred_element_type=F32)


def inproj_odd(x, gain, w, tables, bsz, tb, tm):
    width = FOX_HEADS * HEAD_DIM
    rows = lambda w_: pl.BlockSpec((None, tm, w_), lambda i, b: (b, i, 0))
    ts = pl.BlockSpec((tm, width), lambda i, b: (i, 0))
    shp = lambda w_, dt: jax.ShapeDtypeStruct((bsz, tb, w_), dt)
    return pl.pallas_call(
        _inproj_odd_kernel,
        grid=(tb // tm, bsz),
        in_specs=[rows(D_MODEL), _const_spec((1, D_MODEL)), _const_spec(w.shape, True), ts, ts, ts],
        out_specs=[rows(2 * width), rows(width), rows(width), rows(2 * width), rows(width), rows(width), rows(128)],
        out_shape=[shp(2 * width, BF16), shp(width, F32), shp(width, F32), shp(2 * width, BF16), shp(width, F32),
                   shp(width, F32), shp(128, F32)],
        compiler_params=_cparams(("parallel", "parallel")),
        name="inproj_odd",
    )(x.reshape(bsz, tb, D_MODEL), gain.reshape(1, D_MODEL), w, *tables)


def pack_odd_w_in(w_in):
    c0 = 3 * FOX_HEADS * HEAD_DIM
    f = jnp.zeros((D_MODEL, 128), w_in.dtype).at[:, :FOX_HEADS].set(w_in[:, c0:c0 + FOX_HEADS])
    return jnp.concatenate([w_in[:, :c0], w_in[:, c0 + FOX_HEADS:], f], axis=1).astype(BF16)


def diff_lambda_value(diff_lambda, lam_init):
    lq1, lk1, lq2, lk2 = diff_lambda.astype(F32)
    return jnp.exp(jnp.sum(lq1 * lk1)) - jnp.exp(jnp.sum(lq2 * lk2)) + lam_init


def _logical_kernel(x_ref, *refs, seq):
    o_ref = refs[-1]
    o_ref[0:N_META, :] = x_ref[x_ref.shape[0] - N_META:, :]
    o_ref[N_META:, :] = x_ref[0:seq, :]


def slab_to_logical(x, seq, layer, nlayers, stacked=None):
    bsz, tb, width = x.shape
    x_spec = pl.BlockSpec((None, tb, width), lambda b: (b, 0, 0))
    return pl.pallas_call(
        functools.partial(_logical_kernel, seq=seq),
        grid=(bsz,),
        in_specs=[x_spec] if stacked is None else [x_spec, pl.BlockSpec(memory_space=pl.ANY)],
        out_specs=pl.BlockSpec((None, None, N_META + seq, width), lambda b: (layer, b, 0, 0)),
        out_shape=jax.ShapeDtypeStruct((nlayers, bsz, N_META + seq, width), x.dtype),
        input_output_aliases={} if stacked is None else {1: 0},
        compiler_params=_cparams(("parallel",)),
        name="slab_to_logical",
    )(*([x] if stacked is None else [x, stacked]))


def odd_mixer_prompt(x, gain, p, lam_init, bsz, tb, nframes, tm, tq):
    meta_blk = tb // BLOCK - 1
    pad = BLOCK - N_META
    pos = jnp.concatenate([N_META + jnp.arange(nframes, dtype=jnp.int32), jnp.zeros((pad,), jnp.int32),
                           jnp.arange(N_META, dtype=jnp.int32)])
    tables = rope_tables(pos, 2 * DIFF_HEADS)
    qfa, kf, vf, qda, kd, vd, f = inproj_odd(x, gain, p["w_in"], tables, bsz, tb, tm)
    segments = [(nframes, BLOCK, pad)] + [(s, tq, 0) for s in range(0, nframes, tq)]
    logf, cum, _ = gate_cumsum(f, p["f_bias"], jnp.zeros((bsz, 1, 128), F32), bsz, tb, segments, True)
    kfa, vfa, kdb, vda, kd = attn_prep_kv(kf, vf, kd, vd, cum, tables, bsz, tb, tm)
    nk = nframes // tq
    lam = diff_lambda_value(p["diff_lambda"], lam_init)
    extra = (kfa, vfa, kdb, vda, meta_blk)
    common = dict(lam=lam, gain=p["diff_gain"], out_scale=1.0 - lam_init, bsz=bsz, te=BLOCK, diff_extra="valid")
    frames = attention((qfa, qda, cum), tb, (kfa, vfa, kdb, vda), extra, tq=tq, nq=nk, tk=tq,
                       main_causal=True, fox_extra="valid", zero_pad_q=False, **common)
    meta = attention((qfa, qda, cum), tb, None, extra, tq=BLOCK, nq=1, tk=BLOCK, main_causal=False,
                     fox_extra="causal_valid", zero_pad_q=True, q_block0=meta_blk, **common)
    mixed = jnp.concatenate([frames, meta], axis=1).reshape(bsz * tb, -1)
    return mixed, kf, vf, logf, kd, vd


def odd_mixer_sample(x, gain, p, lam_init, layer, kf_cache, vf_cache, lf_cache, kd_cache, vd_cache, bsz, tb, tk):
    past = kf_cache.shape[2]
    tables = rope_tables(past + jnp.arange(tb, dtype=jnp.int32), 2 * DIFF_HEADS)
    qfa, kf, vf, qda, kd, vd, f = inproj_odd(x, gain, p["w_in"], tables, bsz, tb, tb)
    lf128 = jnp.zeros((bsz, past, 128), F32).at[:, :, :FOX_HEADS].set(lf_cache[layer].astype(F32))
    zero_c = jnp.zeros((bsz, 1, 128), F32)
    _, cum_past, total = gate_cumsum(lf128, p["f_bias"], zero_c, bsz, past, [(s, tk, 0) for s in range(0, past, tk)],
                                     False)
    logf, cum, _ = gate_cumsum(f, p["f_bias"], total, bsz, tb, [(0, tb, 0)], True)
    kfa, vfa, kdb, vda, kd = attn_prep_kv(kf, vf, kd, vd, cum, tables, bsz, tb, tb)
    tview = lambda a: jnp.transpose(a.astype(F32), (0, 1, 3, 4, 2))
    vd_flat = vd_cache.astype(F32).reshape(vd_cache.shape[0], bsz, past, -1)
    main = tuple(attn_prep_kv_cached(tview(kf_cache), tview(vf_cache), tview(kd_cache), vd_flat, cum_past, layer,
                                     bsz, past, tk))
    extra = (kfa, vfa, kdb, vda, 0)
    mixed = attention((qfa, qda, cum), tb, main, extra, lam=diff_lambda_value(p["diff_lambda"], lam_init),
                      gain=p["diff_gain"], out_scale=1.0 - lam_init, bsz=bsz, tq=tb, nq=1, tk=tk, te=tb,
                      main_causal=False, fox_extra="causal", diff_extra="none", zero_pad_q=False)
    return mixed.reshape(bsz * tb, -1), kf, vf, logf, kd, vd


def kernel(x_prompt, x_sample, state_conv, state_delta, state_ssm_re, state_ssm_im, cache_fox_k, cache_fox_v,
           cache_fox_logf, cache_diff_k, cache_diff_v, meta_tokens, norm_mix, norm_ffn, norm_final, w_in_even,
           w_out_even, conv_w, gdn_a_log, gdn_dt_bias, gdn_out_norm, ssm_lambda_re, ssm_lambda_im, ssm_log_dt,
           ssm_b_re, ssm_b_im, ssm_c_re, ssm_c_im, ssm_d, ssm_glu_w, ssm_glu_b, w_in_odd, w_out_odd, fox_f_bias,
           diff_lambda, diff_out_norm, ffn_w1, ffn_w3, ffn_w2):
    bp, seq, _ = x_prompt.shape
    bs, ls, _ = x_sample.shape
    depth = norm_mix.shape[0]
    tb = seq + BLOCK
    pad = BLOCK - N_META
    tm_p = 832
    tq = 512
    meta = jnp.broadcast_to(meta_tokens.astype(F32)[None], (bp, N_META, D_MODEL))
    xp = jnp.concatenate([x_prompt.astype(F32), jnp.zeros((bp, pad, D_MODEL), F32), meta], axis=1)
    xp = xp.reshape(bp * tb, D_MODEL)
    xs = x_sample.astype(F32).reshape(bs * ls, D_MODEL)

    def logical(a):
        return jnp.concatenate([a[:, tb - N_META:], a[:, :seq]], axis=1)

    outs = {k: [] for k in ("conv_p", "conv_s", "delta_p", "delta_s", "re_p", "re_s", "im_p", "im_s", "fk_p", "fk_s",
                            "fv_p", "fv_s", "fl_p", "fl_s", "dk_p", "dk_s", "dv_p", "dv_s")}
    head_split = {"fk_p": (FOX_HEADS, HEAD_DIM), "fv_p": (FOX_HEADS, HEAD_DIM), "dk_p": (2 * DIFF_HEADS, HEAD_DIM),
                  "dv_p": (DIFF_HEADS, DIFF_V)}
    nchunk = FFN_HIDDEN // FFN_CHUNK
    for l in range(depth):
        i = l // 2
        if l % 2 == 0:
            pe = dict(w_in=pack_even_w_in(w_in_even[i]), conv_w=conv_w[i], a_log=gdn_a_log[i], dt_bias=gdn_dt_bias[i],
                      out_gain=gdn_out_norm[i], lam_re=ssm_lambda_re[i], lam_im=ssm_lambda_im[i],
                      log_dt=ssm_log_dt[i], b_re=ssm_b_re[i], b_im=ssm_b_im[i], c_re=ssm_c_re[i], c_im=ssm_c_im[i],
                      d=ssm_d[i], glu_w=ssm_glu_w[i], glu_b=ssm_glu_b[i])
            qkv_w = 3 * GDN_HEADS * GDN_DK
            mixed_p, qkv_p, sd_p, re_p, im_p = even_mixer(
                xp, norm_mix[l], pe, jnp.zeros((bp, CONV_W - 1, qkv_w), F32),
                jnp.zeros((bp, GDN_HEADS, GDN_DK, GDN_DK), F32), jnp.zeros((bp, SSM_GROUPS, SSM_STATE), F32),
                jnp.zeros((bp, SSM_GROUPS, SSM_STATE), F32), bp, tb, tm_p, BLOCK, tb // BLOCK - 1, 32)
            mixed_s, qkv_s, sd_s, re_s, im_s = even_mixer(
                xs, norm_mix[l], pe, state_conv[i], state_delta[i], state_ssm_re[i], state_ssm_im[i],
                bs, ls, ls, ls, 0, ls)
            outs["conv_p"].append(qkv_p.reshape(bp, tb, qkv_w)[:, seq - (CONV_W - 1):seq])
            outs["conv_s"].append(jnp.concatenate([state_conv[i].astype(F32), qkv_s.reshape(bs, ls, qkv_w)],
                                                  axis=1)[:, ls:])
            outs["delta_p"].append(sd_p); outs["delta_s"].append(sd_s)
            outs["re_p"].append(re_p); outs["re_s"].append(re_s)
            outs["im_p"].append(im_p); outs["im_s"].append(im_s)
            w_out = w_out_even[i]
        else:
            lam_init = 0.8 - 0.6 * math.exp(-0.3 * l)
            po = dict(w_in=pack_odd_w_in(w_in_odd[i]), diff_lambda=diff_lambda[i], diff_gain=diff_out_norm[i],
                      f_bias=jnp.zeros((1, 128), F32).at[0, :FOX_HEADS].set(fox_f_bias[i].astype(F32)))
            mixed_p, kf, vf, lf, kd, vd = odd_mixer_prompt(xp, norm_mix[l], po, lam_init, bp, tb, seq, tm_p, tq)
            mixed_p = [(mixed_p, mixed_p.shape[-1], "flat")]
            for key, slab in (("fk_p", kf), ("fv_p", vf), ("dk_p", kd), ("dv_p", vd)):
                outs[key].append(slab_to_logical(slab, seq, 0, 1).reshape((bp, N_META + seq) + head_split[key]))
            outs["fl_p"].append(logical(lf)[:, :, :FOX_HEADS])
            mixed_s, kf, vf, lf, kd, vd = odd_mixer_sample(
                xs, norm_mix[l], po, lam_init, i, cache_fox_k, cache_fox_v, cache_fox_logf, cache_diff_k,
                cache_diff_v, bs, ls, 512)
            mixed_s = [(mixed_s, mixed_s.shape[-1], "flat")]
            outs["fk_s"].append(kf.reshape(bs, ls, FOX_HEADS, HEAD_DIM))
            outs["fv_s"].append(vf.reshape(bs, ls, FOX_HEADS, HEAD_DIM))
            outs["fl_s"].append(lf[:, :, :FOX_HEADS])
            outs["dk_s"].append(kd.reshape(bs, ls, 2 * DIFF_HEADS, HEAD_DIM))
            outs["dv_s"].append(vd.reshape(bs, ls, DIFF_HEADS, DIFF_V))
            w_out = w_out_odd[i]
        w1 = ffn_w1[l].astype(BF16).reshape(D_MODEL, nchunk, FFN_CHUNK).transpose(1, 0, 2)
        w3 = ffn_w3[l].astype(BF16).reshape(D_MODEL, nchunk, FFN_CHUNK).transpose(1, 0, 2)
        w2 = ffn_w2[l].astype(BF16).reshape(nchunk, FFN_CHUNK, D_MODEL)
        xp = outproj_ffn(xp, mixed_p, w_out.astype(BF16), norm_ffn[l], w1, w3, w2, bp, tb, tm_p)
        xs = outproj_ffn(xs, mixed_s, w_out.astype(BF16), norm_ffn[l], w1, w3, w2, bs, ls, ls)

    y_prompt = final_norm(xp, norm_final, bp, tb, seq, 512).reshape(bp, seq, D_MODEL)
    y_sample = final_norm(xs, norm_final, bs, ls, ls, ls).reshape(bs, ls, D_MODEL)
    order = ("conv_p", "conv_s", "delta_p", "delta_s", "re_p", "re_s", "im_p", "im_s", "fk_p", "fk_s", "fv_p", "fv_s",
             "fl_p", "fl_s", "dk_p", "dk_s", "dv_p", "dv_s")
    return (y_prompt, y_sample) + tuple(jnp.stack(outs[k]) for k in order)
```

```python
import functools
import math

import jax
import jax.numpy as jnp
import numpy as np
from jax import lax
from jax.experimental import pallas as pl
from jax.experimental.pallas import tpu as pltpu

F32 = jnp.float32
BF16 = jnp.bfloat16
HIGHEST = lax.Precision.HIGHEST
LOG2E = math.log2(math.e)

D_MODEL = 1024
N_META = 16
RMS_EPS = 1e-6
HEAD_DIM = 64
GDN_HEADS = 8
GDN_DK = 128
CONV_W = 4
SSM_WIDTH = 512
SSM_GROUP = 16
SSM_GROUPS = 32
SSM_STATE = 64
FOX_HEADS = 8
DIFF_HEADS = 4
DIFF_V = 128
ROT_DIM = 16
ROPE_THETA = 500000.0
FFN_HIDDEN = 2816
FFN_CHUNK = 256
BLOCK = 64
VMEM_LIMIT = 56 * 1024 * 1024


def _cparams(sem):
    return pltpu.CompilerParams(dimension_semantics=sem, vmem_limit_bytes=VMEM_LIMIT)


def _const_spec(shape, single_buffer=False):
    nd = len(shape)
    if single_buffer:
        return pl.BlockSpec(shape, lambda *_: (0,) * nd, pipeline_mode=pl.Buffered(1))
    return pl.BlockSpec(shape, lambda *_: (0,) * nd)


def _sigmoid(x):
    return 1.0 / (1.0 + jnp.exp(-x))


def _silu(x):
    return x * _sigmoid(x)


def _rms_scale(x):
    return x * lax.rsqrt(jnp.mean(x * x, axis=-1, keepdims=True) + RMS_EPS)


def _row_spec(kind, tm, width, nt):
    if kind == "flat":
        return pl.BlockSpec((tm, width), lambda b, i: (b * nt + i, 0))
    return pl.BlockSpec((tm, width), lambda b, i: (i, b))


def _row_shape(kind, bsz, tb, width, dtype):
    if kind == "flat":
        return jax.ShapeDtypeStruct((bsz * tb, width), dtype)
    return jax.ShapeDtypeStruct((tb, bsz * width), dtype)


def _inproj_kernel(x_ref, g_ref, w_ref, *o_refs, widths):
    xn = (_rms_scale(x_ref[...]) * g_ref[...]).astype(BF16)
    off = 0
    for o_ref, wd in zip(o_refs, widths):
        for c0 in range(0, wd, 512):
            cw = min(512, wd - c0)
            o_ref[:, c0:c0 + cw] = jnp.dot(xn, w_ref[:, off + c0:off + c0 + cw],
                                           preferred_element_type=F32)
        off += wd


def inproj(x, gain, w, outs, bsz, tb, tm):
    nt = tb // tm
    widths = tuple(wd for wd, _ in outs)
    return pl.pallas_call(
        functools.partial(_inproj_kernel, widths=widths),
        grid=(bsz, nt),
        in_specs=[_row_spec("flat", tm, D_MODEL, nt), _const_spec((1, D_MODEL)), _const_spec(w.shape, True)],
        out_specs=[_row_spec(kind, tm, wd, nt) for wd, kind in outs],
        out_shape=[_row_shape(kind, bsz, tb, wd, F32) for wd, kind in outs],
        compiler_params=_cparams(("parallel", "parallel")),
        name="inproj",
    )(x, gain.reshape(1, D_MODEL), w)


def _outffn_kernel(res_ref, *refs, a_widths, nc):
    na = len(a_widths)
    a_refs = refs[:na]
    wout_ref, g_ref, w1_ref, w3_ref, w2_ref, o_ref, acc_ref, xn_ref = refs[na:]
    x1 = res_ref[...]
    off = 0
    for a_ref, wd in zip(a_refs, a_widths):
        x1 = x1 + jnp.dot(a_ref[...].astype(BF16), wout_ref[off:off + wd, :], preferred_element_type=F32)
        off += wd
    xn_ref[...] = (_rms_scale(x1) * g_ref[...]).astype(BF16)
    acc_ref[...] = x1

    def body(c, carry):
        xn = xn_ref[...]
        h1 = jnp.dot(xn, w1_ref[c], preferred_element_type=F32)
        h3 = jnp.dot(xn, w3_ref[c], preferred_element_type=F32)
        act = (_silu(h1) * h3).astype(BF16)
        acc_ref[...] += jnp.dot(act, w2_ref[c], preferred_element_type=F32)
        return carry

    lax.fori_loop(0, nc, body, 0)
    o_ref[...] = acc_ref[...]


def outproj_ffn(res, mixed, w_out, gain, w1, w3, w2, bsz, tb, tm):
    nt = tb // tm
    nc = w1.shape[0]
    a_widths = tuple(wd for _, wd, _ in mixed)
    row = _row_spec("flat", tm, D_MODEL, nt)
    return pl.pallas_call(
        functools.partial(_outffn_kernel, a_widths=a_widths, nc=nc),
        grid=(bsz, nt),
        in_specs=[row] + [_row_spec(kind, tm, wd, nt) for _, wd, kind in mixed]
        + [_const_spec(w_out.shape, True), _const_spec((1, D_MODEL)), _const_spec(w1.shape, True),
           _const_spec(w3.shape, True), _const_spec(w2.shape, True)],
        out_specs=row,
        out_shape=jax.ShapeDtypeStruct(res.shape, F32),
        scratch_shapes=[pltpu.VMEM((tm, D_MODEL), F32), pltpu.VMEM((tm, D_MODEL), BF16)],
        compiler_params=_cparams(("parallel", "parallel")),
        name="outproj_ffn",
    )(res, *[a for a, _, _ in mixed], w_out, gain.reshape(1, D_MODEL), w1, w3, w2)


def _final_norm_kernel(x_ref, g_ref, o_ref):
    o_ref[...] = _rms_scale(x_ref[...]) * g_ref[...]


def final_norm(x, gain, bsz, tb, t_out, tm):
    spec = pl.BlockSpec((None, tm, D_MODEL), lambda b, i: (b, i, 0))
    return pl.pallas_call(
        _final_norm_kernel,
        grid=(bsz, t_out // tm),
        in_specs=[spec, _const_spec((1, D_MODEL))],
        out_specs=spec,
        out_shape=jax.ShapeDtypeStruct((bsz, t_out, D_MODEL), F32),
        compiler_params=_cparams(("parallel", "parallel")),
        name="final_norm",
    )(x.reshape(bsz, tb, D_MODEL), gain.reshape(1, D_MODEL))


def _dot_nt(a, b):
    return lax.dot_general(a, b, (((1,), (1,)), ((), ())), preferred_element_type=F32)


def _dot_hi(a, b):
    return jnp.dot(a, b, precision=HIGHEST, preferred_element_type=F32)


def _bdot(a, b):
    return lax.dot_general(a, b, (((2,), (1,)), ((0,), (0,))), preferred_element_type=F32)


def _bdot_nt(a, b):
    return lax.dot_general(a, b, (((2,), (2,)), ((0,), (0,))), preferred_element_type=F32)


def _split(a):
    hi = a.astype(BF16)
    return hi, (a - hi.astype(F32)).astype(BF16)


def _bdot_split(a, b):
    return _bdot(a[0], b[0]) + _bdot(a[1], b[0]) + _bdot(a[0], b[1])


def _unit_lower_solve(a_low, rhs, csz):
    sub = 16
    r = lax.broadcasted_iota(jnp.int32, (csz, csz), 0)
    c = lax.broadcasted_iota(jnp.int32, (csz, csz), 1)
    same = ((r // sub) == (c // sub))[None]
    eye = (r == c).astype(F32)[None]
    dg = jnp.where(same, a_low, 0.0)
    off = a_low - dg
    d1 = _split(dg)
    d2 = _split(_bdot_split(d1, d1))
    d4 = _split(_bdot_split(d2, d2))
    d8 = _split(_bdot_split(d4, d4))
    t0 = eye - dg
    for dp in (d2, d4, d8):
        t0 = t0 + _bdot_split(_split(t0), dp)
    t0 = _split(t0)
    g1 = _split(_bdot_split(t0, _split(off)))
    x = _bdot_split(t0, _split(rhs))
    nblk = csz // sub
    powers = []
    gp = g1
    k = 2
    while k < nblk:
        gp = _split(_bdot_split(gp, gp))
        powers.append(gp)
        k *= 2
    for gp in powers:
        x = x + _bdot_split(gp, _split(x))
    return x - _bdot_split(g1, _split(x))


def _gdn_kernel(qkv_ref, z_ref, ba_ref, cw_ref, alog_ref, dtb_ref, gain_ref, conv0_ref, s0_ref,
                o_ref, sout_ref, xbuf, s_ref, *, csz, nb):
    c = pl.program_id(1)
    nh = nb * GDN_HEADS

    @pl.when(c == 0)
    def _():
        xbuf[:, 0:8, :] = conv0_ref[...]
        s_ref[...] = s0_ref[...].reshape(nh, GDN_DK, GDN_DK)

    xbuf[:, 8:8 + csz, :] = qkv_ref[...]

    r = lax.broadcasted_iota(jnp.int32, (csz, csz), 0)
    cc = lax.broadcasted_iota(jnp.int32, (csz, csz), 1)
    causal = r >= cc
    strict = r > cc
    gain = gain_ref[...]
    beta, gc, gct = [], [], []
    for bb in range(nb):
        ba = ba_ref[bb]
        beta.append(_sigmoid(ba))
        xs = ba + dtb_ref[...]
        softplus = jnp.maximum(xs, 0.0) + jnp.log(1.0 + jnp.exp(-jnp.abs(xs)))
        logg = -jnp.exp(alog_ref[...]) * softplus
        gc.append(_dot_hi(causal.astype(F32), logg))
        gct.append(gc[bb].T)

    def conv_silu(bb, col):
        y = cw_ref[3:4, col:col + GDN_DK] * xbuf[bb, 8:8 + csz, col:col + GDN_DK]
        for j in range(3):
            y = y + cw_ref[j:j + 1, col:col + GDN_DK] * xbuf[bb, 5 + j:5 + j + csz, col:col + GDN_DK]
        return _silu(y)

    heads = [(bb, h) for bb in range(nb) for h in range(GDN_HEADS)]
    q = jnp.stack([conv_silu(bb, h * GDN_DK) for bb, h in heads])
    k = jnp.stack([conv_silu(bb, (GDN_HEADS + h) * GDN_DK) for bb, h in heads])
    v = jnp.stack([conv_silu(bb, (2 * GDN_HEADS + h) * GDN_DK) for bb, h in heads])
    q = q * lax.rsqrt(jnp.sum(q * q, axis=-1, keepdims=True) + 1e-6) * (GDN_DK ** -0.5)
    k = k * lax.rsqrt(jnp.sum(k * k, axis=-1, keepdims=True) + 1e-6)
    bh = jnp.stack([beta[bb][:, h:h + 1] for bb, h in heads])
    gch = jnp.stack([gc[bb][:, 8 + h:9 + h] for bb, h in heads])
    grow = jnp.stack([gct[bb][8 + h:9 + h, :] for bb, h in heads])
    decay = jnp.where(causal[None], jnp.exp(gch - grow), 0.0)
    kb = k.astype(BF16)
    kk = _bdot_nt(kb, kb)
    qk = _bdot_nt(q.astype(BF16), kb)
    a_low = jnp.where(strict[None], bh * kk * decay, 0.0)
    eg = jnp.exp(gch)
    rhs = jnp.concatenate([v * bh, k * (bh * eg)], axis=-1)
    sol = _unit_lower_solve(a_low, rhs, csz)
    u = sol[:, :, :GDN_DK]
    w = sol[:, :, GDN_DK:]
    s_old = s_ref[...]
    sb = s_old.astype(BF16)
    v_new = u - _bdot(w.astype(BF16), sb)
    vb = v_new.astype(BF16)
    o = _bdot((q * eg).astype(BF16), sb) + _bdot((qk * decay).astype(BF16), vb)
    g_last = gch[:, csz - 1:csz, :]
    k2 = (k * jnp.exp(g_last - gch)).astype(BF16)
    on = _rms_scale(o) * gain
    for n, (bb, h) in enumerate(heads):
        s_ref[n] = s_old[n] * jnp.exp(g_last[n]) + jnp.dot(k2[n].T, vb[n], preferred_element_type=F32)
        zz = z_ref[bb, :, h * GDN_DK:(h + 1) * GDN_DK]
        o_ref[bb, :, h * GDN_DK:(h + 1) * GDN_DK] = (on[n] * _silu(zz)).astype(BF16)

    xbuf[:, 0:8, :] = xbuf[:, csz:csz + 8, :]

    @pl.when(c == pl.num_programs(1) - 1)
    def _():
        sout_ref[...] = s_ref[...].reshape(nb, GDN_HEADS, GDN_DK, GDN_DK)


GDN_BATCHES_PER_STEP = 4


def gdn(qkv, z, ba, conv_w, a_log, dt_bias, out_gain, conv0, s0, bsz, tb, csz, first_block):
    nblk = tb // csz
    nb = GDN_BATCHES_PER_STEP
    qk_w = 3 * GDN_HEADS * GDN_DK
    o_w = GDN_HEADS * GDN_DK

    def rows(b, c):
        return (b, (c + first_block) % nblk, 0)

    def lane_pad(vec):
        return jnp.zeros((1, 128), F32).at[0, 8:16].set(vec.astype(F32))

    conv0p = jnp.concatenate([jnp.zeros((bsz, 8 - (CONV_W - 1), qk_w), F32), conv0.astype(F32)], axis=1)
    state = pl.BlockSpec((nb, GDN_HEADS, GDN_DK, GDN_DK), lambda b, c: (b, 0, 0, 0))
    o, s_new = pl.pallas_call(
        functools.partial(_gdn_kernel, csz=csz, nb=nb),
        grid=(bsz // nb, nblk),
        in_specs=[pl.BlockSpec((nb, csz, qk_w), rows), pl.BlockSpec((nb, csz, o_w), rows),
                  pl.BlockSpec((nb, csz, 128), rows), _const_spec((CONV_W, qk_w)), _const_spec((1, 128)),
                  _const_spec((1, 128)), _const_spec((1, GDN_DK)),
                  pl.BlockSpec((nb, 8, qk_w), lambda b, c: (b, 0, 0)), state],
        out_specs=[pl.BlockSpec((nb, csz, o_w), rows), state],
        out_shape=[jax.ShapeDtypeStruct((bsz, tb, o_w), BF16),
                   jax.ShapeDtypeStruct((bsz, GDN_HEADS, GDN_DK, GDN_DK), F32)],
        scratch_shapes=[pltpu.VMEM((nb, 8 + csz, qk_w), F32), pltpu.VMEM((nb * GDN_HEADS, GDN_DK, GDN_DK), F32)],
        compiler_params=_cparams(("parallel", "arbitrary")),
        name="gdn",
    )(qkv.reshape(bsz, tb, qk_w), z.reshape(bsz, tb, o_w), ba.reshape(bsz, tb, 128), conv_w.astype(F32),
      lane_pad(a_log), lane_pad(dt_bias), out_gain.reshape(1, GDN_DK).astype(F32), conv0p, s0.astype(F32))
    return o.reshape(bsz * tb, o_w), s_new


S5_CHUNKS = 4
S5_CH_U = SSM_WIDTH // S5_CHUNKS
S5_CH_P = S5_CH_U // SSM_GROUP * SSM_STATE


def _s5_kernel(u_ref, wb_ref, wc_ref, a_ref, d_ref, gw_ref, gb_ref, st0_ref, o_ref, stout_ref, xs, st_ref,
               *, tt, bsz):
    c = pl.program_id(0)

    @pl.when(c == 0)
    def _():
        st_ref[...] = st0_ref[...]

    rows = tt * bsz
    u = u_ref[...].reshape(rows, SSM_WIDTH)
    ub = u.astype(BF16)
    for j in range(S5_CHUNKS):
        xs[j] = jnp.dot(ub[:, j * S5_CH_U:(j + 1) * S5_CH_U], wb_ref[j], preferred_element_type=F32)

    for j in range(S5_CHUNKS):
        ar = jnp.broadcast_to(a_ref[j, 0:1, :], (bsz, S5_CH_P))
        ai = jnp.broadcast_to(a_ref[j, 1:2, :], (bsz, S5_CH_P))

        def step(t, carry, j=j, ar=ar, ai=ai):
            re, im = carry
            row = pl.multiple_of(t * bsz, bsz)
            nre = ar * re - ai * im + xs[j, pl.ds(row, bsz), 0:S5_CH_P]
            nim = ar * im + ai * re + xs[j, pl.ds(row, bsz), S5_CH_P:2 * S5_CH_P]
            xs[j, pl.ds(row, bsz), 0:S5_CH_P] = nre
            xs[j, pl.ds(row, bsz), S5_CH_P:2 * S5_CH_P] = nim
            return nre, nim

        re, im = lax.fori_loop(0, tt, step, (st_ref[j, :, 0:S5_CH_P], st_ref[j, :, S5_CH_P:2 * S5_CH_P]))
        st_ref[j, :, 0:S5_CH_P] = re
        st_ref[j, :, S5_CH_P:2 * S5_CH_P] = im

    ys = [jnp.dot(xs[j].astype(BF16), wc_ref[j], preferred_element_type=F32) for j in range(S5_CHUNKS)]
    y = jnp.concatenate(ys, axis=-1) + u * d_ref[...]
    hg = 0.5 * y * (1.0 + jnp.tanh(math.sqrt(2.0 / math.pi) * (y + 0.044715 * (y * y * y))))
    gate = _sigmoid(jnp.dot(hg.astype(BF16), gw_ref[...], preferred_element_type=F32) + gb_ref[...])
    o_ref[...] = (hg * gate).reshape(tt, bsz, SSM_WIDTH)

    @pl.when(c == pl.num_programs(0) - 1)
    def _():
        stout_ref[...] = st_ref[...]


def s5(u_tb, re0, im0, lam_re, lam_im, log_dt, b_re, b_im, c_re, c_im, d_skip, glu_w, glu_b,
       bsz, tb, tt, first_tile):
    ntile = tb // tt
    lr = jnp.minimum(lam_re.astype(F32), -1e-4)
    li = lam_im.astype(F32)
    dt = jnp.exp(log_dt.astype(F32))[:, None]
    mag = jnp.exp(lr * dt)
    ar, ai = mag * jnp.cos(li * dt), mag * jnp.sin(li * dt)
    den = lr * lr + li * li
    nr, ni = ar - 1.0, ai
    cr, ci = (nr * lr + ni * li) / den, (ni * lr - nr * li) / den
    br, bi = b_re.astype(F32), b_im.astype(F32)
    bbr = cr[..., None] * br - ci[..., None] * bi
    bbi = cr[..., None] * bi + ci[..., None] * br
    gpc = S5_CH_U // SSM_GROUP

    def block_diag_in(bb):
        bb = bb.reshape(S5_CHUNKS, gpc, SSM_STATE, SSM_GROUP)
        eye = jnp.eye(gpc, dtype=F32)
        return jnp.einsum("jgpm,gh->jgmhp", bb, eye).reshape(S5_CHUNKS, S5_CH_U, S5_CH_P)

    def block_diag_out(cm):
        cm = cm.reshape(S5_CHUNKS, gpc, SSM_GROUP, SSM_STATE)
        eye = jnp.eye(gpc, dtype=F32)
        return jnp.einsum("jgmp,gh->jgphm", cm, eye).reshape(S5_CHUNKS, S5_CH_P, S5_CH_U)

    wb = jnp.concatenate([block_diag_in(bbr), block_diag_in(bbi)], axis=-1).astype(BF16)
    wc = jnp.concatenate([block_diag_out(c_re.astype(F32)), -block_diag_out(c_im.astype(F32))],
                         axis=1).astype(BF16)
    a_pack = jnp.stack([ar.reshape(S5_CHUNKS, S5_CH_P), ai.reshape(S5_CHUNKS, S5_CH_P)], axis=1)
    st0 = jnp.concatenate([re0.astype(F32).reshape(bsz, S5_CHUNKS, S5_CH_P),
                           im0.astype(F32).reshape(bsz, S5_CHUNKS, S5_CH_P)], axis=-1).transpose(1, 0, 2)

    def tile(c):
        return ((c + first_tile) % ntile, 0, 0)

    out, st = pl.pallas_call(
        functools.partial(_s5_kernel, tt=tt, bsz=bsz),
        grid=(ntile,),
        in_specs=[pl.BlockSpec((tt, bsz, SSM_WIDTH), tile), _const_spec(wb.shape), _const_spec(wc.shape),
                  _const_spec(a_pack.shape), _const_spec((1, SSM_WIDTH)), _const_spec((SSM_WIDTH, SSM_WIDTH)),
                  _const_spec((1, SSM_WIDTH)), _const_spec(st0.shape)],
        out_specs=[pl.BlockSpec((tt, bsz, SSM_WIDTH), tile), _const_spec(st0.shape)],
        out_shape=[jax.ShapeDtypeStruct((tb, bsz, SSM_WIDTH), F32), jax.ShapeDtypeStruct(st0.shape, F32)],
        scratch_shapes=[pltpu.VMEM((S5_CHUNKS, tt * bsz, 2 * S5_CH_P), F32),
                        pltpu.VMEM((S5_CHUNKS, bsz, 2 * S5_CH_P), F32)],
        compiler_params=_cparams(("arbitrary",)),
        name="s5",
    )(u_tb, wb, wc, a_pack, d_skip.reshape(1, SSM_WIDTH).astype(F32), glu_w.astype(BF16),
      glu_b.reshape(1, SSM_WIDTH).astype(F32), st0)
    st = st.transpose(1, 0, 2)
    re_new = st[..., :S5_CH_P].reshape(bsz, SSM_GROUPS, SSM_STATE)
    im_new = st[..., S5_CH_P:].reshape(bsz, SSM_GROUPS, SSM_STATE)
    return out, re_new, im_new


EVEN_OUTS = ((3 * GDN_HEADS * GDN_DK, "flat"), (GDN_HEADS * GDN_DK, "flat"), (128, "flat"), (SSM_WIDTH, "tmajor"))


def pack_even_w_in(w_in):
    qkv_w = 3 * GDN_HEADS * GDN_DK
    z_w = GDN_HEADS * GDN_DK
    c0 = qkv_w + z_w
    ba = jnp.zeros((D_MODEL, 128), w_in.dtype).at[:, :2 * GDN_HEADS].set(w_in[:, c0:c0 + 2 * GDN_HEADS])
    return jnp.concatenate([w_in[:, :c0], ba, w_in[:, c0 + 2 * GDN_HEADS:]], axis=1).astype(BF16)


def even_mixer(x, gain, p, conv0, s0, re0, im0, bsz, tb, tm, csz, first_block, tt):
    qkv, z, ba, u_t = inproj(x, gain, p["w_in"], EVEN_OUTS, bsz, tb, tm)
    o_gdn, s_new = gdn(qkv, z, ba, p["conv_w"], p["a_log"], p["dt_bias"], p["out_gain"], conv0, s0,
                       bsz, tb, csz, first_block)
    o_s5, re_new, im_new = s5(u_t.reshape(tb, bsz, SSM_WIDTH), re0, im0, p["lam_re"], p["lam_im"], p["log_dt"],
                              p["b_re"], p["b_im"], p["c_re"], p["c_im"], p["d"], p["glu_w"], p["glu_b"],
                              bsz, tb, tt, first_block * csz // tt)
    mixed = [(o_gdn, GDN_HEADS * GDN_DK, "flat"), (o_s5.reshape(tb, bsz * SSM_WIDTH), SSM_WIDTH, "tmajor")]
    return mixed, qkv, s_new, re_new, im_new


def _gate_kernel(f_ref, bias_ref, c0_ref, logf_ref, cum_ref, tot_ref, *, segments, log_sigmoid):
    carry = c0_ref[...]
    for start, size, pad_rows in segments:
        x = f_ref[start:start + size, :]
        if log_sigmoid:
            x = x + bias_ref[...]
            x = jnp.minimum(x, 0.0) - jnp.log(1.0 + jnp.exp(-jnp.abs(x)))
        if pad_rows:
            x = jnp.where(lax.broadcasted_iota(jnp.int32, x.shape, 0) >= pad_rows, x, 0.0)
        r = lax.broadcasted_iota(jnp.int32, (size, size), 0)
        cc = lax.broadcasted_iota(jnp.int32, (size, size), 1)
        cum = _dot_hi((r >= cc).astype(F32), x) + carry
        logf_ref[start:start + size, :] = x
        cum_ref[start:start + size, :] = cum * LOG2E
        carry = cum[size - 1:size, :]
    tot_ref[...] = carry


def gate_cumsum(f, bias, carry0, bsz, tb, segments, log_sigmoid):
    seq = pl.BlockSpec((None, tb, 128), lambda b: (b, 0, 0))
    one = pl.BlockSpec((None, 1, 128), lambda b: (b, 0, 0))
    return pl.pallas_call(
        functools.partial(_gate_kernel, segments=tuple(segments), log_sigmoid=log_sigmoid),
        grid=(bsz,),
        in_specs=[seq, _const_spec((1, 128)), one],
        out_specs=[seq, seq, one],
        out_shape=[jax.ShapeDtypeStruct((bsz, tb, 128), F32), jax.ShapeDtypeStruct((bsz, tb, 128), F32),
                   jax.ShapeDtypeStruct((bsz, 1, 128), F32)],
        compiler_params=_cparams(("parallel",)),
        name="gate_cumsum",
    )(f, bias, carry0)


N_BIAS_TERMS = 3


def _head_padded(x, scale, other):
    lane = lax.broadcasted_iota(jnp.int32, (x.shape[0], 128), 1)
    parts = []
    for h in range(x.shape[1] // HEAD_DIM):
        g = x[:, (h // 2) * 128:(h // 2 + 1) * 128] * scale
        own = (lane >= HEAD_DIM) == bool(h % 2)
        parts.append(jnp.where(own, g, other(h, lane)).astype(BF16))
    return jnp.concatenate(parts, axis=-1)


def _rotary(x, c_ref, s1_ref, s2_ref):
    width = x.shape[-1]
    return (x * c_ref[...] + pltpu.roll(x, width - ROT_DIM // 2, 1) * s1_ref[...]
            + pltpu.roll(x, ROT_DIM // 2, 1) * s2_ref[...])


def _prep_k_kernel(kf_ref, cum_ref, kfa_ref):
    cum = cum_ref[...]

    def offset_pieces(h, lane):
        c = jnp.broadcast_to(cum[:, h:h + 1], lane.shape)
        hi = c.astype(BF16).astype(F32)
        mid = (c - hi).astype(BF16).astype(F32)
        lo = c - hi - mid
        sel = lane % HEAD_DIM
        return jnp.where(sel == 0, -hi, jnp.where(sel == 1, -mid, jnp.where(sel == 2, -lo, 0.0)))

    kfa_ref[...] = _head_padded(kf_ref[...], 1.0, offset_pieces)


def _prep_kv_cached_kernel(kft_ref, vft_ref, kdt_ref, vd_ref, cum_ref, kfa_ref, vfa_ref, kdb_ref, vda_ref):
    rows = kft_ref.shape[-1]
    cumt = cum_ref[...].T
    sub = lax.broadcasted_iota(jnp.int32, (HEAD_DIM, rows), 0)
    ones = jnp.ones((HEAD_DIM, rows), F32)
    for h in range(FOX_HEADS):
        c = jnp.broadcast_to(cumt[h:h + 1, :], (HEAD_DIM, rows))
        hi = c.astype(BF16).astype(F32)
        mid = (c - hi).astype(BF16).astype(F32)
        lo = c - hi - mid
        pieces = jnp.where(sub == 0, -hi, jnp.where(sub == 1, -mid, jnp.where(sub == 2, -lo, 0.0)))
        pair_k = (kft_ref[h], pieces) if h % 2 == 0 else (pieces, kft_ref[h])
        pair_v = (vft_ref[h], ones) if h % 2 == 0 else (ones, vft_ref[h])
        kfa_ref[:, h * 128:(h + 1) * 128] = jnp.concatenate(pair_k, axis=0).T.astype(BF16)
        vfa_ref[:, h * 128:(h + 1) * 128] = jnp.concatenate(pair_v, axis=0).T.astype(BF16)
    for g in range(DIFF_HEADS):
        kdb_ref[:, g * 128:(g + 1) * 128] = jnp.concatenate([kdt_ref[2 * g], kdt_ref[2 * g + 1]],
                                                            axis=0).T.astype(BF16)
    vd = vd_ref[...].astype(BF16)
    ones_v = jnp.ones((rows, DIFF_V), BF16)
    vda_ref[...] = jnp.concatenate(
        [a for hd in range(DIFF_HEADS) for a in (vd[:, hd * DIFF_V:(hd + 1) * DIFF_V], ones_v)], axis=-1)


def attn_prep_kv_cached(kft, vft, kdt, vd, cum2, layer, bsz, tb, tm):
    width = vd.shape[-1]
    ts = pl.BlockSpec((None, None, FOX_HEADS, HEAD_DIM, tm), lambda i, b: (layer, b, 0, 0, i))
    xs_in = pl.BlockSpec((None, None, tm, width), lambda i, b: (layer, b, i, 0))
    xs = pl.BlockSpec((None, tm, width), lambda i, b: (b, i, 0))
    ws = pl.BlockSpec((None, tm, 2 * width), lambda i, b: (b, i, 0))
    cs = pl.BlockSpec((None, tm, 128), lambda i, b: (b, i, 0))
    shp = lambda w: jax.ShapeDtypeStruct((bsz, tb, w), BF16)
    return pl.pallas_call(
        _prep_kv_cached_kernel,
        grid=(tb // tm, bsz),
        in_specs=[ts, ts, ts, xs_in, cs],
        out_specs=[ws, ws, xs, ws],
        out_shape=[shp(2 * width), shp(2 * width), shp(width), shp(2 * width)],
        compiler_params=_cparams(("parallel", "parallel")),
        name="attn_prep_kv_cached",
    )(kft, vft, kdt, vd, cum2)


def attn_prep_k(kf, cum2, bsz, tb, tm):
    width = kf.shape[-1]
    return pl.pallas_call(
        _prep_k_kernel,
        grid=(tb // tm, bsz),
        in_specs=[pl.BlockSpec((None, tm, width), lambda i, b: (b, i, 0)),
                  pl.BlockSpec((None, tm, 128), lambda i, b: (b, i, 0))],
        out_specs=pl.BlockSpec((None, tm, 2 * width), lambda i, b: (b, i, 0)),
        out_shape=jax.ShapeDtypeStruct((bsz, tb, 2 * width), BF16),
        compiler_params=_cparams(("parallel", "parallel")),
        name="attn_prep_k",
    )(kf, cum2)


def rope_tables(pos, nheads):
    half = ROT_DIM // 2
    inv_freq = ROPE_THETA ** (-jnp.arange(0, ROT_DIM, 2, dtype=F32) / ROT_DIM)
    ang = pos.astype(F32)[:, None] * inv_freq[None, :]
    cos, sin = jnp.cos(ang), jnp.sin(ang)
    t = pos.shape[0]
    rest = jnp.zeros((t, HEAD_DIM - ROT_DIM), F32)
    z = jnp.zeros((t, half), F32)
    ctab = jnp.concatenate([cos, cos, rest + 1.0], axis=1)
    s1 = jnp.concatenate([-sin, z, rest], axis=1)
    s2 = jnp.concatenate([z, sin, rest], axis=1)
    return tuple(jnp.tile(a, (1, nheads)) for a in (ctab, s1, s2))


NEG_INF = -1e30
N_MAPS = FOX_HEADS + 2 * DIFF_HEADS


def _attn_kernel(*refs, tq, tk, te, has_main, main_causal, fox_extra, diff_extra, zero_pad_q, pad_rows,
                 out_scale):
    if has_main:
        (qf_ref, qd_ref, cq_ref, kf_ref, vf_ref, kd_ref, vd_ref,
         kfe_ref, vfe_ref, kde_ref, vde_ref, lam_ref, gain_ref,
         o_ref, m_ref, cqb_ref, accf_ref, accd_ref) = refs
    else:
        (qf_ref, qd_ref, cq_ref, kfe_ref, vfe_ref, kde_ref, vde_ref, lam_ref, gain_ref,
         o_ref, m_ref, cqb_ref, accf_ref, accd_ref) = refs
    i = pl.program_id(1)
    j = pl.program_id(2)
    neg = NEG_INF * LOG2E

    def lanes(x, ncols):
        reps, rem = divmod(ncols, 128)
        return jnp.concatenate([x] * reps + ([x[:, :rem]] if rem else []), axis=-1)

    def one_map(idx, q, k, v, acc_ref, slot, cq, mask, qr):
        t = _dot_nt(q, k)
        nrows, ncols = t.shape
        if mask is not None:
            t = jnp.where(mask, t, neg)
        mt = jnp.broadcast_to(jnp.max(t, axis=-1, keepdims=True), (nrows, 128))
        m_prev = m_ref[idx, qr]
        if cq is None:
            m_new = jnp.maximum(m_prev, mt)
            shift = m_new
        else:
            m_new = jnp.maximum(m_prev, mt + cq)
            shift = m_new - cq
        p = jnp.exp2(t - lanes(shift, ncols))
        alpha = jnp.exp2(m_prev - m_new)
        m_ref[idx, qr] = m_new
        acc_ref[slot, qr] = (lanes(alpha, acc_ref.shape[-1]) * acc_ref[slot, qr]
                             + jnp.dot(p.astype(BF16), v, preferred_element_type=F32))

    def process(k_tiles, fox_mask, diff_mask, qr=slice(None)):
        def rows(which, lo, hi):
            parts = [tile[which][:n, lo:hi] for tile, n in k_tiles]
            return parts[0] if len(parts) == 1 else jnp.concatenate(parts, axis=0)

        for h in range(FOX_HEADS):
            lo, hi = h * 128, (h + 1) * 128
            one_map(h, qf_ref[qr, lo:hi], rows(0, lo, hi), rows(1, lo, hi), accf_ref, h, cqb_ref[h, qr], fox_mask, qr)
        for hm in range(2 * DIFF_HEADS):
            g = hm // 2
            one_map(FOX_HEADS + hm, qd_ref[qr, hm * 128:(hm + 1) * 128], rows(2, g * 128, (g + 1) * 128),
                    rows(3, g * 2 * DIFF_V, (g + 1) * 2 * DIFF_V), accd_ref, hm, None, diff_mask, qr)

    def visible(extra_kind, main_kind, width, r0=0, nrows=tq):
        if extra_kind == "none" and main_kind == "none":
            return None
        r = lax.broadcasted_iota(jnp.int32, (nrows, width), 0) + r0
        c = lax.broadcasted_iota(jnp.int32, (nrows, width), 1)
        cm = c - te
        rules = {"valid": c >= pad_rows, "causal": c <= r,
                 "causal_valid": jnp.logical_and(c <= r, c >= pad_rows),
                 "main_causal": cm <= r,
                 "main_block_causal": cm < (lax.shift_right_logical(r, 6) + 1) * BLOCK}
        in_extra = c < te
        if main_kind == "none":
            return jnp.logical_or(c >= te, rules[extra_kind])
        if extra_kind == "none":
            return jnp.logical_or(in_extra, rules[main_kind])
        return jnp.logical_or(jnp.logical_and(in_extra, rules[extra_kind]),
                              jnp.logical_and(c >= te, rules[main_kind]))

    @pl.when(j == 0)
    def _():
        m_ref[...] = jnp.full(m_ref.shape, NEG_INF, F32)
        accf_ref[...] = jnp.zeros(accf_ref.shape, F32)
        accd_ref[...] = jnp.zeros(accd_ref.shape, F32)
        for h in range(FOX_HEADS):
            cqb_ref[h] = jnp.broadcast_to(cq_ref[:, h:h + 1], (tq, 128))

    extra_tile = (kfe_ref, vfe_ref, kde_ref, vde_ref)
    if has_main:
        main_tile = (kf_ref, vf_ref, kd_ref, vd_ref)
        if main_causal:
            assert tq == tk
            last, before_last = j == i, j < i
            fox_main, diff_main = "main_causal", "main_block_causal"
        else:
            last, before_last = j == pl.num_programs(2) - 1, j < pl.num_programs(2) - 1
            fox_main = diff_main = "none"

        @pl.when(before_last)
        def _():
            process([(main_tile, tk)], None, None)

        @pl.when(last)
        def _():
            process([(extra_tile, te), (main_tile, tk)], visible(fox_extra, fox_main, te + tk),
                    visible(diff_extra, diff_main, te + tk))
    else:
        last = j == 0
        process([(extra_tile, te)], visible(fox_extra, "none", te), visible(diff_extra, "none", te))

    @pl.when(last)
    def _():
        lam = lam_ref[...]
        upper = lax.broadcasted_iota(jnp.int32, (tq, 128), 1) >= HEAD_DIM
        outs = []
        for g in range(FOX_HEADS // 2):
            a0, a1 = accf_ref[2 * g], accf_ref[2 * g + 1]
            outs.append(jnp.where(upper, a1 / pltpu.roll(a1, HEAD_DIM, 1), a0 / pltpu.roll(a0, HEAD_DIM, 1)))
        for hd in range(DIFF_HEADS):
            a1, a2 = accd_ref[2 * hd], accd_ref[2 * hd + 1]
            o = a1[:, :DIFF_V] / a1[:, DIFF_V:] - lam * (a2[:, :DIFF_V] / a2[:, DIFF_V:])
            outs.append(_rms_scale(o) * gain_ref[...] * out_scale)
        out = jnp.concatenate(outs, axis=-1)
        if zero_pad_q:
            out = jnp.where(lax.broadcasted_iota(jnp.int32, out.shape, 0) >= pad_rows, out, 0.0)
        o_ref[...] = out.astype(BF16)


def attention(q_arrays, q_rows, main, extra, lam, gain, out_scale, bsz, tq, nq, tk, te,
              main_causal, fox_extra, diff_extra, zero_pad_q, q_block0=0, pad_rows=BLOCK - N_META):
    qf, qd, cumq = q_arrays
    has_main = main is not None
    nk = main[0].shape[1] // tk if has_main else 1
    kv_widths = (FOX_HEADS * 128, FOX_HEADS * 128, 2 * DIFF_HEADS * HEAD_DIM, DIFF_HEADS * 2 * DIFF_V)

    def qspec(width):
        return pl.BlockSpec((None, tq, width), lambda b, i, j: (b, i + q_block0, 0))

    in_specs = [qspec(FOX_HEADS * 128), qspec(2 * DIFF_HEADS * 128), qspec(128)]
    args = [qf, qd, cumq]
    if has_main:
        if main_causal:
            kmap = lambda b, i, j: (b, jnp.minimum(j, i), 0)
        else:
            kmap = lambda b, i, j: (b, j, 0)
        in_specs += [pl.BlockSpec((None, tk, w), kmap) for w in kv_widths]
        args += list(main)
    eblk = extra[4]
    emap = lambda b, i, j: (b, eblk, 0)
    in_specs += [pl.BlockSpec((None, te, w), emap) for w in kv_widths] + [_const_spec((1, 1)), _const_spec((1, DIFF_V))]
    args += list(extra[:4]) + [lam.reshape(1, 1).astype(F32), gain.reshape(1, DIFF_V).astype(F32)]
    width = FOX_HEADS * HEAD_DIM + DIFF_HEADS * DIFF_V
    return pl.pallas_call(
        functools.partial(_attn_kernel, tq=tq, tk=tk, te=te, has_main=has_main, main_causal=main_causal,
                          fox_extra=fox_extra, diff_extra=diff_extra, zero_pad_q=zero_pad_q, pad_rows=pad_rows,
                          out_scale=out_scale),
        grid=(bsz, nq, nk),
        in_specs=in_specs,
        out_specs=pl.BlockSpec((None, tq, width), lambda b, i, j: (b, i, 0)),
        out_shape=jax.ShapeDtypeStruct((bsz, nq * tq, width), BF16),
        scratch_shapes=[pltpu.VMEM((N_MAPS, tq, 128), F32), pltpu.VMEM((FOX_HEADS, tq, 128), F32),
                        pltpu.VMEM((FOX_HEADS, tq, 128), F32), pltpu.VMEM((2 * DIFF_HEADS, tq, 2 * DIFF_V), F32)],
        compiler_params=_cparams(("parallel", "parallel", "arbitrary")),
        name="attention",
    )(*args)


def _inproj_odd_kernel(x_ref, g_ref, w_ref, c_ref, s1_ref, s2_ref,
                       qfa_ref, kf_ref, vf_ref, vfa_ref, qda_ref, kd_ref, kdb_ref, vd_ref, vda_ref, f_ref):
    xn = (_rms_scale(x_ref[...]) * g_ref[...]).astype(BF16)
    width = FOX_HEADS * HEAD_DIM
    proj = lambda n: jnp.dot(xn, w_ref[:, n * width:(n + 1) * width], preferred_element_type=F32)
    qscale = HEAD_DIM ** -0.5 * LOG2E
    ones = lambda h, lane: jnp.where(lane % HEAD_DIM < N_BIAS_TERMS, 1.0, 0.0)
    qfa_ref[...] = _head_padded(proj(0), qscale, ones)
    kf_ref[...] = proj(1)
    vf = proj(2)
    vf_ref[...] = vf
    vfa_ref[...] = _head_padded(vf, 1.0, lambda h, lane: 1.0)
    qda_ref[...] = _head_padded(_rotary(proj(3), c_ref, s1_ref, s2_ref), qscale, lambda h, lane: 0.0)
    kd = _rotary(proj(4), c_ref, s1_ref, s2_ref)
    kd_ref[...] = kd
    kdb_ref[...] = kd.astype(BF16)
    vd = proj(5)
    vd_ref[...] = vd
    vdb = vd.astype(BF16)
    ones_v = jnp.ones((vdb.shape[0], DIFF_V), BF16)
    vda_ref[...] = jnp.concatenate(
        [a for hd in range(DIFF_HEADS) for a in (vdb[:, hd * DIFF_V:(hd + 1) * DIFF_V], ones_v)], axis=-1)
    f_ref[...] = jnp.dot(xn, w_ref[:, 6 * width:], preferred_element_type=F32)


def inproj_odd(x, gain, w, tables, bsz, tb, tm):
    width = FOX_HEADS * HEAD_DIM
    rows = lambda w_: pl.BlockSpec((None, tm, w_), lambda i, b: (b, i, 0))
    ts = pl.BlockSpec((tm, width), lambda i, b: (i, 0))
    shp = lambda w_, dt: jax.ShapeDtypeStruct((bsz, tb, w_), dt)
    outs = [(2 * width, BF16), (width, F32), (width, F32), (2 * width, BF16), (2 * width, BF16), (width, F32),
            (width, BF16), (width, F32), (2 * width, BF16), (128, F32)]
    return pl.pallas_call(
        _inproj_odd_kernel,
        grid=(tb // tm, bsz),
        in_specs=[rows(D_MODEL), _const_spec((1, D_MODEL)), _const_spec(w.shape, True), ts, ts, ts],
        out_specs=[rows(w_) for w_, _ in outs],
        out_shape=[shp(w_, dt) for w_, dt in outs],
        compiler_params=_cparams(("parallel", "parallel")),
        name="inproj_odd",
    )(x.reshape(bsz, tb, D_MODEL), gain.reshape(1, D_MODEL), w, *tables)


def pack_odd_w_in(w_in):
    c0 = 3 * FOX_HEADS * HEAD_DIM
    f = jnp.zeros((D_MODEL, 128), w_in.dtype).at[:, :FOX_HEADS].set(w_in[:, c0:c0 + FOX_HEADS])
    return jnp.concatenate([w_in[:, :c0], w_in[:, c0 + FOX_HEADS:], f], axis=1).astype(BF16)


def diff_lambda_value(diff_lambda, lam_init):
    lq1, lk1, lq2, lk2 = diff_lambda.astype(F32)
    return jnp.exp(jnp.sum(lq1 * lk1)) - jnp.exp(jnp.sum(lq2 * lk2)) + lam_init


def _logical_kernel(x_ref, *refs, seq):
    o_ref = refs[-1]
    o_ref[0:N_META, :] = x_ref[x_ref.shape[0] - N_META:, :]
    o_ref[N_META:, :] = x_ref[0:seq, :]


def slab_to_logical(x, seq, layer, nlayers, stacked=None):
    bsz, tb, width = x.shape
    x_spec = pl.BlockSpec((None, tb, width), lambda b: (b, 0, 0))
    return pl.pallas_call(
        functools.partial(_logical_kernel, seq=seq),
        grid=(bsz,),
        in_specs=[x_spec] if stacked is None else [x_spec, pl.BlockSpec(memory_space=pl.ANY)],
        out_specs=pl.BlockSpec((None, None, N_META + seq, width), lambda b: (layer, b, 0, 0)),
        out_shape=jax.ShapeDtypeStruct((nlayers, bsz, N_META + seq, width), x.dtype),
        input_output_aliases={} if stacked is None else {1: 0},
        compiler_params=_cparams(("parallel",)),
        name="slab_to_logical",
    )(*([x] if stacked is None else [x, stacked]))


def odd_mixer_prompt(x, gain, p, lam_init, bsz, tb, nframes, tm, tq):
    meta_blk = tb // BLOCK - 1
    pad = BLOCK - N_META
    pos = jnp.concatenate([N_META + jnp.arange(nframes, dtype=jnp.int32), jnp.zeros((pad,), jnp.int32),
                           jnp.arange(N_META, dtype=jnp.int32)])
    tables = rope_tables(pos, 2 * DIFF_HEADS)
    qfa, kf, vf, vfa, qda, kd, kdb, vd, vda, f = inproj_odd(x, gain, p["w_in"], tables, bsz, tb, tm // 2)
    segments = [(nframes, BLOCK, pad)] + [(s, tq, 0) for s in range(0, nframes, tq)]
    logf, cum, _ = gate_cumsum(f, p["f_bias"], jnp.zeros((bsz, 1, 128), F32), bsz, tb, segments, True)
    kfa = attn_prep_k(kf, cum, bsz, tb, tm)
    nk = nframes // tq
    lam = diff_lambda_value(p["diff_lambda"], lam_init)
    extra = (kfa, vfa, kdb, vda, meta_blk)
    common = dict(lam=lam, gain=p["diff_gain"], out_scale=1.0 - lam_init, bsz=bsz, te=BLOCK, diff_extra="valid")
    frames = attention((qfa, qda, cum), tb, (kfa, vfa, kdb, vda), extra, tq=tq, nq=nk, tk=tq,
                       main_causal=True, fox_extra="valid", zero_pad_q=False, **common)
    meta = attention((qfa, qda, cum), tb, None, extra, tq=BLOCK, nq=1, tk=BLOCK, main_causal=False,
                     fox_extra="causal_valid", zero_pad_q=True, q_block0=meta_blk, **common)
    mixed = jnp.concatenate([frames, meta], axis=1).reshape(bsz * tb, -1)
    return mixed, kf, vf, logf, kd, vd


def odd_mixer_sample(x, gain, p, lam_init, layer, kf_cache, vf_cache, lf_cache, kd_cache, vd_cache, bsz, tb, tk):
    past = kf_cache.shape[2]
    tables = rope_tables(past + jnp.arange(tb, dtype=jnp.int32), 2 * DIFF_HEADS)
    qfa, kf, vf, vfa, qda, kd, kdb, vd, vda, f = inproj_odd(x, gain, p["w_in"], tables, bsz, tb, tb)
    lf128 = jnp.zeros((bsz, past, 128), F32).at[:, :, :FOX_HEADS].set(lf_cache[layer].astype(F32))
    zero_c = jnp.zeros((bsz, 1, 128), F32)
    _, cum_past, total = gate_cumsum(lf128, p["f_bias"], zero_c, bsz, past, [(s, tk, 0) for s in range(0, past, tk)],
                                     False)
    logf, cum, _ = gate_cumsum(f, p["f_bias"], total, bsz, tb, [(0, tb, 0)], True)
    kfa = attn_prep_k(kf, cum, bsz, tb, tb)
    tview = lambda a: jnp.transpose(a.astype(F32), (0, 1, 3, 4, 2))
    vd_flat = vd_cache.astype(F32).reshape(vd_cache.shape[0], bsz, past, -1)
    main = tuple(attn_prep_kv_cached(tview(kf_cache), tview(vf_cache), tview(kd_cache), vd_flat, cum_past, layer,
                                     bsz, past, tk))
    extra = (kfa, vfa, kdb, vda, 0)
    mixed = attention((qfa, qda, cum), tb, main, extra, lam=diff_lambda_value(p["diff_lambda"], lam_init),
                      gain=p["diff_gain"], out_scale=1.0 - lam_init, bsz=bsz, tq=tb, nq=1, tk=tk, te=tb,
                      main_causal=False, fox_extra="causal", diff_extra="none", zero_pad_q=False)
    return mixed.reshape(bsz * tb, -1), kf, vf, logf, kd, vd


def kernel(x_prompt, x_sample, state_conv, state_delta, state_ssm_re, state_ssm_im, cache_fox_k, cache_fox_v,
           cache_fox_logf, cache_diff_k, cache_diff_v, meta_tokens, norm_mix, norm_ffn, norm_final, w_in_even,
           w_out_even, conv_w, gdn_a_log, gdn_dt_bias, gdn_out_norm, ssm_lambda_re, ssm_lambda_im, ssm_log_dt,
           ssm_b_re, ssm_b_im, ssm_c_re, ssm_c_im, ssm_d, ssm_glu_w, ssm_glu_b, w_in_odd, w_out_odd, fox_f_bias,
           diff_lambda, diff_out_norm, ffn_w1, ffn_w3, ffn_w2):
    bp, seq, _ = x_prompt.shape
    bs, ls, _ = x_sample.shape
    depth = norm_mix.shape[0]
    tb = seq + BLOCK
    pad = BLOCK - N_META
    tm_p = 832
    tq = 512
    meta = jnp.broadcast_to(meta_tokens.astype(F32)[None], (bp, N_META, D_MODEL))
    xp = jnp.concatenate([x_prompt.astype(F32), jnp.zeros((bp, pad, D_MODEL), F32), meta], axis=1)
    xp = xp.reshape(bp * tb, D_MODEL)
    xs = x_sample.astype(F32).reshape(bs * ls, D_MODEL)

    def logical(a):
        return jnp.concatenate([a[:, tb - N_META:], a[:, :seq]], axis=1)

    outs = {k: [] for k in ("conv_p", "conv_s", "delta_p", "delta_s", "re_p", "re_s", "im_p", "im_s", "fk_p", "fk_s",
                            "fv_p", "fv_s", "fl_p", "fl_s", "dk_p", "dk_s", "dv_p", "dv_s")}
    head_split = {"fk_p": (FOX_HEADS, HEAD_DIM), "fv_p": (FOX_HEADS, HEAD_DIM), "dk_p": (2 * DIFF_HEADS, HEAD_DIM),
                  "dv_p": (DIFF_HEADS, DIFF_V)}
    nchunk = FFN_HIDDEN // FFN_CHUNK
    for l in range(depth):
        i = l // 2
        if l % 2 == 0:
            pe = dict(w_in=pack_even_w_in(w_in_even[i]), conv_w=conv_w[i], a_log=gdn_a_log[i], dt_bias=gdn_dt_bias[i],
                      out_gain=gdn_out_norm[i], lam_re=ssm_lambda_re[i], lam_im=ssm_lambda_im[i],
                      log_dt=ssm_log_dt[i], b_re=ssm_b_re[i], b_im=ssm_b_im[i], c_re=ssm_c_re[i], c_im=ssm_c_im[i],
                      d=ssm_d[i], glu_w=ssm_glu_w[i], glu_b=ssm_glu_b[i])
            qkv_w = 3 * GDN_HEADS * GDN_DK
            mixed_p, qkv_p, sd_p, re_p, im_p = even_mixer(
                xp, norm_mix[l], pe, jnp.zeros((bp, CONV_W - 1, qkv_w), F32),
                jnp.zeros((bp, GDN_HEADS, GDN_DK, GDN_DK), F32), jnp.zeros((bp, SSM_GROUPS, SSM_STATE), F32),
                jnp.zeros((bp, SSM_GROUPS, SSM_STATE), F32), bp, tb, tm_p, BLOCK, tb // BLOCK - 1, 32)
            mixed_s, qkv_s, sd_s, re_s, im_s = even_mixer(
                xs, norm_mix[l], pe, state_conv[i], state_delta[i], state_ssm_re[i], state_ssm_im[i],
                bs, ls, ls, ls, 0, ls)
            outs["conv_p"].append(qkv_p.reshape(bp, tb, qkv_w)[:, seq - (CONV_W - 1):seq])
            outs["conv_s"].append(jnp.concatenate([state_conv[i].astype(F32), qkv_s.reshape(bs, ls, qkv_w)],
                                                  axis=1)[:, ls:])
            outs["delta_p"].append(sd_p); outs["delta_s"].append(sd_s)
            outs["re_p"].append(re_p); outs["re_s"].append(re_s)
            outs["im_p"].append(im_p); outs["im_s"].append(im_s)
            w_out = w_out_even[i]
        else:
            lam_init = 0.8 - 0.6 * math.exp(-0.3 * l)
            po = dict(w_in=pack_odd_w_in(w_in_odd[i]), diff_lambda=diff_lambda[i], diff_gain=diff_out_norm[i],
                      f_bias=jnp.zeros((1, 128), F32).at[0, :FOX_HEADS].set(fox_f_bias[i].astype(F32)))
            mixed_p, kf, vf, lf, kd, vd = odd_mixer_prompt(xp, norm_mix[l], po, lam_init, bp, tb, seq, tm_p, tq)
            mixed_p = [(mixed_p, mixed_p.shape[-1], "flat")]
            for key, slab in (("fk_p", kf), ("fv_p", vf), ("dk_p", kd), ("dv_p", vd)):
                outs[key].append(slab_to_logical(slab, seq, 0, 1).reshape((bp, N_META + seq) + head_split[key]))
            outs["fl_p"].append(logical(lf)[:, :, :FOX_HEADS])
            mixed_s, kf, vf, lf, kd, vd = odd_mixer_sample(
                xs, norm_mix[l], po, lam_init, i, cache_fox_k, cache_fox_v, cache_fox_logf, cache_diff_k,
                cache_diff_v, bs, ls, 512)
            mixed_s = [(mixed_s, mixed_s.shape[-1], "flat")]
            outs["fk_s"].append(kf.reshape(bs, ls, FOX_HEADS, HEAD_DIM))
            outs["fv_s"].append(vf.reshape(bs, ls, FOX_HEADS, HEAD_DIM))
            outs["fl_s"].append(lf[:, :, :FOX_HEADS])
            outs["dk_s"].append(kd.reshape(bs, ls, 2 * DIFF_HEADS, HEAD_DIM))
            outs["dv_s"].append(vd.reshape(bs, ls, DIFF_HEADS, DIFF_V))
            w_out = w_out_odd[i]
        w1 = ffn_w1[l].astype(BF16).reshape(D_MODEL, nchunk, FFN_CHUNK).transpose(1, 0, 2)
        w3 = ffn_w3[l].astype(BF16).reshape(D_MODEL, nchunk, FFN_CHUNK).transpose(1, 0, 2)
        w2 = ffn_w2[l].astype(BF16).reshape(nchunk, FFN_CHUNK, D_MODEL)
        xp = outproj_ffn(xp, mixed_p, w_out.astype(BF16), norm_ffn[l], w1, w3, w2, bp, tb, tm_p)
        xs = outproj_ffn(xs, mixed_s, w_out.astype(BF16), norm_ffn[l], w1, w3, w2, bs, ls, ls)

    y_prompt = final_norm(xp, norm_final, bp, tb, seq, 512).reshape(bp, seq, D_MODEL)
    y_sample = final_norm(xs, norm_final, bs, ls, ls, ls).reshape(bs, ls, D_MODEL)
    order = ("conv_p", "conv_s", "delta_p", "delta_s", "re_p", "re_s", "im_p", "im_s", "fk_p", "fk_s", "fv_p", "fv_s",
             "fl_p", "fl_s", "dk_p", "dk_s", "dv_p", "dv_s")
    return (y_prompt, y_sample) + tuple(jnp.stack(outs[k]) for k in order)
```
